```python
import jax, jax.numpy as jnp
from jax import lax
import numpy as np

D_MODEL = 1024
BATCH = 8
SEQ = 2048
DEPTH = 1
DEC_BATCH = 4
DEC_SEQ = 4096
PAST_LEN = 128

N_MEM = 256
A_HEADS = 8
A_KV_HEADS = 2
A_GROUP = A_HEADS // A_KV_HEADS
A_HEAD_DIM = 64
A_HALF_WIN = 128
B_GROUPS = ((128, 1), (512, 4), (2048, 16))
B_N_GROUPS = 3
B_HEADS_PER_GROUP = 4
B_HEAD_DIM = 128
M_HEADS = 4
M_HEAD_DIM = 128
N_BRANCH = 3
BRANCH_WIDTH = D_MODEL // 2
D_FF = 4 * D_MODEL
EPS = 1e-6
NEG_INF = -1e30

A_Q_W = A_HEADS * A_HEAD_DIM
A_KV_W = A_KV_HEADS * A_HEAD_DIM
B_W = B_N_GROUPS * B_HEADS_PER_GROUP * B_HEAD_DIM
M_Q_W = M_HEADS * M_HEAD_DIM
GATE_W = N_BRANCH * D_MODEL
PROJ_WIDTHS = (A_Q_W, A_KV_W, A_KV_W, B_W, B_W, B_W, M_Q_W, GATE_W)
IN_WIDTH = A_Q_W + 2 * A_KV_W + 3 * B_W + M_Q_W + GATE_W

kernel_name = 'hybrid_gated_parallel_encoder'


def rms_norm(x, gain):
    xf = x.astype(jnp.float32)
    y = xf * lax.rsqrt(jnp.mean(xf * xf, axis=-1, keepdims=True) + EPS) * gain.astype(jnp.float32)
    return y.astype(x.dtype)


def alibi_slopes(n):
    return jnp.asarray(2.0 ** (-8.0 * (np.arange(n) + 1) / n), dtype=jnp.float32)


def banded_attention(q, k, v, slopes, half_win, pos_stride, sink=None):
    n, length, hkv, grp, hd = q.shape
    block = half_win
    nb = -(-length // block)
    lp = nb * block
    extra = lp - length
    qb = jnp.pad(q, ((0, 0), (0, extra), (0, 0), (0, 0), (0, 0))).reshape(n, nb, block, hkv, grp, hd)
    kv_pad = ((0, 0), (block, block + extra), (0, 0), (0, 0))

    def windows(t):
        tb = jnp.pad(t, kv_pad).reshape(n, nb + 2, block, hkv, hd)
        return jnp.concatenate([tb[:, :-2], tb[:, 1:-1], tb[:, 2:]], axis=2)

    kw, vw = windows(k), windows(v)
    q_pos = jnp.arange(nb)[:, None] * block + jnp.arange(block)[None, :]
    k_pos = jnp.arange(nb)[:, None] * block - block + jnp.arange(3 * block)[None, :]
    dist = jnp.abs(q_pos[:, :, None] - k_pos[:, None, :])
    valid = (k_pos[:, None, :] >= 0) & (k_pos[:, None, :] < length) & (dist <= half_win)
    logits = jnp.einsum('nbqhgd,nbkhd->nbhgqk', qb, kw, preferred_element_type=jnp.float32) * (hd ** -0.5)
    penalty = slopes[None, :, :, None, None] * (dist * pos_stride).astype(jnp.float32)[:, None, None]
    logits = jnp.where(valid[:, None, None], logits - penalty, NEG_INF)
    m = jnp.max(logits, axis=-1)
    if sink is not None:
        sink_f = sink.astype(jnp.float32)[None, None, :, :, None]
        m = jnp.maximum(m, sink_f)
    p = jnp.exp(logits - m[..., None])
    denom = jnp.sum(p, axis=-1)
    if sink is not None:
        denom = denom + jnp.exp(sink_f - m)
    out = jnp.einsum('nbhgqk,nbkhd->nbhgqd', p, vw.astype(jnp.float32)) / denom[..., None]
    out = out.transpose(0, 1, 4, 2, 3, 5).reshape(n, lp, hkv, grp, hd)[:, :length]
    lse = (m + jnp.log(denom)).transpose(0, 1, 4, 2, 3).reshape(n, lp, hkv, grp)[:, :length]
    return out, lse


def dilated_attention(q, k, v, slopes):
    n, s, _, hb, hd = q.shape
    outs, lses = [], []
    for gi, (window, dil) in enumerate(B_GROUPS):
        sub = s // dil

        def gather(t):
            return t[:, :, gi].reshape(n, sub, dil, hb, hd).transpose(0, 2, 1, 3, 4).reshape(n * dil, sub, hb, hd)

        o, lse = banded_attention(gather(q)[:, :, :, None], gather(k), gather(v),
                                  slopes[gi][:, None], window // (2 * dil), dil)
        outs.append(o[:, :, :, 0].reshape(n, dil, sub, hb, hd).transpose(0, 2, 1, 3, 4).reshape(n, s, hb, hd))
        lses.append(lse[:, :, :, 0].reshape(n, dil, sub, hb).transpose(0, 2, 1, 3).reshape(n, s, hb))
    weights = jax.nn.softmax(jnp.stack(lses), axis=0)
    return jnp.einsum('gnsh,gnshd->nshd', weights, jnp.stack(outs))


def memory_attention(q, k, v):
    logits = jnp.einsum('nshd,nmhd->nhsm', q, k, preferred_element_type=jnp.float32) * (q.shape[-1] ** -0.5)
    p = jax.nn.softmax(logits, axis=-1)
    return jnp.einsum('nhsm,nmhd->nshd', p, v.astype(jnp.float32))


def encoder_layer(x, mem, g_mix, g_mem, w_in, b_gate, w_mem_kv, gq_a, gk_a, sink_a,
                  gq_b, gk_b, gq_m, gk_m, w_branch, w_out, g_mlp, w_up, w_down):
    n, s, _ = x.shape
    h = rms_norm(x, g_mix)
    proj = h @ w_in
    split_at = [int(i) for i in np.cumsum(PROJ_WIDTHS)[:-1]]
    qa, ka, va, qb, kb, vb, qm, gate_logits = jnp.split(proj, split_at, axis=-1)

    qa = rms_norm(qa.reshape(n, s, A_KV_HEADS, A_GROUP, A_HEAD_DIM), gq_a)
    ka = rms_norm(ka.reshape(n, s, A_KV_HEADS, A_HEAD_DIM), gk_a)
    va = va.reshape(n, s, A_KV_HEADS, A_HEAD_DIM)
    oa, _ = banded_attention(qa, ka, va, alibi_slopes(A_HEADS).reshape(A_KV_HEADS, A_GROUP),
                             A_HALF_WIN, 1, sink=sink_a.reshape(A_KV_HEADS, A_GROUP))
    oa = oa.reshape(n, s, A_Q_W)

    shp_b = (n, s, B_N_GROUPS, B_HEADS_PER_GROUP, B_HEAD_DIM)
    qb = rms_norm(qb.reshape(shp_b), gq_b)
    kb = rms_norm(kb.reshape(shp_b), gk_b)
    vb = vb.reshape(shp_b)
    ob = dilated_attention(qb, kb, vb, alibi_slopes(B_N_GROUPS * B_HEADS_PER_GROUP).reshape(B_N_GROUPS, B_HEADS_PER_GROUP))
    ob = ob.reshape(n, s, B_HEADS_PER_GROUP * B_HEAD_DIM)

    mh = rms_norm(mem, g_mem)
    mk, mv = jnp.split(mh @ w_mem_kv, 2, axis=-1)
    qm = rms_norm(qm.reshape(n, s, M_HEADS, M_HEAD_DIM), gq_m)
    mk = rms_norm(mk.reshape(n, N_MEM, M_HEADS, M_HEAD_DIM), gk_m)
    mv = mv.reshape(n, N_MEM, M_HEADS, M_HEAD_DIM)
    om = memory_attention(qm, mk, mv).reshape(n, s, M_HEADS * M_HEAD_DIM)

    branches = jnp.stack([oa, ob, om], axis=2).astype(x.dtype)
    y_br = jnp.einsum('nsbc,bcd->nsbd', branches, w_branch)
    gates = jax.nn.sigmoid(gate_logits.reshape(n, s, N_BRANCH, D_MODEL) + b_gate)
    x = x + jnp.sum(gates * y_br, axis=2) @ w_out

    h2 = rms_norm(x, g_mlp)
    x = x + jnp.square(jax.nn.relu(h2 @ w_up)) @ w_down
    return x


def setup_inputs(seed: int = 0) -> dict:
    key = jax.random.key(seed)
    ks = jax.random.split(key, 24)
    f32 = jnp.float32

    def nrm(k, shape, scale):
        return jax.random.normal(k, shape, f32) * scale

    def gain(k, shape):
        return 1.0 + 0.05 * jax.random.normal(k, shape, f32)

    return {
        'x_prompt': nrm(ks[0], (BATCH, SEQ, D_MODEL), 1.0),
        'x_sample': nrm(ks[1], (DEC_BATCH, DEC_SEQ, D_MODEL), 1.0),
        'mem_prompt': nrm(ks[2], (BATCH, N_MEM, D_MODEL), 1.0),
        'mem_sample': nrm(ks[3], (DEC_BATCH, N_MEM, D_MODEL), 1.0),
        'g_mix': gain(ks[4], (DEPTH, D_MODEL)),
        'g_mem': gain(ks[5], (DEPTH, D_MODEL)),
        'w_in': nrm(ks[6], (DEPTH, D_MODEL, IN_WIDTH), D_MODEL ** -0.5),
        'b_gate': nrm(ks[7], (DEPTH, N_BRANCH, D_MODEL), 0.1),
        'w_mem_kv': nrm(ks[8], (DEPTH, D_MODEL, 2 * M_HEADS * M_HEAD_DIM), D_MODEL ** -0.5),
        'gq_a': gain(ks[9], (DEPTH, A_HEAD_DIM)),
        'gk_a': gain(ks[10], (DEPTH, A_HEAD_DIM)),
        'sink_a': nrm(ks[11], (DEPTH, A_HEADS), 0.5),
        'gq_b': gain(ks[12], (DEPTH, B_HEAD_DIM)),
        'gk_b': gain(ks[13], (DEPTH, B_HEAD_DIM)),
        'gq_m': gain(ks[14], (DEPTH, M_HEAD_DIM)),
        'gk_m': gain(ks[15], (DEPTH, M_HEAD_DIM)),
        'w_branch': nrm(ks[16], (DEPTH, N_BRANCH, BRANCH_WIDTH, D_MODEL), BRANCH_WIDTH ** -0.5),
        'w_out': nrm(ks[17], (DEPTH, D_MODEL, D_MODEL), D_MODEL ** -0.5),
        'g_mlp': gain(ks[18], (DEPTH, D_MODEL)),
        'w_up': nrm(ks[19], (DEPTH, D_MODEL, D_FF), D_MODEL ** -0.5),
        'w_down': nrm(ks[20], (DEPTH, D_FF, D_MODEL), D_FF ** -0.5),
    }


def reference(x_prompt, x_sample, mem_prompt, mem_sample, g_mix, g_mem, w_in, b_gate, w_mem_kv,
              gq_a, gk_a, sink_a, gq_b, gk_b, gq_m, gk_m, w_branch, w_out, g_mlp, w_up, w_down):
    def run(x, mem):
        for l in range(DEPTH):
            x = encoder_layer(x, mem, g_mix[l], g_mem[l], w_in[l], b_gate[l], w_mem_kv[l],
                              gq_a[l], gk_a[l], sink_a[l], gq_b[l], gk_b[l], gq_m[l], gk_m[l],
                              w_branch[l], w_out[l], g_mlp[l], w_up[l], w_down[l])
        return x

    y_prompt = run(x_prompt, mem_prompt)
    y_sample = run(x_sample, mem_sample)
    return (y_prompt, y_sample)
```

```python
import functools

import numpy as np
import jax
import jax.numpy as jnp
from jax import lax
from jax.experimental import pallas as pl
from jax.experimental.pallas import tpu as pltpu

D_MODEL = 1024
N_MEM = 256
A_HEADS = 8
A_KV_HEADS = 2
A_HEAD_DIM = 64
A_HALF_WIN = 128
B_GROUPS = ((128, 1), (512, 4), (2048, 16))
B_HEADS_PER_GROUP = 4
B_HEAD_DIM = 128
M_HEADS = 4
M_HEAD_DIM = 128
N_BRANCH = 3
BRANCH_WIDTH = D_MODEL // 2
D_FF = 4 * D_MODEL
EPS = 1e-6
NEG_INF = -1e30

A_Q_W = A_HEADS * A_HEAD_DIM
A_KV_W = A_KV_HEADS * A_HEAD_DIM
B_GROUP_W = B_HEADS_PER_GROUP * B_HEAD_DIM
B_W = len(B_GROUPS) * B_GROUP_W
M_W = M_HEADS * M_HEAD_DIM
GATE_W = N_BRANCH * D_MODEL

LANES = 128
B_HALF_WIN = 64
VMEM_LIMIT_BYTES = 56 * 1024 * 1024

BF16 = jnp.bfloat16
F32 = jnp.float32


def _alibi_slopes(n):
    return [float(2.0 ** (-8.0 * (i + 1) / n)) for i in range(n)]


def _params(n_grid_axes):
    return pltpu.CompilerParams(
        dimension_semantics=("arbitrary",) * n_grid_axes,
        vmem_limit_bytes=VMEM_LIMIT_BYTES)


def _resident(shape):
    zeros = (0,) * len(shape)
    return pl.BlockSpec(shape, lambda *_: zeros, pipeline_mode=pl.Buffered(1))


def _rms_rows(x, gain):
    ms = jnp.mean(x * x, axis=-1, keepdims=True)
    return x * lax.rsqrt(ms + EPS) * gain


def _head_norm128(blk, gain):
    ms = jnp.sum(blk * blk, axis=-1, keepdims=True) * (1.0 / LANES)
    return blk * lax.rsqrt(ms + EPS) * gain


def _head_norm64(blk, gain2):
    low = lax.broadcasted_iota(jnp.int32, blk.shape, 1) < A_HEAD_DIM
    sq = blk * blk
    ss_lo = jnp.sum(jnp.where(low, sq, 0.0), axis=-1, keepdims=True)
    ss_hi = jnp.sum(jnp.where(low, 0.0, sq), axis=-1, keepdims=True)
    ms = jnp.where(low, ss_lo, ss_hi) * (1.0 / A_HEAD_DIM)
    return blk * lax.rsqrt(ms + EPS) * gain2


PROJ_ROWS = 512
PROJ_COLS = 512


def _proj_kernel(x_ref, g_ref, wqa_ref, wkva_ref, wqb_ref, wkb_ref, wvb_ref, wqm_ref,
                 gqa_ref, gka_ref, gqb_ref, gkb_ref, gqm_ref,
                 qa_ref, ka_ref, va_ref, qb_ref, kb_ref, vb_ref, qm_ref):
    h = _rms_rows(x_ref[...], g_ref[...]).astype(BF16)

    def mm(w_ref, lo, hi):
        return jnp.dot(h, w_ref[:, lo:hi], preferred_element_type=F32)

    y = mm(wqa_ref, 0, A_Q_W)
    for c in range(A_Q_W // LANES):
        sl = slice(c * LANES, (c + 1) * LANES)
        qa_ref[:, sl] = _head_norm64(y[:, sl], gqa_ref[...]).astype(BF16)

    y = mm(wkva_ref, 0, 2 * A_KV_W)
    ka_ref[...] = _head_norm64(y[:, :A_KV_W], gka_ref[...]).astype(BF16)
    va_ref[...] = y[:, A_KV_W:].astype(BF16)

    def normed(w_ref, gain_ref, o_ref, width):
        for lo in range(0, width, PROJ_COLS):
            y = mm(w_ref, lo, lo + PROJ_COLS)
            for c in range(PROJ_COLS // LANES):
                sl = slice(c * LANES, (c + 1) * LANES)
                o_ref[:, lo + c * LANES:lo + (c + 1) * LANES] = (
                    _head_norm128(y[:, sl], gain_ref[...]).astype(BF16))

    normed(wqb_ref, gqb_ref, qb_ref, B_W)
    normed(wkb_ref, gkb_ref, kb_ref, B_W)
    for lo in range(0, B_W, PROJ_COLS):
        vb_ref[:, lo:lo + PROJ_COLS] = mm(wvb_ref, lo, lo + PROJ_COLS).astype(BF16)
    normed(wqm_ref, gqm_ref, qm_ref, M_W)


def _proj(x2d, g_mix, ws, gains):
    t = x2d.shape[0]
    tm = PROJ_ROWS
    widths = (A_Q_W, A_KV_W, A_KV_W, B_W, B_W, B_W, M_W)
    row = lambda i: (i, 0)
    in_specs = ([pl.BlockSpec((tm, D_MODEL), row), _resident((1, D_MODEL))]
                + [_resident(w.shape) for w in ws]
                + [_resident((1, LANES)) for _ in gains])
    return pl.pallas_call(
        _proj_kernel,
        grid=(t // tm,),
        in_specs=in_specs,
        out_specs=[pl.BlockSpec((tm, w), row) for w in widths],
        out_shape=[jax.ShapeDtypeStruct((t, w), BF16) for w in widths],
        compiler_params=_params(1),
        name="proj",
    )(x2d, g_mix, *ws, *gains)


def _mem_kv_kernel(m_ref, g_ref, w_ref, gk_ref, k_ref, v_ref):
    h = _rms_rows(m_ref[...], g_ref[...]).astype(BF16)
    y = jnp.dot(h, w_ref[...], preferred_element_type=F32)
    for c in range(M_HEADS):
        sl = slice(c * LANES, (c + 1) * LANES)
        k_ref[:, sl] = _head_norm128(y[:, sl], gk_ref[...]).astype(BF16)
    v_ref[...] = y[:, M_W:].astype(BF16)


def _mem_kv(mem2d, g_mem, w_mem_kv, gk_m):
    t = mem2d.shape[0]
    row = lambda i: (i, 0)
    return pl.pallas_call(
        _mem_kv_kernel,
        grid=(t // N_MEM,),
        in_specs=[pl.BlockSpec((N_MEM, D_MODEL), row), _resident((1, D_MODEL)),
                  _resident(w_mem_kv.shape), _resident((1, LANES))],
        out_specs=[pl.BlockSpec((N_MEM, M_W), row)] * 2,
        out_shape=[jax.ShapeDtypeStruct((t, M_W), BF16)] * 2,
        compiler_params=_params(1),
        name="mem_kv",
    )(mem2d, g_mem, w_mem_kv, gk_m)


A_BLOCK = A_HALF_WIN
A_STEP = 256


def _attn_a_kernel(seq_len, q_ref, kp_ref, kc_ref, kn_ref, vp_ref, vc_ref, vn_ref, sink_ref,
                   o_ref, kbuf, vbuf):
    i = pl.program_id(1)
    slopes = _alibi_slopes(A_HEADS)
    for buf, prev, cur, nxt in ((kbuf, kp_ref, kc_ref, kn_ref), (vbuf, vp_ref, vc_ref, vn_ref)):
        buf[0:A_BLOCK, :] = prev[0]
        buf[A_BLOCK:A_BLOCK + A_STEP, :] = cur[0]
        buf[A_BLOCK + A_STEP:, :] = nxt[0]

    n_keys = 3 * A_BLOCK
    q_row = lax.broadcasted_iota(jnp.int32, (A_BLOCK, n_keys), 0)
    k_col = lax.broadcasted_iota(jnp.int32, (A_BLOCK, n_keys), 1)
    dist = jnp.abs(q_row + A_BLOCK - k_col)
    dist_f = dist.astype(F32)
    low = lax.broadcasted_iota(jnp.int32, (n_keys, LANES), 1) < A_HEAD_DIM
    low_q = lax.broadcasted_iota(jnp.int32, (A_BLOCK, LANES), 1) < A_HEAD_DIM
    zero = jnp.zeros((n_keys, LANES), BF16)

    def swap_halves(t):
        return jnp.concatenate([t[:, A_HEAD_DIM:], t[:, :A_HEAD_DIM]], axis=1)

    for j in range(A_STEP // A_BLOCK):
        k_pos = i * A_STEP + (j - 1) * A_BLOCK + k_col
        mask = (dist <= A_HALF_WIN) & (k_pos >= 0) & (k_pos < seq_len)
        kw = kbuf[j * A_BLOCK:j * A_BLOCK + n_keys, :]
        vw = vbuf[j * A_BLOCK:j * A_BLOCK + n_keys, :]
        kw_sw, vw_sw = swap_halves(kw), swap_halves(vw)
        for kvh in range(A_KV_HEADS):
            k_lo, k_hi = (kw, kw_sw) if kvh == 0 else (kw_sw, kw)
            v_lo, v_hi = (vw, vw_sw) if kvh == 0 else (vw_sw, vw)
            kk = jnp.concatenate([jnp.where(low, k_lo, zero), jnp.where(low, zero, k_hi)], axis=0)
            vv = jnp.concatenate([jnp.where(low, v_lo, zero), jnp.where(low, zero, v_hi)], axis=0)
            for c in range(2):
                tile = kvh * 2 + c
                sl = slice(tile * LANES, (tile + 1) * LANES)
                q = q_ref[0, j * A_BLOCK:(j + 1) * A_BLOCK, sl]
                s = lax.dot_general(q, kk, (((1,), (1,)), ((), ())), preferred_element_type=F32)
                ps, rdens = [], []
                for half in range(2):
                    head = 2 * tile + half
                    sh = s[:, half * n_keys:(half + 1) * n_keys]
                    sh = jnp.where(mask, sh - slopes[head] * dist_f, NEG_INF)
                    sink = sink_ref[head]
                    m = jnp.maximum(jnp.max(sh, axis=-1, keepdims=True), sink)
                    p = jnp.exp(sh - m)
                    den = jnp.sum(p, axis=-1, keepdims=True) + jnp.exp(sink - m)
                    ps.append(p.astype(BF16))
                    rdens.append(1.0 / den)
                o = jnp.dot(jnp.concatenate(ps, axis=1), vv, preferred_element_type=F32)
                o = o * jnp.where(low_q, rdens[0], rdens[1])
                o_ref[0, j * A_BLOCK:(j + 1) * A_BLOCK, sl] = o.astype(BF16)


def _attn_a(qa, ka, va, sink):
    n, s, _ = qa.shape
    per_step = A_STEP // A_BLOCK
    last = s // A_BLOCK - 1
    cur = lambda b, i: (b, i, 0)
    prev = lambda b, i: (b, jnp.maximum(i * per_step - 1, 0), 0)
    nxt = lambda b, i: (b, jnp.minimum((i + 1) * per_step, last), 0)
    kv_specs = [pl.BlockSpec((1, A_BLOCK, A_KV_W), prev), pl.BlockSpec((1, A_STEP, A_KV_W), cur),
                pl.BlockSpec((1, A_BLOCK, A_KV_W), nxt)]
    return pl.pallas_call(
        functools.partial(_attn_a_kernel, s),
        grid=(n, s // A_STEP),
        in_specs=[pl.BlockSpec((1, A_STEP, A_Q_W), cur)] + kv_specs + kv_specs
                 + [pl.BlockSpec(memory_space=pltpu.SMEM)],
        out_specs=pl.BlockSpec((1, A_STEP, A_Q_W), cur),
        out_shape=jax.ShapeDtypeStruct((n, s, A_Q_W), BF16),
        scratch_shapes=[pltpu.VMEM((A_STEP + 2 * A_BLOCK, A_KV_W), BF16)] * 2,
        compiler_params=_params(2),
        name="attn_a",
    )(qa, ka, ka, ka, va, va, va, sink)


B_BLOCK = B_HALF_WIN
B_STEP = 128


def _attn_b_kernel(sub_len, slopes, q_ref, kp_ref, kc_ref, kn_ref, vp_ref, vc_ref, vn_ref,
                   o_ref, lse_ref, kbuf, vbuf):
    i = pl.program_id(2)
    for buf, prev, cur, nxt in ((kbuf, kp_ref, kc_ref, kn_ref), (vbuf, vp_ref, vc_ref, vn_ref)):
        buf[0:B_BLOCK, :] = prev[0]
        buf[B_BLOCK:B_BLOCK + B_STEP, :] = cur[0]
        buf[B_BLOCK + B_STEP:, :] = nxt[0]

    n_keys = 3 * B_BLOCK
    q_row = lax.broadcasted_iota(jnp.int32, (B_BLOCK, n_keys), 0)
    k_col = lax.broadcasted_iota(jnp.int32, (B_BLOCK, n_keys), 1)
    dist = jnp.abs(q_row + B_BLOCK - k_col)
    dist_f = dist.astype(F32)
    lane = lax.broadcasted_iota(jnp.int32, (B_BLOCK, LANES), 1)
    lanes_per_head = LANES // B_HEADS_PER_GROUP

    for j in range(B_STEP // B_BLOCK):
        k_idx = i * B_STEP + (j - 1) * B_BLOCK + k_col
        mask = (dist <= B_HALF_WIN) & (k_idx >= 0) & (k_idx < sub_len)
        rows = slice(j * B_BLOCK, (j + 1) * B_BLOCK)
        lse_tile = jnp.zeros((B_BLOCK, LANES), F32)
        for h in range(B_HEADS_PER_GROUP):
            sl = slice(h * B_HEAD_DIM, (h + 1) * B_HEAD_DIM)
            q = q_ref[0, rows, sl]
            k = kbuf[j * B_BLOCK:j * B_BLOCK + n_keys, sl]
            v = vbuf[j * B_BLOCK:j * B_BLOCK + n_keys, sl]
            s = lax.dot_general(q, k, (((1,), (1,)), ((), ())), preferred_element_type=F32)
            s = jnp.where(mask, s - slopes[h] * dist_f, NEG_INF)
            m = jnp.max(s, axis=-1, keepdims=True)
            p = jnp.exp(s - m)
            den = jnp.sum(p, axis=-1, keepdims=True)
            o = jnp.dot(p.astype(BF16), v, preferred_element_type=F32) * (1.0 / den)
            o_ref[0, rows, sl] = o.astype(BF16)
            lse = m + jnp.log(den)
            lse_tile = jnp.where(lane >= h * lanes_per_head, lse, lse_tile)
        lse_ref[0, rows, :] = lse_tile


def _attn_b_group(qb, kb, vb, gi):
    n, s, _ = qb.shape
    _, dil = B_GROUPS[gi]
    sub = s // dil
    blocks_per_pos = B_W // B_GROUP_W
    view = lambda t: t.reshape(n, sub, dil * B_W)
    per_step = B_STEP // B_BLOCK
    last = sub // B_BLOCK - 1
    col = lambda r: r * blocks_per_pos + gi
    cur = lambda b, r, i: (b, i, col(r))
    prev = lambda b, r, i: (b, jnp.maximum(i * per_step - 1, 0), col(r))
    nxt = lambda b, r, i: (b, jnp.minimum((i + 1) * per_step, last), col(r))
    out = lambda b, r, i: (b, i, r)
    kv_specs = [pl.BlockSpec((1, B_BLOCK, B_GROUP_W), prev), pl.BlockSpec((1, B_STEP, B_GROUP_W), cur),
                pl.BlockSpec((1, B_BLOCK, B_GROUP_W), nxt)]
    all_slopes = _alibi_slopes(len(B_GROUPS) * B_HEADS_PER_GROUP)
    slopes = [all_slopes[gi * B_HEADS_PER_GROUP + h] * dil for h in range(B_HEADS_PER_GROUP)]
    o, lse = pl.pallas_call(
        functools.partial(_attn_b_kernel, sub, slopes),
        grid=(n, dil, sub // B_STEP),
        in_specs=[pl.BlockSpec((1, B_STEP, B_GROUP_W), cur)] + kv_specs + kv_specs,
        out_specs=[pl.BlockSpec((1, B_STEP, B_GROUP_W), out), pl.BlockSpec((1, B_STEP, LANES), out)],
        out_shape=[jax.ShapeDtypeStruct((n, sub, dil * B_GROUP_W), BF16),
                   jax.ShapeDtypeStruct((n, sub, dil * LANES), F32)],
        scratch_shapes=[pltpu.VMEM((B_STEP + 2 * B_BLOCK, B_GROUP_W), BF16)] * 2,
        compiler_params=_params(3),
        name=f"attn_b{gi}",
    )(view(qb), view(kb), view(kb), view(kb), view(vb), view(vb), view(vb))
    return o.reshape(n * s, B_GROUP_W), lse.reshape(n * s, LANES)


M_STEP = 512


def _attn_m_kernel(q_ref, k_ref, v_ref, o_ref):
    for h in range(M_HEADS):
        sl = slice(h * M_HEAD_DIM, (h + 1) * M_HEAD_DIM)
        s = lax.dot_general(q_ref[0, :, sl], k_ref[0, :, sl], (((1,), (1,)), ((), ())),
                            preferred_element_type=F32)
        m = jnp.max(s, axis=-1, keepdims=True)
        p = jnp.exp(s - m)
        den = jnp.sum(p, axis=-1, keepdims=True)
        o = jnp.dot(p.astype(BF16), v_ref[0, :, sl], preferred_element_type=F32) * (1.0 / den)
        o_ref[0, :, sl] = o.astype(BF16)


def _attn_m(qm, mk, mv):
    n, s, _ = qm.shape
    cur = lambda b, i: (b, i, 0)
    mem = lambda b, i: (b, 0, 0)
    return pl.pallas_call(
        _attn_m_kernel,
        grid=(n, s // M_STEP),
        in_specs=[pl.BlockSpec((1, M_STEP, M_W), cur), pl.BlockSpec((1, N_MEM, M_W), mem),
                  pl.BlockSpec((1, N_MEM, M_W), mem)],
        out_specs=pl.BlockSpec((1, M_STEP, M_W), cur),
        out_shape=jax.ShapeDtypeStruct((n, s, M_W), BF16),
        compiler_params=_params(2),
        name="attn_m",
    )(qm, mk, mv)


MERGE_ROWS = 512


def _merge_kernel(x_ref, g_ref, oa_ref, o0_ref, o1_ref, o2_ref, l0_ref, l1_ref, l2_ref, om_ref,
                  wg_ref, bg_ref, wbr_ref, wout_ref, y_ref):
    x = x_ref[...]
    h = _rms_rows(x, g_ref[...]).astype(BF16)

    lses = [l0_ref[...], l1_ref[...], l2_ref[...]]
    top = jnp.maximum(jnp.maximum(lses[0], lses[1]), lses[2])
    es = [jnp.exp(l - top) for l in lses]
    r = 1.0 / (es[0] + es[1] + es[2])
    ws = [e * r for e in es]
    lanes_per_head = LANES // B_HEADS_PER_GROUP
    ob_heads = []
    for hd in range(B_HEADS_PER_GROUP):
        sl = slice(hd * B_HEAD_DIM, (hd + 1) * B_HEAD_DIM)
        acc = None
        for w, o_ref in zip(ws, (o0_ref, o1_ref, o2_ref)):
            wh = w[:, hd * lanes_per_head:hd * lanes_per_head + 1]
            term = wh * o_ref[:, sl].astype(F32)
            acc = term if acc is None else acc + term
        ob_heads.append(acc.astype(BF16))
    ob = jnp.concatenate(ob_heads, axis=1)

    z = None
    for b, o in enumerate((oa_ref[...], ob, om_ref[...])):
        logits = jnp.dot(h, wg_ref[:, b * D_MODEL:(b + 1) * D_MODEL], preferred_element_type=F32)
        gate = jax.nn.sigmoid(logits + bg_ref[b:b + 1, :])
        term = gate * jnp.dot(o, wbr_ref[b], preferred_element_type=F32)
        z = term if z is None else z + term
    y_ref[...] = x + jnp.dot(z.astype(BF16), wout_ref[...], preferred_element_type=F32)


def _merge(x2d, g_mix, oa, obs, lses, om, w_gate, b_gate, w_branch, w_out):
    t = x2d.shape[0]
    tm = MERGE_ROWS
    row = lambda i: (i, 0)
    rows = lambda w: pl.BlockSpec((tm, w), row)
    return pl.pallas_call(
        _merge_kernel,
        grid=(t // tm,),
        in_specs=[rows(D_MODEL), _resident((1, D_MODEL)), rows(A_Q_W)]
                 + [rows(B_GROUP_W)] * 3 + [rows(LANES)] * 3 + [rows(M_W)]
                 + [_resident(w_gate.shape), _resident(b_gate.shape), _resident(w_branch.shape),
                    _resident(w_out.shape)],
        out_specs=rows(D_MODEL),
        out_shape=jax.ShapeDtypeStruct((t, D_MODEL), F32),
        compiler_params=_params(1),
        name="merge",
    )(x2d, g_mix, oa, *obs, *lses, om, w_gate, b_gate, w_branch, w_out)


MLP_ROWS = 512
MLP_FF_CHUNK = 1024


def _mlp_kernel(x_ref, g_ref, wup_ref, wdn_ref, y_ref):
    x = x_ref[...]
    h = _rms_rows(x, g_ref[...]).astype(BF16)
    acc = x
    for lo in range(0, D_FF, MLP_FF_CHUNK):
        u = jnp.dot(h, wup_ref[:, lo:lo + MLP_FF_CHUNK], preferred_element_type=F32)
        a = jnp.square(jnp.maximum(u, 0.0)).astype(BF16)
        acc = acc + jnp.dot(a, wdn_ref[lo:lo + MLP_FF_CHUNK, :], preferred_element_type=F32)
    y_ref[...] = acc


def _mlp(x2d, g_mlp, w_up, w_down):
    t = x2d.shape[0]
    tm = MLP_ROWS
    row = lambda i: (i, 0)
    return pl.pallas_call(
        _mlp_kernel,
        grid=(t // tm,),
        in_specs=[pl.BlockSpec((tm, D_MODEL), row), _resident((1, D_MODEL)),
                  _resident(w_up.shape), _resident(w_down.shape)],
        out_specs=pl.BlockSpec((tm, D_MODEL), row),
        out_shape=jax.ShapeDtypeStruct((t, D_MODEL), F32),
        compiler_params=_params(1),
        name="mlp",
    )(x2d, g_mlp, w_up, w_down)


def _tile2(g):
    return jnp.concatenate([g, g]).reshape(1, LANES)


def _layer(x, mem, g_mix, g_mem, w_in, b_gate, w_mem_kv, gq_a, gk_a, sink_a, gq_b, gk_b, gq_m, gk_m,
           w_branch, w_out, g_mlp, w_up, w_down):
    n, s, _ = x.shape
    x2d = x.reshape(n * s, D_MODEL)
    row = lambda g: g.reshape(1, -1)

    bounds = np.cumsum((0, A_Q_W, A_KV_W, A_KV_W, B_W, B_W, B_W, M_W, GATE_W))
    seg = lambda a, b: w_in[:, int(bounds[a]):int(bounds[b])].astype(BF16)
    ws = (seg(0, 1), seg(1, 3), seg(3, 4), seg(4, 5), seg(5, 6), seg(6, 7))
    w_gate = seg(7, 8)
    gains = (_tile2(gq_a) * (A_HEAD_DIM ** -0.5), _tile2(gk_a),
             row(gq_b) * (B_HEAD_DIM ** -0.5), row(gk_b), row(gq_m) * (M_HEAD_DIM ** -0.5))

    qa, ka, va, qb, kb, vb, qm = _proj(x2d, row(g_mix), ws, gains)
    mk, mv = _mem_kv(mem.reshape(n * N_MEM, D_MODEL), row(g_mem), w_mem_kv.astype(BF16), row(gk_m))

    r3 = lambda t: t.reshape(n, s, t.shape[-1])
    oa = _attn_a(r3(qa), r3(ka), r3(va), sink_a).reshape(n * s, A_Q_W)
    b_out = [_attn_b_group(r3(qb), r3(kb), r3(vb), gi) for gi in range(len(B_GROUPS))]
    om = _attn_m(r3(qm), mk.reshape(n, N_MEM, M_W), mv.reshape(n, N_MEM, M_W)).reshape(n * s, M_W)

    x1 = _merge(x2d, row(g_mix), oa, [o for o, _ in b_out], [l for _, l in b_out], om,
                w_gate, b_gate, w_branch.astype(BF16), w_out.astype(BF16))
    y = _mlp(x1, row(g_mlp), w_up.astype(BF16), w_down.astype(BF16))
    return y.reshape(n, s, D_MODEL)


def kernel(x_prompt, x_sample, mem_prompt, mem_sample, g_mix, g_mem, w_in, b_gate, w_mem_kv, gq_a, gk_a,
           sink_a, gq_b, gk_b, gq_m, gk_m, w_branch, w_out, g_mlp, w_up, w_down):
    depth = w_in.shape[0]

    def run(x, mem):
        for l in range(depth):
            x = _layer(x, mem, g_mix[l], g_mem[l], w_in[l], b_gate[l], w_mem_kv[l], gq_a[l], gk_a[l],
                       sink_a[l], gq_b[l], gk_b[l], gq_m[l], gk_m[l], w_branch[l], w_out[l], g_mlp[l],
                       w_up[l], w_down[l])
        return x

    return (run(x_prompt, mem_prompt), run(x_sample, mem_sample))
```

```python
import functools

import numpy as np
import jax
import jax.numpy as jnp
from jax import lax
from jax.experimental import pallas as pl
from jax.experimental.pallas import tpu as pltpu

D_MODEL = 1024
N_MEM = 256
A_HEADS = 8
A_KV_HEADS = 2
A_HEAD_DIM = 64
A_HALF_WIN = 128
B_GROUPS = ((128, 1), (512, 4), (2048, 16))
B_HEADS_PER_GROUP = 4
B_HEAD_DIM = 128
M_HEADS = 4
M_HEAD_DIM = 128
N_BRANCH = 3
BRANCH_WIDTH = D_MODEL // 2
D_FF = 4 * D_MODEL
EPS = 1e-6
NEG_INF = -1e30

A_Q_W = A_HEADS * A_HEAD_DIM
A_KV_W = A_KV_HEADS * A_HEAD_DIM
B_GROUP_W = B_HEADS_PER_GROUP * B_HEAD_DIM
B_W = len(B_GROUPS) * B_GROUP_W
M_W = M_HEADS * M_HEAD_DIM
GATE_W = N_BRANCH * D_MODEL

LANES = 128
B_HALF_WIN = 64
VMEM_LIMIT_BYTES = 56 * 1024 * 1024

BF16 = jnp.bfloat16
F32 = jnp.float32


def _alibi_slopes(n):
    return [float(2.0 ** (-8.0 * (i + 1) / n)) for i in range(n)]


def _params(n_grid_axes):
    return pltpu.CompilerParams(
        dimension_semantics=("arbitrary",) * n_grid_axes,
        vmem_limit_bytes=VMEM_LIMIT_BYTES)


def _resident(shape):
    zeros = (0,) * len(shape)
    return pl.BlockSpec(shape, lambda *_: zeros, pipeline_mode=pl.Buffered(1))


def _rms_rows(x, gain):
    ms = jnp.mean(x * x, axis=-1, keepdims=True)
    return x * lax.rsqrt(ms + EPS) * gain


def _head_norm128(blk, gain):
    ms = jnp.sum(blk * blk, axis=-1, keepdims=True) * (1.0 / LANES)
    return blk * lax.rsqrt(ms + EPS) * gain


def _head_norm64(blk, gain2):
    low = lax.broadcasted_iota(jnp.int32, blk.shape, 1) < A_HEAD_DIM
    sq = blk * blk
    ss_lo = jnp.sum(jnp.where(low, sq, 0.0), axis=-1, keepdims=True)
    ss_hi = jnp.sum(jnp.where(low, 0.0, sq), axis=-1, keepdims=True)
    ms = jnp.where(low, ss_lo, ss_hi) * (1.0 / A_HEAD_DIM)
    return blk * lax.rsqrt(ms + EPS) * gain2


PROJ_ROWS = 512
N_DILATED_SLABS = sum(3 * (B_GROUP_W // LANES) for _, dil in B_GROUPS if dil > 1)


def _proj_kernel(x_ref, g_ref, wqa_ref, wkva_ref, wqb_ref, wkb_ref, wvb_ref, wqm_ref,
                 gqa_ref, gka_ref, gqb_ref, gkb_ref, gqm_ref,
                 qa_ref, ka_ref, va_ref, qm_ref, *rest):
    b_refs, ybuf = rest[:-1], rest[-1]
    tm = x_ref.shape[1]
    h = _rms_rows(x_ref[0], g_ref[...]).astype(BF16)

    def mm(w_ref, lo, hi):
        return jnp.dot(h, w_ref[:, lo:hi], preferred_element_type=F32)

    y = mm(wqa_ref, 0, A_Q_W)
    for c in range(A_Q_W // LANES):
        sl = slice(c * LANES, (c + 1) * LANES)
        qa_ref[0, :, sl] = _head_norm64(y[:, sl], gqa_ref[...]).astype(BF16)

    y = mm(wkva_ref, 0, 2 * A_KV_W)
    ka_ref[0] = _head_norm64(y[:, :A_KV_W], gka_ref[...]).astype(BF16)
    va_ref[0] = y[:, A_KV_W:].astype(BF16)

    y = mm(wqm_ref, 0, M_W)
    for c in range(M_W // LANES):
        sl = slice(c * LANES, (c + 1) * LANES)
        qm_ref[0, :, sl] = _head_norm128(y[:, sl], gqm_ref[...]).astype(BF16)

    slab = 0
    for gi, (_, dil) in enumerate(B_GROUPS):
        lo = gi * B_GROUP_W
        for part, (w_ref, gain_ref) in enumerate(((wqb_ref, gqb_ref), (wkb_ref, gkb_ref), (wvb_ref, None))):
            o_ref = b_refs[3 * gi + part]
            y = mm(w_ref, lo, lo + B_GROUP_W)
            for c in range(B_GROUP_W // LANES):
                sl = slice(c * LANES, (c + 1) * LANES)
                blk = y[:, sl]
                if gain_ref is not None:
                    blk = _head_norm128(blk, gain_ref[...])
                if dil == 1:
                    o_ref[0, 0, :, sl] = blk.astype(BF16)
                else:
                    ybuf[slab + c] = blk
                    for r in range(dil):
                        o_ref[0, r, :, sl] = ybuf[slab + c, pl.ds(r, tm // dil, stride=dil), :].astype(BF16)
            if dil > 1:
                slab += B_GROUP_W // LANES


def _proj(x, g_mix, ws, gains):
    n, s, _ = x.shape
    tm = PROJ_ROWS
    cur = lambda b, i: (b, i, 0)
    plane = lambda b, i: (b, 0, i, 0)
    in_specs = ([pl.BlockSpec((1, tm, D_MODEL), cur), _resident((1, D_MODEL))]
                + [_resident(w.shape) for w in ws]
                + [_resident((1, LANES)) for _ in gains])
    widths = (A_Q_W, A_KV_W, A_KV_W, M_W)
    out_specs = [pl.BlockSpec((1, tm, w), cur) for w in widths]
    out_shape = [jax.ShapeDtypeStruct((n, s, w), BF16) for w in widths]
    for _, dil in B_GROUPS:
        out_specs += [pl.BlockSpec((1, dil, tm // dil, B_GROUP_W), plane)] * 3
        out_shape += [jax.ShapeDtypeStruct((n, dil, s // dil, B_GROUP_W), BF16)] * 3
    return pl.pallas_call(
        _proj_kernel,
        grid=(n, s // tm),
        in_specs=in_specs,
        out_specs=out_specs,
        out_shape=out_shape,
        scratch_shapes=[pltpu.VMEM((N_DILATED_SLABS, tm, LANES), F32)],
        compiler_params=_params(2),
        name="proj",
    )(x, g_mix, *ws, *gains)


def _mem_kv_kernel(m_ref, g_ref, w_ref, gk_ref, k_ref, v_ref):
    h = _rms_rows(m_ref[...], g_ref[...]).astype(BF16)
    y = jnp.dot(h, w_ref[...], preferred_element_type=F32)
    for c in range(M_HEADS):
        sl = slice(c * LANES, (c + 1) * LANES)
        k_ref[:, sl] = _head_norm128(y[:, sl], gk_ref[...]).astype(BF16)
    v_ref[...] = y[:, M_W:].astype(BF16)


def _mem_kv(mem2d, g_mem, w_mem_kv, gk_m):
    t = mem2d.shape[0]
    row = lambda i: (i, 0)
    return pl.pallas_call(
        _mem_kv_kernel,
        grid=(t // N_MEM,),
        in_specs=[pl.BlockSpec((N_MEM, D_MODEL), row), _resident((1, D_MODEL)),
                  _resident(w_mem_kv.shape), _resident((1, LANES))],
        out_specs=[pl.BlockSpec((N_MEM, M_W), row)] * 2,
        out_shape=[jax.ShapeDtypeStruct((t, M_W), BF16)] * 2,
        compiler_params=_params(1),
        name="mem_kv",
    )(mem2d, g_mem, w_mem_kv, gk_m)


A_BLOCK = A_HALF_WIN
A_STEP = 256


def _attn_a_kernel(seq_len, q_ref, kp_ref, kc_ref, kn_ref, vp_ref, vc_ref, vn_ref, sink_ref,
                   o_ref, kbuf, vbuf):
    i = pl.program_id(1)
    slopes = _alibi_slopes(A_HEADS)
    for buf, prev, cur, nxt in ((kbuf, kp_ref, kc_ref, kn_ref), (vbuf, vp_ref, vc_ref, vn_ref)):
        buf[0:A_BLOCK, :] = prev[0]
        buf[A_BLOCK:A_BLOCK + A_STEP, :] = cur[0]
        buf[A_BLOCK + A_STEP:, :] = nxt[0]

    n_keys = 3 * A_BLOCK
    q_row = lax.broadcasted_iota(jnp.int32, (A_BLOCK, n_keys), 0)
    k_col = lax.broadcasted_iota(jnp.int32, (A_BLOCK, n_keys), 1)
    dist = jnp.abs(q_row + A_BLOCK - k_col)
    dist_f = dist.astype(F32)
    low = lax.broadcasted_iota(jnp.int32, (n_keys, LANES), 1) < A_HEAD_DIM
    low_q = lax.broadcasted_iota(jnp.int32, (A_BLOCK, LANES), 1) < A_HEAD_DIM
    zero = jnp.zeros((n_keys, LANES), BF16)

    def swap_halves(t):
        return jnp.concatenate([t[:, A_HEAD_DIM:], t[:, :A_HEAD_DIM]], axis=1)

    for j in range(A_STEP // A_BLOCK):
        k_pos = i * A_STEP + (j - 1) * A_BLOCK + k_col
        mask = (dist <= A_HALF_WIN) & (k_pos >= 0) & (k_pos < seq_len)
        kw = kbuf[j * A_BLOCK:j * A_BLOCK + n_keys, :]
        vw = vbuf[j * A_BLOCK:j * A_BLOCK + n_keys, :]
        kw_sw, vw_sw = swap_halves(kw), swap_halves(vw)
        for kvh in range(A_KV_HEADS):
            k_lo, k_hi = (kw, kw_sw) if kvh == 0 else (kw_sw, kw)
            v_lo, v_hi = (vw, vw_sw) if kvh == 0 else (vw_sw, vw)
            kk = jnp.concatenate([jnp.where(low, k_lo, zero), jnp.where(low, zero, k_hi)], axis=0)
            vv = jnp.concatenate([jnp.where(low, v_lo, zero), jnp.where(low, zero, v_hi)], axis=0)
            for c in range(2):
                tile = kvh * 2 + c
                sl = slice(tile * LANES, (tile + 1) * LANES)
                q = q_ref[0, j * A_BLOCK:(j + 1) * A_BLOCK, sl]
                s = lax.dot_general(q, kk, (((1,), (1,)), ((), ())), preferred_element_type=F32)
                ps, rdens = [], []
                for half in range(2):
                    head = 2 * tile + half
                    sh = s[:, half * n_keys:(half + 1) * n_keys]
                    sh = jnp.where(mask, sh - slopes[head] * dist_f, NEG_INF)
                    sink = sink_ref[head]
                    m = jnp.maximum(jnp.max(sh, axis=-1, keepdims=True), sink)
                    p = jnp.exp(sh - m)
                    den = jnp.sum(p, axis=-1, keepdims=True) + jnp.exp(sink - m)
                    ps.append(p.astype(BF16))
                    rdens.append(1.0 / den)
                o = jnp.dot(jnp.concatenate(ps, axis=1), vv, preferred_element_type=F32)
                o = o * jnp.where(low_q, rdens[0], rdens[1])
                o_ref[0, j * A_BLOCK:(j + 1) * A_BLOCK, sl] = o.astype(BF16)


def _attn_a(qa, ka, va, sink):
    n, s, _ = qa.shape
    per_step = A_STEP // A_BLOCK
    last = s // A_BLOCK - 1
    cur = lambda b, i: (b, i, 0)
    prev = lambda b, i: (b, jnp.maximum(i * per_step - 1, 0), 0)
    nxt = lambda b, i: (b, jnp.minimum((i + 1) * per_step, last), 0)
    kv_specs = [pl.BlockSpec((1, A_BLOCK, A_KV_W), prev), pl.BlockSpec((1, A_STEP, A_KV_W), cur),
                pl.BlockSpec((1, A_BLOCK, A_KV_W), nxt)]
    return pl.pallas_call(
        functools.partial(_attn_a_kernel, s),
        grid=(n, s // A_STEP),
        in_specs=[pl.BlockSpec((1, A_STEP, A_Q_W), cur)] + kv_specs + kv_specs
                 + [pl.BlockSpec(memory_space=pltpu.SMEM)],
        out_specs=pl.BlockSpec((1, A_STEP, A_Q_W), cur),
        out_shape=jax.ShapeDtypeStruct((n, s, A_Q_W), BF16),
        scratch_shapes=[pltpu.VMEM((A_STEP + 2 * A_BLOCK, A_KV_W), BF16)] * 2,
        compiler_params=_params(2),
        name="attn_a",
    )(qa, ka, ka, ka, va, va, va, sink)


B_BLOCK = B_HALF_WIN
B_ROWS_PER_STEP = 256


def _attn_b_kernel(sub_len, slopes, q_ref, kp_ref, kc_ref, kn_ref, vp_ref, vc_ref, vn_ref,
                   o_ref, lse_ref, kbuf, vbuf):
    i = pl.program_id(2)
    n_planes, step = q_ref.shape[1], q_ref.shape[2]
    for buf, prev, cur, nxt in ((kbuf, kp_ref, kc_ref, kn_ref), (vbuf, vp_ref, vc_ref, vn_ref)):
        for r in range(n_planes):
            buf[r, 0:B_BLOCK, :] = prev[0, r]
            buf[r, B_BLOCK:B_BLOCK + step, :] = cur[0, r]
            buf[r, B_BLOCK + step:, :] = nxt[0, r]

    n_keys = 3 * B_BLOCK
    q_row = lax.broadcasted_iota(jnp.int32, (B_BLOCK, n_keys), 0)
    k_col = lax.broadcasted_iota(jnp.int32, (B_BLOCK, n_keys), 1)
    dist = jnp.abs(q_row + B_BLOCK - k_col)
    dist_f = dist.astype(F32)
    bias = [jnp.where(dist <= B_HALF_WIN, -slope * dist_f, NEG_INF) for slope in slopes]
    lane = lax.broadcasted_iota(jnp.int32, (B_BLOCK, LANES), 1)
    lanes_per_head = LANES // B_HEADS_PER_GROUP
    heads = [slice(h * B_HEAD_DIM, (h + 1) * B_HEAD_DIM) for h in range(B_HEADS_PER_GROUP)]
    blocks = [(r, j) for r in range(n_planes) for j in range(step // B_BLOCK)]
    nt = (((1,), (1,)), ((), ()))

    logits = {}
    for r, j in blocks:
        for h, sl in enumerate(heads):
            q = q_ref[0, r, j * B_BLOCK:(j + 1) * B_BLOCK, sl]
            k = kbuf[r, j * B_BLOCK:j * B_BLOCK + n_keys, sl]
            logits[r, j, h] = lax.dot_general(q, k, nt, preferred_element_type=F32)

    probs = {}
    for r, j in blocks:
        k_idx = i * step + (j - 1) * B_BLOCK + k_col
        valid = (k_idx >= 0) & (k_idx < sub_len)
        lse_tile = jnp.zeros((B_BLOCK, LANES), F32)
        for h in range(B_HEADS_PER_GROUP):
            s = jnp.where(valid, logits[r, j, h] + bias[h], NEG_INF)
            m = jnp.max(s, axis=-1, keepdims=True)
            p = jnp.exp(s - m)
            den = jnp.sum(p, axis=-1, keepdims=True)
            probs[r, j, h] = (p.astype(BF16), 1.0 / den)
            lse_tile = jnp.where(lane >= h * lanes_per_head, m + jnp.log(den), lse_tile)
        lse_ref[0, r, j * B_BLOCK:(j + 1) * B_BLOCK, :] = lse_tile

    for r, j in blocks:
        for h, sl in enumerate(heads):
            p, rden = probs[r, j, h]
            v = vbuf[r, j * B_BLOCK:j * B_BLOCK + n_keys, sl]
            o = jnp.dot(p, v, preferred_element_type=F32) * rden
            o_ref[0, r, j * B_BLOCK:(j + 1) * B_BLOCK, sl] = o.astype(BF16)


def _attn_b_group(q, k, v, gi):
    n, dil, sub, _ = q.shape
    step = min(sub, B_ROWS_PER_STEP)
    n_planes = B_ROWS_PER_STEP // step
    per_step = step // B_BLOCK
    last = sub // B_BLOCK - 1
    cur = lambda b, r, i: (b, r, i, 0)
    prev = lambda b, r, i: (b, r, jnp.maximum(i * per_step - 1, 0), 0)
    nxt = lambda b, r, i: (b, r, jnp.minimum((i + 1) * per_step, last), 0)
    kv_specs = [pl.BlockSpec((1, n_planes, B_BLOCK, B_GROUP_W), prev),
                pl.BlockSpec((1, n_planes, step, B_GROUP_W), cur),
                pl.BlockSpec((1, n_planes, B_BLOCK, B_GROUP_W), nxt)]
    all_slopes = _alibi_slopes(len(B_GROUPS) * B_HEADS_PER_GROUP)
    slopes = [all_slopes[gi * B_HEADS_PER_GROUP + h] * dil for h in range(B_HEADS_PER_GROUP)]
    return pl.pallas_call(
        functools.partial(_attn_b_kernel, sub, slopes),
        grid=(n, dil // n_planes, sub // step),
        in_specs=[pl.BlockSpec((1, n_planes, step, B_GROUP_W), cur)] + kv_specs + kv_specs,
        out_specs=[pl.BlockSpec((1, n_planes, step, B_GROUP_W), cur),
                   pl.BlockSpec((1, n_planes, step, LANES), cur)],
        out_shape=[jax.ShapeDtypeStruct((n, dil, sub, B_GROUP_W), BF16),
                   jax.ShapeDtypeStruct((n, dil, sub, LANES), F32)],
        scratch_shapes=[pltpu.VMEM((n_planes, step + 2 * B_BLOCK, B_GROUP_W), BF16)] * 2,
        compiler_params=_params(3),
        name=f"attn_b{gi}",
    )(q, k, k, k, v, v, v)


M_STEP = 512


def _attn_m_kernel(q_ref, k_ref, v_ref, o_ref):
    for h in range(M_HEADS):
        sl = slice(h * M_HEAD_DIM, (h + 1) * M_HEAD_DIM)
        s = lax.dot_general(q_ref[0, :, sl], k_ref[0, :, sl], (((1,), (1,)), ((), ())),
                            preferred_element_type=F32)
        m = jnp.max(s, axis=-1, keepdims=True)
        p = jnp.exp(s - m)
        den = jnp.sum(p, axis=-1, keepdims=True)
        o = jnp.dot(p.astype(BF16), v_ref[0, :, sl], preferred_element_type=F32) * (1.0 / den)
        o_ref[0, :, sl] = o.astype(BF16)


def _attn_m(qm, mk, mv):
    n, s, _ = qm.shape
    cur = lambda b, i: (b, i, 0)
    mem = lambda b, i: (b, 0, 0)
    return pl.pallas_call(
        _attn_m_kernel,
        grid=(n, s // M_STEP),
        in_specs=[pl.BlockSpec((1, M_STEP, M_W), cur), pl.BlockSpec((1, N_MEM, M_W), mem),
                  pl.BlockSpec((1, N_MEM, M_W), mem)],
        out_specs=pl.BlockSpec((1, M_STEP, M_W), cur),
        out_shape=jax.ShapeDtypeStruct((n, s, M_W), BF16),
        compiler_params=_params(2),
        name="attn_m",
    )(qm, mk, mv)


MERGE_ROWS = 512
N_MERGE_SLABS = sum(B_GROUP_W // LANES + 1 for _, dil in B_GROUPS if dil > 1)


def _merge_kernel(x_ref, g_ref, oa_ref, o0_ref, o1_ref, o2_ref, l0_ref, l1_ref, l2_ref, om_ref,
                  wg_ref, bg_ref, wbr_ref, wout_ref, y_ref, nat):
    tm = x_ref.shape[1]
    x = x_ref[0]
    h = _rms_rows(x, g_ref[...]).astype(BF16)
    tiles = B_GROUP_W // LANES

    o_nat, lse_nat = [], []
    slab = 0
    for (_, dil), o_ref, l_ref in zip(B_GROUPS, (o0_ref, o1_ref, o2_ref), (l0_ref, l1_ref, l2_ref)):
        if dil == 1:
            o_nat.append([o_ref[0, 0, :, c * LANES:(c + 1) * LANES].astype(F32) for c in range(tiles)])
            lse_nat.append(l_ref[0, 0])
            continue
        for r in range(dil):
            rows = pl.ds(r, tm // dil, stride=dil)
            for c in range(tiles):
                nat[slab + c, rows, :] = o_ref[0, r, :, c * LANES:(c + 1) * LANES].astype(F32)
            nat[slab + tiles, rows, :] = l_ref[0, r]
        o_nat.append([nat[slab + c] for c in range(tiles)])
        lse_nat.append(nat[slab + tiles])
        slab += tiles + 1

    top = jnp.maximum(jnp.maximum(lse_nat[0], lse_nat[1]), lse_nat[2])
    es = [jnp.exp(l - top) for l in lse_nat]
    r_sum = 1.0 / (es[0] + es[1] + es[2])
    ws = [e * r_sum for e in es]
    lanes_per_head = LANES // B_HEADS_PER_GROUP
    ob_heads = []
    for hd in range(B_HEADS_PER_GROUP):
        acc = None
        for w, o in zip(ws, o_nat):
            term = w[:, hd * lanes_per_head:hd * lanes_per_head + 1] * o[hd]
            acc = term if acc is None else acc + term
        ob_heads.append(acc.astype(BF16))
    ob = jnp.concatenate(ob_heads, axis=1)

    z = None
    for b, o in enumerate((oa_ref[0], ob, om_ref[0])):
        logits = jnp.dot(h, wg_ref[:, b * D_MODEL:(b + 1) * D_MODEL], preferred_element_type=F32)
        gate = jax.nn.sigmoid(logits + bg_ref[b:b + 1, :])
        term = gate * jnp.dot(o, wbr_ref[b], preferred_element_type=F32)
        z = term if z is None else z + term
    y_ref[0] = x + jnp.dot(z.astype(BF16), wout_ref[...], preferred_element_type=F32)


def _merge(x, g_mix, oa, obs, lses, om, w_gate, b_gate, w_branch, w_out):
    n, s, _ = x.shape
    tm = MERGE_ROWS
    cur = lambda b, i: (b, i, 0)
    plane = lambda b, i: (b, 0, i, 0)
    rows = lambda w: pl.BlockSpec((1, tm, w), cur)
    planes = lambda w: [pl.BlockSpec((1, dil, tm // dil, w), plane) for _, dil in B_GROUPS]
    return pl.pallas_call(
        _merge_kernel,
        grid=(n, s // tm),
        in_specs=[rows(D_MODEL), _resident((1, D_MODEL)), rows(A_Q_W)]
                 + planes(B_GROUP_W) + planes(LANES) + [rows(M_W)]
                 + [_resident(w_gate.shape), _resident(b_gate.shape), _resident(w_branch.shape),
                    _resident(w_out.shape)],
        out_specs=rows(D_MODEL),
        out_shape=jax.ShapeDtypeStruct((n, s, D_MODEL), F32),
        scratch_shapes=[pltpu.VMEM((N_MERGE_SLABS, tm, LANES), F32)],
        compiler_params=_params(2),
        name="merge",
    )(x, g_mix, oa, *obs, *lses, om, w_gate, b_gate, w_branch, w_out)


MLP_ROWS = 512
MLP_FF_CHUNK = 1024


def _mlp_kernel(x_ref, g_ref, wup_ref, wdn_ref, y_ref):
    x = x_ref[...]
    h = _rms_rows(x, g_ref[...]).astype(BF16)
    acc = x
    for lo in range(0, D_FF, MLP_FF_CHUNK):
        u = jnp.dot(h, wup_ref[:, lo:lo + MLP_FF_CHUNK], preferred_element_type=F32)
        a = jnp.square(jnp.maximum(u, 0.0)).astype(BF16)
        acc = acc + jnp.dot(a, wdn_ref[lo:lo + MLP_FF_CHUNK, :], preferred_element_type=F32)
    y_ref[...] = acc


def _mlp(x2d, g_mlp, w_up, w_down):
    t = x2d.shape[0]
    tm = MLP_ROWS
    row = lambda i: (i, 0)
    return pl.pallas_call(
        _mlp_kernel,
        grid=(t // tm,),
        in_specs=[pl.BlockSpec((tm, D_MODEL), row), _resident((1, D_MODEL)),
                  _resident(w_up.shape), _resident(w_down.shape)],
        out_specs=pl.BlockSpec((tm, D_MODEL), row),
        out_shape=jax.ShapeDtypeStruct((t, D_MODEL), F32),
        compiler_params=_params(1),
        name="mlp",
    )(x2d, g_mlp, w_up, w_down)


def _tile2(g):
    return jnp.concatenate([g, g]).reshape(1, LANES)


def _layer(x, mem, g_mix, g_mem, w_in, b_gate, w_mem_kv, gq_a, gk_a, sink_a, gq_b, gk_b, gq_m, gk_m,
           w_branch, w_out, g_mlp, w_up, w_down):
    n, s, _ = x.shape
    row = lambda g: g.reshape(1, -1)

    bounds = np.cumsum((0, A_Q_W, A_KV_W, A_KV_W, B_W, B_W, B_W, M_W, GATE_W))
    seg = lambda a, b: w_in[:, int(bounds[a]):int(bounds[b])].astype(BF16)
    ws = (seg(0, 1), seg(1, 3), seg(3, 4), seg(4, 5), seg(5, 6), seg(6, 7))
    w_gate = seg(7, 8)
    gains = (_tile2(gq_a) * (A_HEAD_DIM ** -0.5), _tile2(gk_a),
             row(gq_b) * (B_HEAD_DIM ** -0.5), row(gk_b), row(gq_m) * (M_HEAD_DIM ** -0.5))

    qa, ka, va, qm, *qkv_b = _proj(x, row(g_mix), ws, gains)
    mk, mv = _mem_kv(mem.reshape(n * N_MEM, D_MODEL), row(g_mem), w_mem_kv.astype(BF16), row(gk_m))

    oa = _attn_a(qa, ka, va, sink_a)
    b_out = [_attn_b_group(*qkv_b[3 * gi:3 * gi + 3], gi) for gi in range(len(B_GROUPS))]
    om = _attn_m(qm, mk.reshape(n, N_MEM, M_W), mv.reshape(n, N_MEM, M_W))

    x1 = _merge(x, row(g_mix), oa, [o for o, _ in b_out], [l for _, l in b_out], om,
                w_gate, b_gate, w_branch.astype(BF16), w_out.astype(BF16))
    y = _mlp(x1.reshape(n * s, D_MODEL), row(g_mlp), w_up.astype(BF16), w_down.astype(BF16))
    return y.reshape(n, s, D_MODEL)


def kernel(x_prompt, x_sample, mem_prompt, mem_sample, g_mix, g_mem, w_in, b_gate, w_mem_kv, gq_a, gk_a,
           sink_a, gq_b, gk_b, gq_m, gk_m, w_branch, w_out, g_mlp, w_up, w_down):
    depth = w_in.shape[0]

    def run(x, mem):
        for l in range(depth):
            x = _layer(x, mem, g_mix[l], g_mem[l], w_in[l], b_gate[l], w_mem_kv[l], gq_a[l], gk_a[l],
                       sink_a[l], gq_b[l], gk_b[l], gq_m[l], gk_m[l], w_branch[l], w_out[l], g_mlp[l],
                       w_up[l], w_down[l])
        return x

    return (run(x_prompt, mem_prompt), run(x_sample, mem_sample))
```

```python
import functools

import numpy as np
import jax
import jax.numpy as jnp
from jax import lax
from jax.experimental import pallas as pl
from jax.experimental.pallas import tpu as pltpu

D_MODEL = 1024
N_MEM = 256
A_HEADS = 8
A_KV_HEADS = 2
A_HEAD_DIM = 64
A_HALF_WIN = 128
B_GROUPS = ((128, 1), (512, 4), (2048, 16))
B_HEADS_PER_GROUP = 4
B_HEAD_DIM = 128
M_HEADS = 4
M_HEAD_DIM = 128
N_BRANCH = 3
BRANCH_WIDTH = D_MODEL // 2
D_FF = 4 * D_MODEL
EPS = 1e-6
NEG_INF = -1e30

A_Q_W = A_HEADS * A_HEAD_DIM
A_KV_W = A_KV_HEADS * A_HEAD_DIM
B_GROUP_W = B_HEADS_PER_GROUP * B_HEAD_DIM
B_W = len(B_GROUPS) * B_GROUP_W
M_W = M_HEADS * M_HEAD_DIM
GATE_W = N_BRANCH * D_MODEL

LANES = 128
B_HALF_WIN = 64
VMEM_LIMIT_BYTES = 56 * 1024 * 1024

BF16 = jnp.bfloat16
F32 = jnp.float32


def _alibi_slopes(n):
    return [float(2.0 ** (-8.0 * (i + 1) / n)) for i in range(n)]


def _params(n_grid_axes):
    return pltpu.CompilerParams(
        dimension_semantics=("arbitrary",) * n_grid_axes,
        vmem_limit_bytes=VMEM_LIMIT_BYTES)


def _resident(shape):
    zeros = (0,) * len(shape)
    return pl.BlockSpec(shape, lambda *_: zeros, pipeline_mode=pl.Buffered(1))


def _rms_rows(x, gain):
    ms = jnp.mean(x * x, axis=-1, keepdims=True)
    return x * lax.rsqrt(ms + EPS) * gain


def _head_norm128(blk, gain):
    ms = jnp.sum(blk * blk, axis=-1, keepdims=True) * (1.0 / LANES)
    return blk * lax.rsqrt(ms + EPS) * gain


def _head_norm64(blk, gain2):
    low = lax.broadcasted_iota(jnp.int32, blk.shape, 1) < A_HEAD_DIM
    sq = blk * blk
    ss_lo = jnp.sum(jnp.where(low, sq, 0.0), axis=-1, keepdims=True)
    ss_hi = jnp.sum(jnp.where(low, 0.0, sq), axis=-1, keepdims=True)
    ms = jnp.where(low, ss_lo, ss_hi) * (1.0 / A_HEAD_DIM)
    return blk * lax.rsqrt(ms + EPS) * gain2


PROJ_ROWS = 512
N_DILATED_SLABS = sum(3 * (B_GROUP_W // LANES) for _, dil in B_GROUPS if dil > 1)


def _proj_kernel(x_ref, g_ref, wqa_ref, wkva_ref, wqb_ref, wkb_ref, wvb_ref, wqm_ref,
                 gqa_ref, gka_ref, gqb_ref, gkb_ref, gqm_ref,
                 qa_ref, ka_ref, va_ref, qm_ref, *rest):
    b_refs, ybuf = rest[:-1], rest[-1]
    tm = x_ref.shape[1]
    h = _rms_rows(x_ref[0], g_ref[...]).astype(BF16)

    def mm(w_ref, lo, hi):
        return jnp.dot(h, w_ref[:, lo:hi], preferred_element_type=F32)

    def branch_b(gi, slab):
        dil = B_GROUPS[gi][1]
        lo = gi * B_GROUP_W
        for part, (w_ref, gain_ref) in enumerate(((wqb_ref, gqb_ref), (wkb_ref, gkb_ref), (wvb_ref, None))):
            o_ref = b_refs[3 * gi + part]
            y = mm(w_ref, lo, lo + B_GROUP_W)
            for c in range(B_GROUP_W // LANES):
                sl = slice(c * LANES, (c + 1) * LANES)
                blk = y[:, sl]
                if gain_ref is not None:
                    blk = _head_norm128(blk, gain_ref[...])
                if dil == 1:
                    o_ref[0, 0, :, sl] = blk.astype(BF16)
                else:
                    ybuf[slab + c] = blk
                    for r in range(dil):
                        o_ref[0, r, :, sl] = ybuf[slab + c, pl.ds(r, tm // dil, stride=dil), :].astype(BF16)
            slab += B_GROUP_W // LANES
        return slab

    by_dilation = sorted(range(len(B_GROUPS)), key=lambda gi: -B_GROUPS[gi][1])
    slab = 0
    for gi in by_dilation[:-1]:
        slab = branch_b(gi, slab)

    y = mm(wqa_ref, 0, A_Q_W)
    for c in range(A_Q_W // LANES):
        sl = slice(c * LANES, (c + 1) * LANES)
        qa_ref[0, :, sl] = _head_norm64(y[:, sl], gqa_ref[...]).astype(BF16)

    y = mm(wkva_ref, 0, 2 * A_KV_W)
    ka_ref[0] = _head_norm64(y[:, :A_KV_W], gka_ref[...]).astype(BF16)
    va_ref[0] = y[:, A_KV_W:].astype(BF16)

    y = mm(wqm_ref, 0, M_W)
    for c in range(M_W // LANES):
        sl = slice(c * LANES, (c + 1) * LANES)
        qm_ref[0, :, sl] = _head_norm128(y[:, sl], gqm_ref[...]).astype(BF16)

    branch_b(by_dilation[-1], slab)


def _proj(x, g_mix, ws, gains):
    n, s, _ = x.shape
    tm = PROJ_ROWS
    cur = lambda b, i: (b, i, 0)
    plane = lambda b, i: (b, 0, i, 0)
    in_specs = ([pl.BlockSpec((1, tm, D_MODEL), cur), _resident((1, D_MODEL))]
                + [_resident(w.shape) for w in ws]
                + [_resident((1, LANES)) for _ in gains])
    widths = (A_Q_W, A_KV_W, A_KV_W, M_W)
    out_specs = [pl.BlockSpec((1, tm, w), cur) for w in widths]
    out_shape = [jax.ShapeDtypeStruct((n, s, w), BF16) for w in widths]
    for _, dil in B_GROUPS:
        out_specs += [pl.BlockSpec((1, dil, tm // dil, B_GROUP_W), plane)] * 3
        out_shape += [jax.ShapeDtypeStruct((n, dil, s // dil, B_GROUP_W), BF16)] * 3
    return pl.pallas_call(
        _proj_kernel,
        grid=(n, s // tm),
        in_specs=in_specs,
        out_specs=out_specs,
        out_shape=out_shape,
        scratch_shapes=[pltpu.VMEM((N_DILATED_SLABS, tm, LANES), F32)],
        compiler_params=_params(2),
        name="proj",
    )(x, g_mix, *ws, *gains)


def _mem_kv_kernel(m_ref, g_ref, w_ref, gk_ref, k_ref, v_ref):
    h = _rms_rows(m_ref[...], g_ref[...]).astype(BF16)
    y = jnp.dot(h, w_ref[...], preferred_element_type=F32)
    for c in range(M_HEADS):
        sl = slice(c * LANES, (c + 1) * LANES)
        k_ref[:, sl] = _head_norm128(y[:, sl], gk_ref[...]).astype(BF16)
    v_ref[...] = y[:, M_W:].astype(BF16)


def _mem_kv(mem2d, g_mem, w_mem_kv, gk_m):
    t = mem2d.shape[0]
    row = lambda i: (i, 0)
    return pl.pallas_call(
        _mem_kv_kernel,
        grid=(t // N_MEM,),
        in_specs=[pl.BlockSpec((N_MEM, D_MODEL), row), _resident((1, D_MODEL)),
                  _resident(w_mem_kv.shape), _resident((1, LANES))],
        out_specs=[pl.BlockSpec((N_MEM, M_W), row)] * 2,
        out_shape=[jax.ShapeDtypeStruct((t, M_W), BF16)] * 2,
        compiler_params=_params(1),
        name="mem_kv",
    )(mem2d, g_mem, w_mem_kv, gk_m)


A_BLOCK = A_HALF_WIN
A_STEP = 512
INTERIOR, FIRST, LAST = 0, 1, 2
LOG2E = float(np.log2(np.e))
LN2 = float(np.log(2.0))


def _attn_a_kernel(seq_len, q_ref, kp_ref, kc_ref, kn_ref, vp_ref, vc_ref, vn_ref, sink_ref,
                   o_ref, kbuf, vbuf, bias_ref):
    i = pl.program_id(1)
    n_keys = 3 * A_BLOCK
    n_blocks = seq_len // A_BLOCK

    @pl.when((pl.program_id(0) == 0) & (i == 0))
    def _():
        q_row = lax.broadcasted_iota(jnp.int32, (A_BLOCK, n_keys), 0)
        k_col = lax.broadcasted_iota(jnp.int32, (A_BLOCK, n_keys), 1)
        dist = jnp.abs(q_row + A_BLOCK - k_col)
        dist_f = dist.astype(F32)
        for head, slope in enumerate(_alibi_slopes(A_HEADS)):
            base = jnp.where(dist <= A_HALF_WIN, (-slope * LOG2E) * dist_f, NEG_INF)
            bias_ref[INTERIOR, head] = base
            bias_ref[FIRST, head] = jnp.where(k_col >= A_BLOCK, base, NEG_INF)
            bias_ref[LAST, head] = jnp.where(k_col < 2 * A_BLOCK, base, NEG_INF)

    for buf, prev, cur, nxt in ((kbuf, kp_ref, kc_ref, kn_ref), (vbuf, vp_ref, vc_ref, vn_ref)):
        buf[0:A_BLOCK, :] = prev[0]
        buf[A_BLOCK:A_BLOCK + A_STEP, :] = cur[0]
        buf[A_BLOCK + A_STEP:, :] = nxt[0]

    low = lax.broadcasted_iota(jnp.int32, (n_keys, LANES), 1) < A_HEAD_DIM
    low_q = lax.broadcasted_iota(jnp.int32, (A_BLOCK, LANES), 1) < A_HEAD_DIM
    zero = jnp.zeros((n_keys, LANES), BF16)
    nt = (((1,), (1,)), ((), ()))
    per_step = A_STEP // A_BLOCK
    units = [(j, kvh) for j in range(per_step) for kvh in range(A_KV_HEADS)]

    def swap_halves(t):
        return jnp.concatenate([t[:, A_HEAD_DIM:], t[:, :A_HEAD_DIM]], axis=1)

    def padded(t, t_sw, kvh):
        lo, hi = (t, t_sw) if kvh == 0 else (t_sw, t)
        return jnp.concatenate([jnp.where(low, lo, zero), jnp.where(low, zero, hi)], axis=0)

    logits, values = {}, {}
    for j in range(per_step):
        rows = slice(j * A_BLOCK, (j + 1) * A_BLOCK)
        kw = kbuf[j * A_BLOCK:j * A_BLOCK + n_keys, :]
        vw = vbuf[j * A_BLOCK:j * A_BLOCK + n_keys, :]
        kw_sw, vw_sw = swap_halves(kw), swap_halves(vw)
        for kvh in range(A_KV_HEADS):
            q2 = jnp.concatenate([q_ref[0, rows, (2 * kvh + c) * LANES:(2 * kvh + c + 1) * LANES]
                                  for c in range(2)], axis=0)
            logits[j, kvh] = lax.dot_general(q2, padded(kw, kw_sw, kvh), nt, preferred_element_type=F32)
            values[j, kvh] = padded(vw, vw_sw, kvh)

    probs = {}
    for j, kvh in units:
        block = i * per_step + j
        variant = jnp.where(block == 0, FIRST, jnp.where(block == n_blocks - 1, LAST, INTERIOR))
        s2 = logits[j, kvh]
        p_rows, r_rows = [], []
        for c in range(2):
            ps, rdens = [], []
            for half in range(2):
                head = 2 * (2 * kvh + c) + half
                sh = s2[c * A_BLOCK:(c + 1) * A_BLOCK, half * n_keys:(half + 1) * n_keys]
                sh = sh + bias_ref[variant, head]
                sink = sink_ref[head] * LOG2E
                m = jnp.maximum(jnp.max(sh, axis=-1, keepdims=True), sink)
                p = jnp.exp2(sh - m)
                den = jnp.sum(p, axis=-1, keepdims=True) + jnp.exp2(sink - m)
                ps.append(p.astype(BF16))
                rdens.append(1.0 / den)
            p_rows.append(jnp.concatenate(ps, axis=1))
            r_rows.append(jnp.where(low_q, rdens[0], rdens[1]))
        probs[j, kvh] = (jnp.concatenate(p_rows, axis=0), jnp.concatenate(r_rows, axis=0))

    for j, kvh in units:
        p2, r2 = probs[j, kvh]
        o2 = jnp.dot(p2, values[j, kvh], preferred_element_type=F32) * r2
        for c in range(2):
            tile = 2 * kvh + c
            o_ref[0, j * A_BLOCK:(j + 1) * A_BLOCK, tile * LANES:(tile + 1) * LANES] = (
                o2[c * A_BLOCK:(c + 1) * A_BLOCK].astype(BF16))


def _attn_a(qa, ka, va, sink):
    n, s, _ = qa.shape
    per_step = A_STEP // A_BLOCK
    n_blocks = s // A_BLOCK
    assert n_blocks >= 2 and s % A_STEP == 0
    cur = lambda b, i: (b, i, 0)
    prev = lambda b, i: (b, jnp.maximum(i * per_step - 1, 0), 0)
    nxt = lambda b, i: (b, jnp.minimum((i + 1) * per_step, n_blocks - 1), 0)
    kv_specs = [pl.BlockSpec((1, A_BLOCK, A_KV_W), prev), pl.BlockSpec((1, A_STEP, A_KV_W), cur),
                pl.BlockSpec((1, A_BLOCK, A_KV_W), nxt)]
    return pl.pallas_call(
        functools.partial(_attn_a_kernel, s),
        grid=(n, s // A_STEP),
        in_specs=[pl.BlockSpec((1, A_STEP, A_Q_W), cur)] + kv_specs + kv_specs
                 + [pl.BlockSpec(memory_space=pltpu.SMEM)],
        out_specs=pl.BlockSpec((1, A_STEP, A_Q_W), cur),
        out_shape=jax.ShapeDtypeStruct((n, s, A_Q_W), BF16),
        scratch_shapes=[pltpu.VMEM((A_STEP + 2 * A_BLOCK, A_KV_W), BF16)] * 2
                       + [pltpu.VMEM((3, A_HEADS, A_BLOCK, 3 * A_BLOCK), F32)],
        compiler_params=_params(2),
        name="attn_a",
    )(qa, ka, ka, ka, va, va, va, sink)


B_BLOCK = 2 * B_HALF_WIN
B_KEYS = B_BLOCK + 2 * B_HALF_WIN
B_ROWS_PER_STEP = 512


def _attn_b_kernel(sub_len, slopes, q_ref, kp_ref, kc_ref, kn_ref, vp_ref, vc_ref, vn_ref,
                   o_ref, lse_ref, kbuf, vbuf, bias_ref):
    i = pl.program_id(2)
    n_planes, step = q_ref.shape[1], q_ref.shape[2]
    per_step = step // B_BLOCK
    n_blocks = sub_len // B_BLOCK

    @pl.when((pl.program_id(0) == 0) & (pl.program_id(1) == 0) & (i == 0))
    def _():
        q_row = lax.broadcasted_iota(jnp.int32, (B_BLOCK, B_KEYS), 0)
        k_col = lax.broadcasted_iota(jnp.int32, (B_BLOCK, B_KEYS), 1)
        dist = jnp.abs(q_row + B_HALF_WIN - k_col)
        dist_f = dist.astype(F32)
        for head, slope in enumerate(slopes):
            base = jnp.where(dist <= B_HALF_WIN, (-slope * LOG2E) * dist_f, NEG_INF)
            first = jnp.where(k_col >= B_HALF_WIN, base, NEG_INF)
            bias_ref[0, head] = base
            bias_ref[1, head] = first
            bias_ref[2, head] = jnp.where(k_col < B_HALF_WIN + B_BLOCK, base, NEG_INF)
            bias_ref[3, head] = jnp.where(k_col < B_HALF_WIN + B_BLOCK, first, NEG_INF)

    for buf, prev, cur, nxt in ((kbuf, kp_ref, kc_ref, kn_ref), (vbuf, vp_ref, vc_ref, vn_ref)):
        for r in range(n_planes):
            buf[r, 0:B_HALF_WIN, :] = prev[0, r]
            buf[r, B_HALF_WIN:B_HALF_WIN + step, :] = cur[0, r]
            buf[r, B_HALF_WIN + step:, :] = nxt[0, r]

    lane = lax.broadcasted_iota(jnp.int32, (B_BLOCK, LANES), 1)
    lanes_per_head = LANES // B_HEADS_PER_GROUP
    heads = [slice(h * B_HEAD_DIM, (h + 1) * B_HEAD_DIM) for h in range(B_HEADS_PER_GROUP)]
    blocks = [(r, j) for r in range(n_planes) for j in range(per_step)]
    nt = (((1,), (1,)), ((), ()))

    logits = {}
    for r, j in blocks:
        for h, sl in enumerate(heads):
            q = q_ref[0, r, j * B_BLOCK:(j + 1) * B_BLOCK, sl]
            k = kbuf[r, j * B_BLOCK:j * B_BLOCK + B_KEYS, sl]
            logits[r, j, h] = lax.dot_general(q, k, nt, preferred_element_type=F32)

    probs = {}
    for r, j in blocks:
        block = i * per_step + j
        variant = (block == 0).astype(jnp.int32) + 2 * (block == n_blocks - 1).astype(jnp.int32)
        lse_tile = jnp.zeros((B_BLOCK, LANES), F32)
        for h in range(B_HEADS_PER_GROUP):
            s = logits[r, j, h] + bias_ref[variant, h]
            m = jnp.max(s, axis=-1, keepdims=True)
            p = jnp.exp2(s - m)
            den = jnp.sum(p, axis=-1, keepdims=True)
            probs[r, j, h] = (p.astype(BF16), 1.0 / den)
            lse_tile = jnp.where(lane >= h * lanes_per_head, (m + jnp.log2(den)) * LN2, lse_tile)
        lse_ref[0, r, j * B_BLOCK:(j + 1) * B_BLOCK, :] = lse_tile

    for r, j in blocks:
        for h, sl in enumerate(heads):
            p, rden = probs[r, j, h]
            v = vbuf[r, j * B_BLOCK:j * B_BLOCK + B_KEYS, sl]
            o = jnp.dot(p, v, preferred_element_type=F32) * rden
            o_ref[0, r, j * B_BLOCK:(j + 1) * B_BLOCK, sl] = o.astype(BF16)


def _attn_b_group(q, k, v, gi):
    n, dil, sub, _ = q.shape
    step = min(sub, B_ROWS_PER_STEP)
    n_planes = B_ROWS_PER_STEP // step
    assert sub % step == 0 and dil % n_planes == 0 and step % B_BLOCK == 0
    halo_per_step = step // B_HALF_WIN
    last_halo = sub // B_HALF_WIN - 1
    cur = lambda b, r, i: (b, r, i, 0)
    prev = lambda b, r, i: (b, r, jnp.maximum(i * halo_per_step - 1, 0), 0)
    nxt = lambda b, r, i: (b, r, jnp.minimum((i + 1) * halo_per_step, last_halo), 0)
    kv_specs = [pl.BlockSpec((1, n_planes, B_HALF_WIN, B_GROUP_W), prev),
                pl.BlockSpec((1, n_planes, step, B_GROUP_W), cur),
                pl.BlockSpec((1, n_planes, B_HALF_WIN, B_GROUP_W), nxt)]
    all_slopes = _alibi_slopes(len(B_GROUPS) * B_HEADS_PER_GROUP)
    slopes = [all_slopes[gi * B_HEADS_PER_GROUP + h] * dil for h in range(B_HEADS_PER_GROUP)]
    return pl.pallas_call(
        functools.partial(_attn_b_kernel, sub, slopes),
        grid=(n, dil // n_planes, sub // step),
        in_specs=[pl.BlockSpec((1, n_planes, step, B_GROUP_W), cur)] + kv_specs + kv_specs,
        out_specs=[pl.BlockSpec((1, n_planes, step, B_GROUP_W), cur),
                   pl.BlockSpec((1, n_planes, step, LANES), cur)],
        out_shape=[jax.ShapeDtypeStruct((n, dil, sub, B_GROUP_W), BF16),
                   jax.ShapeDtypeStruct((n, dil, sub, LANES), F32)],
        scratch_shapes=[pltpu.VMEM((n_planes, step + 2 * B_HALF_WIN, B_GROUP_W), BF16)] * 2
                       + [pltpu.VMEM((4, B_HEADS_PER_GROUP, B_BLOCK, B_KEYS), F32)],
        compiler_params=_params(3),
        name=f"attn_b{gi}",
    )(q, k, k, k, v, v, v)


M_STEP = 512


def _attn_m_kernel(q_ref, k_ref, v_ref, o_ref):
    for h in range(M_HEADS):
        sl = slice(h * M_HEAD_DIM, (h + 1) * M_HEAD_DIM)
        s = lax.dot_general(q_ref[0, :, sl], k_ref[0, :, sl], (((1,), (1,)), ((), ())),
                            preferred_element_type=F32)
        m = jnp.max(s, axis=-1, keepdims=True)
        p = jnp.exp2(s - m)
        den = jnp.sum(p, axis=-1, keepdims=True)
        o = jnp.dot(p.astype(BF16), v_ref[0, :, sl], preferred_element_type=F32) * (1.0 / den)
        o_ref[0, :, sl] = o.astype(BF16)


def _attn_m(qm, mk, mv):
    n, s, _ = qm.shape
    cur = lambda b, i: (b, i, 0)
    mem = lambda b, i: (b, 0, 0)
    return pl.pallas_call(
        _attn_m_kernel,
        grid=(n, s // M_STEP),
        in_specs=[pl.BlockSpec((1, M_STEP, M_W), cur), pl.BlockSpec((1, N_MEM, M_W), mem),
                  pl.BlockSpec((1, N_MEM, M_W), mem)],
        out_specs=pl.BlockSpec((1, M_STEP, M_W), cur),
        out_shape=jax.ShapeDtypeStruct((n, s, M_W), BF16),
        compiler_params=_params(2),
        name="attn_m",
    )(qm, mk, mv)


MERGE_ROWS = 512
N_MERGE_SLABS = sum(B_GROUP_W // LANES + 1 for _, dil in B_GROUPS if dil > 1)


def _merge_kernel(x_ref, g_ref, oa_ref, o0_ref, o1_ref, o2_ref, l0_ref, l1_ref, l2_ref, om_ref,
                  wg_ref, bg_ref, wbr_ref, wout_ref, y_ref, nat):
    tm = x_ref.shape[1]
    x = x_ref[0]
    h = _rms_rows(x, g_ref[...]).astype(BF16)
    tiles = B_GROUP_W // LANES

    o_nat, lse_nat = [], []
    slab = 0
    for (_, dil), o_ref, l_ref in zip(B_GROUPS, (o0_ref, o1_ref, o2_ref), (l0_ref, l1_ref, l2_ref)):
        if dil == 1:
            o_nat.append([o_ref[0, 0, :, c * LANES:(c + 1) * LANES].astype(F32) for c in range(tiles)])
            lse_nat.append(l_ref[0, 0])
            continue
        for r in range(dil):
            rows = pl.ds(r, tm // dil, stride=dil)
            for c in range(tiles):
                nat[slab + c, rows, :] = o_ref[0, r, :, c * LANES:(c + 1) * LANES].astype(F32)
            nat[slab + tiles, rows, :] = l_ref[0, r]
        o_nat.append([nat[slab + c] for c in range(tiles)])
        lse_nat.append(nat[slab + tiles])
        slab += tiles + 1

    top = jnp.maximum(jnp.maximum(lse_nat[0], lse_nat[1]), lse_nat[2])
    es = [jnp.exp(l - top) for l in lse_nat]
    r_sum = 1.0 / (es[0] + es[1] + es[2])
    ws = [e * r_sum for e in es]
    lanes_per_head = LANES // B_HEADS_PER_GROUP
    ob_heads = []
    for hd in range(B_HEADS_PER_GROUP):
        acc = None
        for w, o in zip(ws, o_nat):
            term = w[:, hd * lanes_per_head:hd * lanes_per_head + 1] * o[hd]
            acc = term if acc is None else acc + term
        ob_heads.append(acc.astype(BF16))
    ob = jnp.concatenate(ob_heads, axis=1)

    z = None
    for b, o in enumerate((oa_ref[0], ob, om_ref[0])):
        logits = jnp.dot(h, wg_ref[:, b * D_MODEL:(b + 1) * D_MODEL], preferred_element_type=F32)
        gate = jax.nn.sigmoid(logits + bg_ref[b:b + 1, :])
        term = gate * jnp.dot(o, wbr_ref[b], preferred_element_type=F32)
        z = term if z is None else z + term
    y_ref[0] = x + jnp.dot(z.astype(BF16), wout_ref[...], preferred_element_type=F32)


def _merge(x, g_mix, oa, obs, lses, om, w_gate, b_gate, w_branch, w_out):
    n, s, _ = x.shape
    tm = MERGE_ROWS
    cur = lambda b, i: (b, i, 0)
    plane = lambda b, i: (b, 0, i, 0)
    rows = lambda w: pl.BlockSpec((1, tm, w), cur)
    planes = lambda w: [pl.BlockSpec((1, dil, tm // dil, w), plane) for _, dil in B_GROUPS]
    return pl.pallas_call(
        _merge_kernel,
        grid=(n, s // tm),
        in_specs=[rows(D_MODEL), _resident((1, D_MODEL)), rows(A_Q_W)]
                 + planes(B_GROUP_W) + planes(LANES) + [rows(M_W)]
                 + [_resident(w_gate.shape), _resident(b_gate.shape), _resident(w_branch.shape),
                    _resident(w_out.shape)],
        out_specs=rows(D_MODEL),
        out_shape=jax.ShapeDtypeStruct((n, s, D_MODEL), F32),
        scratch_shapes=[pltpu.VMEM((N_MERGE_SLABS, tm, LANES), F32)],
        compiler_params=_params(2),
        name="merge",
    )(x, g_mix, oa, *obs, *lses, om, w_gate, b_gate, w_branch, w_out)


MLP_ROWS = 512
MLP_FF_CHUNK = 1024


def _mlp_kernel(x_ref, g_ref, wup_ref, wdn_ref, y_ref):
    x = x_ref[...]
    h = _rms_rows(x, g_ref[...]).astype(BF16)
    acc = x
    for lo in range(0, D_FF, MLP_FF_CHUNK):
        u = jnp.dot(h, wup_ref[:, lo:lo + MLP_FF_CHUNK], preferred_element_type=F32)
        a = jnp.square(jnp.maximum(u, 0.0)).astype(BF16)
        acc = acc + jnp.dot(a, wdn_ref[lo:lo + MLP_FF_CHUNK, :], preferred_element_type=F32)
    y_ref[...] = acc


def _mlp(x2d, g_mlp, w_up, w_down):
    t = x2d.shape[0]
    tm = MLP_ROWS
    row = lambda i: (i, 0)
    return pl.pallas_call(
        _mlp_kernel,
        grid=(t // tm,),
        in_specs=[pl.BlockSpec((tm, D_MODEL), row), _resident((1, D_MODEL)),
                  _resident(w_up.shape), _resident(w_down.shape)],
        out_specs=pl.BlockSpec((tm, D_MODEL), row),
        out_shape=jax.ShapeDtypeStruct((t, D_MODEL), F32),
        compiler_params=_params(1),
        name="mlp",
    )(x2d, g_mlp, w_up, w_down)


def _tile2(g):
    return jnp.concatenate([g, g]).reshape(1, LANES)


def _layer(x, mem, g_mix, g_mem, w_in, b_gate, w_mem_kv, gq_a, gk_a, sink_a, gq_b, gk_b, gq_m, gk_m,
           w_branch, w_out, g_mlp, w_up, w_down):
    n, s, _ = x.shape
    row = lambda g: g.reshape(1, -1)

    bounds = np.cumsum((0, A_Q_W, A_KV_W, A_KV_W, B_W, B_W, B_W, M_W, GATE_W))
    seg = lambda a, b: w_in[:, int(bounds[a]):int(bounds[b])].astype(BF16)
    ws = (seg(0, 1), seg(1, 3), seg(3, 4), seg(4, 5), seg(5, 6), seg(6, 7))
    w_gate = seg(7, 8)
    gains = (_tile2(gq_a) * (A_HEAD_DIM ** -0.5 * LOG2E), _tile2(gk_a),
             row(gq_b) * (B_HEAD_DIM ** -0.5 * LOG2E), row(gk_b), row(gq_m) * (M_HEAD_DIM ** -0.5 * LOG2E))

    qa, ka, va, qm, *qkv_b = _proj(x, row(g_mix), ws, gains)
    mk, mv = _mem_kv(mem.reshape(n * N_MEM, D_MODEL), row(g_mem), w_mem_kv.astype(BF16), row(gk_m))

    oa = _attn_a(qa, ka, va, sink_a)
    b_out = [_attn_b_group(*qkv_b[3 * gi:3 * gi + 3], gi) for gi in range(len(B_GROUPS))]
    om = _attn_m(qm, mk.reshape(n, N_MEM, M_W), mv.reshape(n, N_MEM, M_W))

    x1 = _merge(x, row(g_mix), oa, [o for o, _ in b_out], [l for _, l in b_out], om,
                w_gate, b_gate, w_branch.astype(BF16), w_out.astype(BF16))
    y = _mlp(x1.reshape(n * s, D_MODEL), row(g_mlp), w_up.astype(BF16), w_down.astype(BF16))
    return y.reshape(n, s, D_MODEL)


def kernel(x_prompt, x_sample, mem_prompt, mem_sample, g_mix, g_mem, w_in, b_gate, w_mem_kv, gq_a, gk_a,
           sink_a, gq_b, gk_b, gq_m, gk_m, w_branch, w_out, g_mlp, w_up, w_down):
    depth = w_in.shape[0]

    def run(x, mem):
        for l in range(depth):
            x = _layer(x, mem, g_mix[l], g_mem[l], w_in[l], b_gate[l], w_mem_kv[l], gq_a[l], gk_a[l],
                       sink_a[l], gq_b[l], gk_b[l], gq_m[l], gk_m[l], w_branch[l], w_out[l], g_mlp[l],
                       w_up[l], w_down[l])
        return x

    return (run(x_prompt, mem_prompt), run(x_sample, mem_sample))
```

```python
import functools

import numpy as np
import jax
import jax.numpy as jnp
from jax import lax
from jax.experimental import pallas as pl
from jax.experimental.pallas import tpu as pltpu

D_MODEL = 1024
N_MEM = 256
A_HEADS = 8
A_KV_HEADS = 2
A_HEAD_DIM = 64
A_HALF_WIN = 128
B_GROUPS = ((128, 1), (512, 4), (2048, 16))
B_HEADS_PER_GROUP = 4
B_HEAD_DIM = 128
M_HEADS = 4
M_HEAD_DIM = 128
N_BRANCH = 3
BRANCH_WIDTH = D_MODEL // 2
D_FF = 4 * D_MODEL
EPS = 1e-6
NEG_INF = -1e30

A_Q_W = A_HEADS * A_HEAD_DIM
A_KV_W = A_KV_HEADS * A_HEAD_DIM
B_GROUP_W = B_HEADS_PER_GROUP * B_HEAD_DIM
B_W = len(B_GROUPS) * B_GROUP_W
M_W = M_HEADS * M_HEAD_DIM
GATE_W = N_BRANCH * D_MODEL

LANES = 128
B_HALF_WIN = 64
VMEM_LIMIT_BYTES = 56 * 1024 * 1024

BF16 = jnp.bfloat16
F32 = jnp.float32


def _alibi_slopes(n):
    return [float(2.0 ** (-8.0 * (i + 1) / n)) for i in range(n)]


def _params(n_grid_axes):
    return pltpu.CompilerParams(
        dimension_semantics=("arbitrary",) * n_grid_axes,
        vmem_limit_bytes=VMEM_LIMIT_BYTES)


def _resident(shape):
    zeros = (0,) * len(shape)
    return pl.BlockSpec(shape, lambda *_: zeros, pipeline_mode=pl.Buffered(1))


def _rms_rows(x, gain):
    ms = jnp.mean(x * x, axis=-1, keepdims=True)
    return x * lax.rsqrt(ms + EPS) * gain


def _head_norm128(blk, gain):
    ms = jnp.sum(blk * blk, axis=-1, keepdims=True) * (1.0 / LANES)
    return blk * lax.rsqrt(ms + EPS) * gain


def _head_norm64(blk, gain2):
    low = lax.broadcasted_iota(jnp.int32, blk.shape, 1) < A_HEAD_DIM
    sq = blk * blk
    ss_lo = jnp.sum(jnp.where(low, sq, 0.0), axis=-1, keepdims=True)
    ss_hi = jnp.sum(jnp.where(low, 0.0, sq), axis=-1, keepdims=True)
    ms = jnp.where(low, ss_lo, ss_hi) * (1.0 / A_HEAD_DIM)
    return blk * lax.rsqrt(ms + EPS) * gain2


PROJ_ROWS = 512
N_DILATED_SLABS = sum(3 * (B_GROUP_W // LANES) for _, dil in B_GROUPS if dil > 1)


def _proj_kernel(x_ref, g_ref, wqa_ref, wkva_ref, wqb_ref, wkb_ref, wvb_ref, wqm_ref,
                 gqa_ref, gka_ref, gqb_ref, gkb_ref, gqm_ref,
                 qa_ref, ka_ref, va_ref, qm_ref, *rest):
    b_refs, ybuf = rest[:-1], rest[-1]
    tm = x_ref.shape[1]
    h = _rms_rows(x_ref[0], g_ref[...]).astype(BF16)

    def mm(w_ref, lo, hi):
        return jnp.dot(h, w_ref[:, lo:hi], preferred_element_type=F32)

    def branch_b(gi, slab):
        dil = B_GROUPS[gi][1]
        lo = gi * B_GROUP_W
        for part, (w_ref, gain_ref) in enumerate(((wqb_ref, gqb_ref), (wkb_ref, gkb_ref), (wvb_ref, None))):
            o_ref = b_refs[3 * gi + part]
            y = mm(w_ref, lo, lo + B_GROUP_W)
            for c in range(B_GROUP_W // LANES):
                sl = slice(c * LANES, (c + 1) * LANES)
                blk = y[:, sl]
                if gain_ref is not None:
                    blk = _head_norm128(blk, gain_ref[...])
                if dil == 1:
                    o_ref[0, 0, :, sl] = blk.astype(BF16)
                else:
                    ybuf[slab + c] = blk
                    for r in range(dil):
                        o_ref[0, r, :, sl] = ybuf[slab + c, pl.ds(r, tm // dil, stride=dil), :].astype(BF16)
            slab += B_GROUP_W // LANES
        return slab

    by_dilation = sorted(range(len(B_GROUPS)), key=lambda gi: -B_GROUPS[gi][1])
    slab = 0
    for gi in by_dilation[:-1]:
        slab = branch_b(gi, slab)

    y = mm(wqa_ref, 0, A_Q_W)
    for c in range(A_Q_W // LANES):
        sl = slice(c * LANES, (c + 1) * LANES)
        qa_ref[0, :, sl] = _head_norm64(y[:, sl], gqa_ref[...]).astype(BF16)

    y = mm(wkva_ref, 0, 2 * A_KV_W)
    ka_ref[0] = _head_norm64(y[:, :A_KV_W], gka_ref[...]).astype(BF16)
    va_ref[0] = y[:, A_KV_W:].astype(BF16)

    y = mm(wqm_ref, 0, M_W)
    for c in range(M_W // LANES):
        sl = slice(c * LANES, (c + 1) * LANES)
        qm_ref[0, :, sl] = _head_norm128(y[:, sl], gqm_ref[...]).astype(BF16)

    branch_b(by_dilation[-1], slab)


def _proj(x, g_mix, ws, gains):
    n, s, _ = x.shape
    tm = PROJ_ROWS
    cur = lambda b, i: (b, i, 0)
    plane = lambda b, i: (b, 0, i, 0)
    in_specs = ([pl.BlockSpec((1, tm, D_MODEL), cur), _resident((1, D_MODEL))]
                + [_resident(w.shape) for w in ws]
                + [_resident((1, LANES)) for _ in gains])
    widths = (A_Q_W, A_KV_W, A_KV_W, M_W)
    out_specs = [pl.BlockSpec((1, tm, w), cur) for w in widths]
    out_shape = [jax.ShapeDtypeStruct((n, s, w), BF16) for w in widths]
    for _, dil in B_GROUPS:
        out_specs += [pl.BlockSpec((1, dil, tm // dil, B_GROUP_W), plane)] * 3
        out_shape += [jax.ShapeDtypeStruct((n, dil, s // dil, B_GROUP_W), BF16)] * 3
    return pl.pallas_call(
        _proj_kernel,
        grid=(n, s // tm),
        in_specs=in_specs,
        out_specs=out_specs,
        out_shape=out_shape,
        scratch_shapes=[pltpu.VMEM((N_DILATED_SLABS, tm, LANES), F32)],
        compiler_params=_params(2),
        name="proj",
    )(x, g_mix, *ws, *gains)


def _mem_kv_kernel(m_ref, g_ref, w_ref, gk_ref, k_ref, v_ref):
    h = _rms_rows(m_ref[...], g_ref[...]).astype(BF16)
    y = jnp.dot(h, w_ref[...], preferred_element_type=F32)
    for c in range(M_HEADS):
        sl = slice(c * LANES, (c + 1) * LANES)
        k_ref[:, sl] = _head_norm128(y[:, sl], gk_ref[...]).astype(BF16)
    v_ref[...] = y[:, M_W:].astype(BF16)


def _mem_kv(mem2d, g_mem, w_mem_kv, gk_m):
    t = mem2d.shape[0]
    row = lambda i: (i, 0)
    return pl.pallas_call(
        _mem_kv_kernel,
        grid=(t // N_MEM,),
        in_specs=[pl.BlockSpec((N_MEM, D_MODEL), row), _resident((1, D_MODEL)),
                  _resident(w_mem_kv.shape), _resident((1, LANES))],
        out_specs=[pl.BlockSpec((N_MEM, M_W), row)] * 2,
        out_shape=[jax.ShapeDtypeStruct((t, M_W), BF16)] * 2,
        compiler_params=_params(1),
        name="mem_kv",
    )(mem2d, g_mem, w_mem_kv, gk_m)


A_BLOCK = A_HALF_WIN
A_STEP = 2048
ATTN_CHUNK = 512
INTERIOR, FIRST, LAST = 0, 1, 2
LOG2E = float(np.log2(np.e))
LN2 = float(np.log(2.0))


def _attn_a_kernel(seq_len, q_ref, kp_ref, kc_ref, kn_ref, vp_ref, vc_ref, vn_ref, sink_ref,
                   o_ref, kbuf, vbuf, bias_ref):
    i = pl.program_id(1)
    n_keys = 3 * A_BLOCK
    n_blocks = seq_len // A_BLOCK

    @pl.when((pl.program_id(0) == 0) & (i == 0))
    def _():
        q_row = lax.broadcasted_iota(jnp.int32, (A_BLOCK, n_keys), 0)
        k_col = lax.broadcasted_iota(jnp.int32, (A_BLOCK, n_keys), 1)
        dist = jnp.abs(q_row + A_BLOCK - k_col)
        dist_f = dist.astype(F32)
        for head, slope in enumerate(_alibi_slopes(A_HEADS)):
            base = jnp.where(dist <= A_HALF_WIN, (-slope * LOG2E) * dist_f, NEG_INF)
            bias_ref[INTERIOR, head] = base
            bias_ref[FIRST, head] = jnp.where(k_col >= A_BLOCK, base, NEG_INF)
            bias_ref[LAST, head] = jnp.where(k_col < 2 * A_BLOCK, base, NEG_INF)

    for buf, prev, cur, nxt in ((kbuf, kp_ref, kc_ref, kn_ref), (vbuf, vp_ref, vc_ref, vn_ref)):
        buf[0:A_BLOCK, :] = prev[0]
        buf[A_BLOCK:A_BLOCK + A_STEP, :] = cur[0]
        buf[A_BLOCK + A_STEP:, :] = nxt[0]

    low = lax.broadcasted_iota(jnp.int32, (n_keys, LANES), 1) < A_HEAD_DIM
    low_q = lax.broadcasted_iota(jnp.int32, (A_BLOCK, LANES), 1) < A_HEAD_DIM
    zero = jnp.zeros((n_keys, LANES), BF16)
    nt = (((1,), (1,)), ((), ()))
    per_chunk = ATTN_CHUNK // A_BLOCK
    units = [(j, kvh) for j in range(per_chunk) for kvh in range(A_KV_HEADS)]

    def swap_halves(t):
        return jnp.concatenate([t[:, A_HEAD_DIM:], t[:, :A_HEAD_DIM]], axis=1)

    def padded(t, t_sw, kvh):
        lo, hi = (t, t_sw) if kvh == 0 else (t_sw, t)
        return jnp.concatenate([jnp.where(low, lo, zero), jnp.where(low, zero, hi)], axis=0)

    def chunk(t, carry):
        row0 = pl.multiple_of(t * ATTN_CHUNK, ATTN_CHUNK)
        logits, values = {}, {}
        for j in range(per_chunk):
            rows = pl.ds(row0 + j * A_BLOCK, A_BLOCK)
            kw = kbuf[pl.ds(row0 + j * A_BLOCK, n_keys), :]
            vw = vbuf[pl.ds(row0 + j * A_BLOCK, n_keys), :]
            kw_sw, vw_sw = swap_halves(kw), swap_halves(vw)
            for kvh in range(A_KV_HEADS):
                q2 = jnp.concatenate([q_ref[0, rows, (2 * kvh + c) * LANES:(2 * kvh + c + 1) * LANES]
                                      for c in range(2)], axis=0)
                logits[j, kvh] = lax.dot_general(q2, padded(kw, kw_sw, kvh), nt,
                                                 preferred_element_type=F32)
                values[j, kvh] = padded(vw, vw_sw, kvh)

        probs = {}
        for j, kvh in units:
            block = (i * A_STEP + row0) // A_BLOCK + j
            variant = jnp.where(block == 0, FIRST, jnp.where(block == n_blocks - 1, LAST, INTERIOR))
            s2 = logits[j, kvh]
            p_rows, r_rows = [], []
            for c in range(2):
                ps, sums, sink_gaps = [], [], []
                for half in range(2):
                    head = 2 * (2 * kvh + c) + half
                    sh = s2[c * A_BLOCK:(c + 1) * A_BLOCK, half * n_keys:(half + 1) * n_keys]
                    sh = sh + bias_ref[variant, head]
                    sink = sink_ref[head] * LOG2E
                    m = jnp.maximum(jnp.max(sh, axis=-1, keepdims=True), sink)
                    p = jnp.exp2(sh - m)
                    ps.append(p.astype(BF16))
                    sums.append(jnp.sum(p, axis=-1, keepdims=True))
                    sink_gaps.append(sink - m)
                den = jnp.where(low_q, sums[0], sums[1]) + jnp.exp2(jnp.where(low_q, sink_gaps[0], sink_gaps[1]))
                p_rows.append(jnp.concatenate(ps, axis=1))
                r_rows.append(1.0 / den)
            probs[j, kvh] = (jnp.concatenate(p_rows, axis=0), jnp.concatenate(r_rows, axis=0))

        for j, kvh in units:
            p2, r2 = probs[j, kvh]
            o2 = jnp.dot(p2, values[j, kvh], preferred_element_type=F32) * r2
            for c in range(2):
                tile = 2 * kvh + c
                o_ref[0, pl.ds(row0 + j * A_BLOCK, A_BLOCK), tile * LANES:(tile + 1) * LANES] = (
                    o2[c * A_BLOCK:(c + 1) * A_BLOCK].astype(BF16))
        return carry

    lax.fori_loop(0, A_STEP // ATTN_CHUNK, chunk, 0)


def _attn_a(qa, ka, va, sink):
    n, s, _ = qa.shape
    per_step = A_STEP // A_BLOCK
    n_blocks = s // A_BLOCK
    assert n_blocks >= 2 and s % A_STEP == 0
    cur = lambda b, i: (b, i, 0)
    prev = lambda b, i: (b, jnp.maximum(i * per_step - 1, 0), 0)
    nxt = lambda b, i: (b, jnp.minimum((i + 1) * per_step, n_blocks - 1), 0)
    kv_specs = [pl.BlockSpec((1, A_BLOCK, A_KV_W), prev), pl.BlockSpec((1, A_STEP, A_KV_W), cur),
                pl.BlockSpec((1, A_BLOCK, A_KV_W), nxt)]
    return pl.pallas_call(
        functools.partial(_attn_a_kernel, s),
        grid=(n, s // A_STEP),
        in_specs=[pl.BlockSpec((1, A_STEP, A_Q_W), cur)] + kv_specs + kv_specs
                 + [pl.BlockSpec(memory_space=pltpu.SMEM)],
        out_specs=pl.BlockSpec((1, A_STEP, A_Q_W), cur),
        out_shape=jax.ShapeDtypeStruct((n, s, A_Q_W), BF16),
        scratch_shapes=[pltpu.VMEM((A_STEP + 2 * A_BLOCK, A_KV_W), BF16)] * 2
                       + [pltpu.VMEM((3, A_HEADS, A_BLOCK, 3 * A_BLOCK), F32)],
        compiler_params=_params(2),
        name="attn_a",
    )(qa, ka, ka, ka, va, va, va, sink)


B_BLOCK = 2 * B_HALF_WIN
B_KEYS = B_BLOCK + 2 * B_HALF_WIN
B_ROWS_PER_STEP = 2048


def _attn_b_kernel(sub_len, slopes, q_ref, kp_ref, kc_ref, kn_ref, vp_ref, vc_ref, vn_ref,
                   o_ref, lse_ref, kbuf, vbuf, bias_ref):
    i = pl.program_id(2)
    n_planes, step = q_ref.shape[1], q_ref.shape[2]
    n_blocks = sub_len // B_BLOCK

    @pl.when((pl.program_id(0) == 0) & (pl.program_id(1) == 0) & (i == 0))
    def _():
        q_row = lax.broadcasted_iota(jnp.int32, (B_BLOCK, B_KEYS), 0)
        k_col = lax.broadcasted_iota(jnp.int32, (B_BLOCK, B_KEYS), 1)
        dist = jnp.abs(q_row + B_HALF_WIN - k_col)
        dist_f = dist.astype(F32)
        for head, slope in enumerate(slopes):
            base = jnp.where(dist <= B_HALF_WIN, (-slope * LOG2E) * dist_f, NEG_INF)
            first = jnp.where(k_col >= B_HALF_WIN, base, NEG_INF)
            bias_ref[0, head] = base
            bias_ref[1, head] = first
            bias_ref[2, head] = jnp.where(k_col < B_HALF_WIN + B_BLOCK, base, NEG_INF)
            bias_ref[3, head] = jnp.where(k_col < B_HALF_WIN + B_BLOCK, first, NEG_INF)

    for buf, prev, cur, nxt in ((kbuf, kp_ref, kc_ref, kn_ref), (vbuf, vp_ref, vc_ref, vn_ref)):
        for r in range(n_planes):
            buf[r, 0:B_HALF_WIN, :] = prev[0, r]
            buf[r, B_HALF_WIN:B_HALF_WIN + step, :] = cur[0, r]
            buf[r, B_HALF_WIN + step:, :] = nxt[0, r]

    lane = lax.broadcasted_iota(jnp.int32, (B_BLOCK, LANES), 1)
    lanes_per_head = LANES // B_HEADS_PER_GROUP
    heads = [slice(h * B_HEAD_DIM, (h + 1) * B_HEAD_DIM) for h in range(B_HEADS_PER_GROUP)]
    nt = (((1,), (1,)), ((), ()))
    rows_per_pass = min(step, ATTN_CHUNK)
    planes_per_pass = ATTN_CHUNK // rows_per_pass
    passes_per_plane = step // rows_per_pass
    blocks = [(dr, j) for dr in range(planes_per_pass) for j in range(rows_per_pass // B_BLOCK)]

    def one_pass(t, carry):
        if planes_per_pass == 1:
            plane0 = t // passes_per_plane
            row0 = pl.multiple_of((t % passes_per_plane) * ATTN_CHUNK, ATTN_CHUNK)
        else:
            plane0, row0 = t * planes_per_pass, 0

        logits = {}
        for dr, j in blocks:
            for h, sl in enumerate(heads):
                q = q_ref[0, plane0 + dr, pl.ds(row0 + j * B_BLOCK, B_BLOCK), sl]
                k = kbuf[plane0 + dr, pl.ds(row0 + j * B_BLOCK, B_KEYS), sl]
                logits[dr, j, h] = lax.dot_general(q, k, nt, preferred_element_type=F32)

        probs = {}
        for dr, j in blocks:
            block = (i * step + row0) // B_BLOCK + j
            variant = (block == 0).astype(jnp.int32) + 2 * (block == n_blocks - 1).astype(jnp.int32)
            m_tile = den_tile = None
            for h in range(B_HEADS_PER_GROUP):
                s = logits[dr, j, h] + bias_ref[variant, h]
                m = jnp.max(s, axis=-1, keepdims=True)
                p = jnp.exp2(s - m)
                den = jnp.sum(p, axis=-1, keepdims=True)
                probs[dr, j, h] = (p.astype(BF16), 1.0 / den)
                m_tile = m if h == 0 else jnp.where(lane >= h * lanes_per_head, m, m_tile)
                den_tile = den if h == 0 else jnp.where(lane >= h * lanes_per_head, den, den_tile)
            lse_ref[0, plane0 + dr, pl.ds(row0 + j * B_BLOCK, B_BLOCK), :] = (
                (m_tile + jnp.log2(den_tile)) * LN2)

        for dr, j in blocks:
            for h, sl in enumerate(heads):
                p, rden = probs[dr, j, h]
                v = vbuf[plane0 + dr, pl.ds(row0 + j * B_BLOCK, B_KEYS), sl]
                o = jnp.dot(p, v, preferred_element_type=F32) * rden
                o_ref[0, plane0 + dr, pl.ds(row0 + j * B_BLOCK, B_BLOCK), sl] = o.astype(BF16)
        return carry

    lax.fori_loop(0, n_planes * step // ATTN_CHUNK, one_pass, 0)


def _attn_b_group(q, k, v, gi):
    n, dil, sub, _ = q.shape
    step = min(sub, B_ROWS_PER_STEP)
    n_planes = B_ROWS_PER_STEP // step
    assert sub % step == 0 and dil % n_planes == 0 and step % B_BLOCK == 0
    assert ATTN_CHUNK % step == 0 or step % ATTN_CHUNK == 0
    halo_per_step = step // B_HALF_WIN
    last_halo = sub // B_HALF_WIN - 1
    cur = lambda b, r, i: (b, r, i, 0)
    prev = lambda b, r, i: (b, r, jnp.maximum(i * halo_per_step - 1, 0), 0)
    nxt = lambda b, r, i: (b, r, jnp.minimum((i + 1) * halo_per_step, last_halo), 0)
    kv_specs = [pl.BlockSpec((1, n_planes, B_HALF_WIN, B_GROUP_W), prev),
                pl.BlockSpec((1, n_planes, step, B_GROUP_W), cur),
                pl.BlockSpec((1, n_planes, B_HALF_WIN, B_GROUP_W), nxt)]
    all_slopes = _alibi_slopes(len(B_GROUPS) * B_HEADS_PER_GROUP)
    slopes = [all_slopes[gi * B_HEADS_PER_GROUP + h] * dil for h in range(B_HEADS_PER_GROUP)]
    return pl.pallas_call(
        functools.partial(_attn_b_kernel, sub, slopes),
        grid=(n, dil // n_planes, sub // step),
        in_specs=[pl.BlockSpec((1, n_planes, step, B_GROUP_W), cur)] + kv_specs + kv_specs,
        out_specs=[pl.BlockSpec((1, n_planes, step, B_GROUP_W), cur),
                   pl.BlockSpec((1, n_planes, step, LANES), cur)],
        out_shape=[jax.ShapeDtypeStruct((n, dil, sub, B_GROUP_W), BF16),
                   jax.ShapeDtypeStruct((n, dil, sub, LANES), F32)],
        scratch_shapes=[pltpu.VMEM((n_planes, step + 2 * B_HALF_WIN, B_GROUP_W), BF16)] * 2
                       + [pltpu.VMEM((4, B_HEADS_PER_GROUP, B_BLOCK, B_KEYS), F32)],
        compiler_params=_params(3),
        name=f"attn_b{gi}",
    )(q, k, k, k, v, v, v)


M_STEP = 2048


def _attn_m_kernel(q_ref, k_ref, v_ref, o_ref):
    heads = [slice(h * M_HEAD_DIM, (h + 1) * M_HEAD_DIM) for h in range(M_HEADS)]
    nt = (((1,), (1,)), ((), ()))

    def chunk(t, carry):
        rows = pl.ds(pl.multiple_of(t * ATTN_CHUNK, ATTN_CHUNK), ATTN_CHUNK)
        logits = [lax.dot_general(q_ref[0, rows, sl], k_ref[0, :, sl], nt, preferred_element_type=F32)
                  for sl in heads]
        probs = []
        for s in logits:
            m = jnp.max(s, axis=-1, keepdims=True)
            p = jnp.exp2(s - m)
            probs.append((p.astype(BF16), 1.0 / jnp.sum(p, axis=-1, keepdims=True)))
        for sl, (p, rden) in zip(heads, probs):
            o = jnp.dot(p, v_ref[0, :, sl], preferred_element_type=F32) * rden
            o_ref[0, rows, sl] = o.astype(BF16)
        return carry

    lax.fori_loop(0, M_STEP // ATTN_CHUNK, chunk, 0)


def _attn_m(qm, mk, mv):
    n, s, _ = qm.shape
    cur = lambda b, i: (b, i, 0)
    mem = lambda b, i: (b, 0, 0)
    return pl.pallas_call(
        _attn_m_kernel,
        grid=(n, s // M_STEP),
        in_specs=[pl.BlockSpec((1, M_STEP, M_W), cur), pl.BlockSpec((1, N_MEM, M_W), mem),
                  pl.BlockSpec((1, N_MEM, M_W), mem)],
        out_specs=pl.BlockSpec((1, M_STEP, M_W), cur),
        out_shape=jax.ShapeDtypeStruct((n, s, M_W), BF16),
        compiler_params=_params(2),
        name="attn_m",
    )(qm, mk, mv)


MERGE_ROWS = 512
N_MERGE_SLABS = sum(B_GROUP_W // LANES + 1 for _, dil in B_GROUPS if dil > 1)


def _merge_kernel(x_ref, g_ref, oa_ref, o0_ref, o1_ref, o2_ref, l0_ref, l1_ref, l2_ref, om_ref,
                  wg_ref, bg_ref, wbr_ref, wout_ref, y_ref, nat):
    tm = x_ref.shape[1]
    x = x_ref[0]
    h = _rms_rows(x, g_ref[...]).astype(BF16)
    tiles = B_GROUP_W // LANES

    o_nat, lse_nat = [], []
    slab = 0
    for (_, dil), o_ref, l_ref in zip(B_GROUPS, (o0_ref, o1_ref, o2_ref), (l0_ref, l1_ref, l2_ref)):
        if dil == 1:
            o_nat.append([o_ref[0, 0, :, c * LANES:(c + 1) * LANES].astype(F32) for c in range(tiles)])
            lse_nat.append(l_ref[0, 0])
            continue
        for r in range(dil):
            rows = pl.ds(r, tm // dil, stride=dil)
            for c in range(tiles):
                nat[slab + c, rows, :] = o_ref[0, r, :, c * LANES:(c + 1) * LANES].astype(F32)
            nat[slab + tiles, rows, :] = l_ref[0, r]
        o_nat.append([nat[slab + c] for c in range(tiles)])
        lse_nat.append(nat[slab + tiles])
        slab += tiles + 1

    top = jnp.maximum(jnp.maximum(lse_nat[0], lse_nat[1]), lse_nat[2])
    es = [jnp.exp(l - top) for l in lse_nat]
    r_sum = 1.0 / (es[0] + es[1] + es[2])
    ws = [e * r_sum for e in es]
    lanes_per_head = LANES // B_HEADS_PER_GROUP
    ob_heads = []
    for hd in range(B_HEADS_PER_GROUP):
        acc = None
        for w, o in zip(ws, o_nat):
            term = w[:, hd * lanes_per_head:hd * lanes_per_head + 1] * o[hd]
            acc = term if acc is None else acc + term
        ob_heads.append(acc.astype(BF16))
    ob = jnp.concatenate(ob_heads, axis=1)

    z = None
    for b, o in enumerate((oa_ref[0], ob, om_ref[0])):
        logits = jnp.dot(h, wg_ref[:, b * D_MODEL:(b + 1) * D_MODEL], preferred_element_type=F32)
        gate = jax.nn.sigmoid(logits + bg_ref[b:b + 1, :])
        term = gate * jnp.dot(o, wbr_ref[b], preferred_element_type=F32)
        z = term if z is None else z + term
    y_ref[0] = x + jnp.dot(z.astype(BF16), wout_ref[...], preferred_element_type=F32)


def _merge(x, g_mix, oa, obs, lses, om, w_gate, b_gate, w_branch, w_out):
    n, s, _ = x.shape
    tm = MERGE_ROWS
    cur = lambda b, i: (b, i, 0)
    plane = lambda b, i: (b, 0, i, 0)
    rows = lambda w: pl.BlockSpec((1, tm, w), cur)
    planes = lambda w: [pl.BlockSpec((1, dil, tm // dil, w), plane) for _, dil in B_GROUPS]
    return pl.pallas_call(
        _merge_kernel,
        grid=(n, s // tm),
        in_specs=[rows(D_MODEL), _resident((1, D_MODEL)), rows(A_Q_W)]
                 + planes(B_GROUP_W) + planes(LANES) + [rows(M_W)]
                 + [_resident(w_gate.shape), _resident(b_gate.shape), _resident(w_branch.shape),
                    _resident(w_out.shape)],
        out_specs=rows(D_MODEL),
        out_shape=jax.ShapeDtypeStruct((n, s, D_MODEL), F32),
        scratch_shapes=[pltpu.VMEM((N_MERGE_SLABS, tm, LANES), F32)],
        compiler_params=_params(2),
        name="merge",
    )(x, g_mix, oa, *obs, *lses, om, w_gate, b_gate, w_branch, w_out)


MLP_ROWS = 512
MLP_FF_CHUNK = 1024


def _mlp_kernel(x_ref, g_ref, wup_ref, wdn_ref, y_ref):
    x = x_ref[...]
    h = _rms_rows(x, g_ref[...]).astype(BF16)
    acc = x
    for lo in range(0, D_FF, MLP_FF_CHUNK):
        u = jnp.dot(h, wup_ref[:, lo:lo + MLP_FF_CHUNK], preferred_element_type=F32)
        a = jnp.square(jnp.maximum(u, 0.0)).astype(BF16)
        acc = acc + jnp.dot(a, wdn_ref[lo:lo + MLP_FF_CHUNK, :], preferred_element_type=F32)
    y_ref[...] = acc


def _mlp(x2d, g_mlp, w_up, w_down):
    t = x2d.shape[0]
    tm = MLP_ROWS
    row = lambda i: (i, 0)
    return pl.pallas_call(
        _mlp_kernel,
        grid=(t // tm,),
        in_specs=[pl.BlockSpec((tm, D_MODEL), row), _resident((1, D_MODEL)),
                  _resident(w_up.shape), _resident(w_down.shape)],
        out_specs=pl.BlockSpec((tm, D_MODEL), row),
        out_shape=jax.ShapeDtypeStruct((t, D_MODEL), F32),
        compiler_params=_params(1),
        name="mlp",
    )(x2d, g_mlp, w_up, w_down)


def _tile2(g):
    return jnp.concatenate([g, g]).reshape(1, LANES)


def _layer(x, mem, g_mix, g_mem, w_in, b_gate, w_mem_kv, gq_a, gk_a, sink_a, gq_b, gk_b, gq_m, gk_m,
           w_branch, w_out, g_mlp, w_up, w_down):
    n, s, _ = x.shape
    row = lambda g: g.reshape(1, -1)

    bounds = np.cumsum((0, A_Q_W, A_KV_W, A_KV_W, B_W, B_W, B_W, M_W, GATE_W))
    seg = lambda a, b: w_in[:, int(bounds[a]):int(bounds[b])].astype(BF16)
    ws = (seg(0, 1), seg(1, 3), seg(3, 4), seg(4, 5), seg(5, 6), seg(6, 7))
    w_gate = seg(7, 8)
    gains = (_tile2(gq_a) * (A_HEAD_DIM ** -0.5 * LOG2E), _tile2(gk_a),
             row(gq_b) * (B_HEAD_DIM ** -0.5 * LOG2E), row(gk_b), row(gq_m) * (M_HEAD_DIM ** -0.5 * LOG2E))

    qa, ka, va, qm, *qkv_b = _proj(x, row(g_mix), ws, gains)
    mk, mv = _mem_kv(mem.reshape(n * N_MEM, D_MODEL), row(g_mem), w_mem_kv.astype(BF16), row(gk_m))

    oa = _attn_a(qa, ka, va, sink_a)
    b_out = [_attn_b_group(*qkv_b[3 * gi:3 * gi + 3], gi) for gi in range(len(B_GROUPS))]
    om = _attn_m(qm, mk.reshape(n, N_MEM, M_W), mv.reshape(n, N_MEM, M_W))

    x1 = _merge(x, row(g_mix), oa, [o for o, _ in b_out], [l for _, l in b_out], om,
                w_gate, b_gate, w_branch.astype(BF16), w_out.astype(BF16))
    y = _mlp(x1.reshape(n * s, D_MODEL), row(g_mlp), w_up.astype(BF16), w_down.astype(BF16))
    return y.reshape(n, s, D_MODEL)


def kernel(x_prompt, x_sample, mem_prompt, mem_sample, g_mix, g_mem, w_in, b_gate, w_mem_kv, gq_a, gk_a,
           sink_a, gq_b, gk_b, gq_m, gk_m, w_branch, w_out, g_mlp, w_up, w_down):
    depth = w_in.shape[0]

    def run(x, mem):
        for l in range(depth):
            x = _layer(x, mem, g_mix[l], g_mem[l], w_in[l], b_gate[l], w_mem_kv[l], gq_a[l], gk_a[l],
                       sink_a[l], gq_b[l], gk_b[l], gq_m[l], gk_m[l], w_branch[l], w_out[l], g_mlp[l],
                       w_up[l], w_down[l])
        return x

    return (run(x_prompt, mem_prompt), run(x_sample, mem_sample))
```

```python
import functools

import numpy as np
import jax
import jax.numpy as jnp
from jax import lax
from jax.experimental import pallas as pl
from jax.experimental.pallas import tpu as pltpu

D_MODEL = 1024
N_MEM = 256
A_HEADS = 8
A_KV_HEADS = 2
A_HEAD_DIM = 64
A_HALF_WIN = 128
B_GROUPS = ((128, 1), (512, 4), (2048, 16))
B_HEADS_PER_GROUP = 4
B_HEAD_DIM = 128
M_HEADS = 4
M_HEAD_DIM = 128
N_BRANCH = 3
BRANCH_WIDTH = D_MODEL // 2
D_FF = 4 * D_MODEL
EPS = 1e-6
NEG_INF = -1e30

A_Q_W = A_HEADS * A_HEAD_DIM
A_KV_W = A_KV_HEADS * A_HEAD_DIM
B_GROUP_W = B_HEADS_PER_GROUP * B_HEAD_DIM
B_W = len(B_GROUPS) * B_GROUP_W
M_W = M_HEADS * M_HEAD_DIM
GATE_W = N_BRANCH * D_MODEL

LANES = 128
B_HALF_WIN = 64
VMEM_LIMIT_BYTES = 56 * 1024 * 1024

BF16 = jnp.bfloat16
F32 = jnp.float32


def _alibi_slopes(n):
    return [float(2.0 ** (-8.0 * (i + 1) / n)) for i in range(n)]


def _params(n_grid_axes):
    return pltpu.CompilerParams(
        dimension_semantics=("arbitrary",) * n_grid_axes,
        vmem_limit_bytes=VMEM_LIMIT_BYTES)


def _resident(shape):
    zeros = (0,) * len(shape)
    return pl.BlockSpec(shape, lambda *_: zeros, pipeline_mode=pl.Buffered(1))


def _rms_rows(x, gain):
    ms = jnp.mean(x * x, axis=-1, keepdims=True)
    return x * lax.rsqrt(ms + EPS) * gain


def _head_norm128(blk, gain):
    ms = jnp.sum(blk * blk, axis=-1, keepdims=True) * (1.0 / LANES)
    return blk * lax.rsqrt(ms + EPS) * gain


def _head_norm64(blk, gain2):
    low = lax.broadcasted_iota(jnp.int32, blk.shape, 1) < A_HEAD_DIM
    sq = blk * blk
    ss_lo = jnp.sum(jnp.where(low, sq, 0.0), axis=-1, keepdims=True)
    ss_hi = jnp.sum(jnp.where(low, 0.0, sq), axis=-1, keepdims=True)
    ms = jnp.where(low, ss_lo, ss_hi) * (1.0 / A_HEAD_DIM)
    return blk * lax.rsqrt(ms + EPS) * gain2


PROJ_ROWS = 512
DEINTERLEAVE_STRIDE = 4
N_DILATED_SLABS = sum(3 * (B_GROUP_W // LANES) for _, dil in B_GROUPS if dil > 1)


def _proj_kernel(x_ref, g_ref, wqa_ref, wkva_ref, wqb_ref, wkb_ref, wvb_ref, wqm_ref,
                 gqa_ref, gka_ref, gqb_ref, gkb_ref, gqm_ref,
                 qa_ref, ka_ref, va_ref, qm_ref, *rest):
    b_refs, ybuf, h_ref = rest[:-2], rest[-2], rest[-1]
    tm = x_ref.shape[1]
    h_ref[...] = _rms_rows(x_ref[0], g_ref[...]).astype(BF16)

    def mm(w_ref, lo, hi):
        return lambda: jnp.dot(h_ref[...], w_ref[:, lo:hi], preferred_element_type=F32)

    def finish_qa(y):
        for c in range(A_Q_W // LANES):
            sl = slice(c * LANES, (c + 1) * LANES)
            qa_ref[0, :, sl] = _head_norm64(y[:, sl], gqa_ref[...]).astype(BF16)

    def finish_kva(y):
        ka_ref[0] = _head_norm64(y[:, :A_KV_W], gka_ref[...]).astype(BF16)
        va_ref[0] = y[:, A_KV_W:].astype(BF16)

    def finish_qm(y):
        for c in range(M_W // LANES):
            sl = slice(c * LANES, (c + 1) * LANES)
            qm_ref[0, :, sl] = _head_norm128(y[:, sl], gqm_ref[...]).astype(BF16)

    def finish_b(o_ref, gain_ref, dil, slab):
        def finish(y):
            for c in range(B_GROUP_W // LANES):
                sl = slice(c * LANES, (c + 1) * LANES)
                blk = y[:, sl]
                if gain_ref is not None:
                    blk = _head_norm128(blk, gain_ref[...])
                if dil == 1:
                    o_ref[0, 0, :, sl] = blk.astype(BF16)
                else:
                    ybuf[slab + c] = blk
                    groups, stride = 1, dil
                    while stride > DEINTERLEAVE_STRIDE:
                        rows = tm // groups
                        parts = [ybuf[slab + c, pl.ds(g * rows + r, rows // DEINTERLEAVE_STRIDE,
                                                      stride=DEINTERLEAVE_STRIDE), :]
                                 for g in range(groups) for r in range(DEINTERLEAVE_STRIDE)]
                        ybuf[slab + c] = jnp.concatenate(parts, axis=0)
                        groups, stride = groups * DEINTERLEAVE_STRIDE, stride // DEINTERLEAVE_STRIDE
                    rows = tm // groups
                    for g in range(groups):
                        for r in range(stride):
                            o_ref[0, g + groups * r, :, sl] = (
                                ybuf[slab + c, pl.ds(g * rows + r, rows // stride, stride=stride), :].astype(BF16))
        return finish

    def branch_b(gi, slab):
        dil = B_GROUPS[gi][1]
        lo = gi * B_GROUP_W
        parts = []
        for part, (w_ref, gain_ref) in enumerate(((wqb_ref, gqb_ref), (wkb_ref, gkb_ref), (wvb_ref, None))):
            parts.append((mm(w_ref, lo, lo + B_GROUP_W), finish_b(b_refs[3 * gi + part], gain_ref, dil, slab)))
            slab += B_GROUP_W // LANES
        return parts, slab

    by_dilation = sorted(range(len(B_GROUPS)), key=lambda gi: -B_GROUPS[gi][1])
    stages, slab = [], 0
    for gi in by_dilation[:-1]:
        parts, slab = branch_b(gi, slab)
        stages += parts
    stages += [(mm(wqa_ref, 0, A_Q_W), finish_qa), (mm(wkva_ref, 0, 2 * A_KV_W), finish_kva),
               (mm(wqm_ref, 0, M_W), finish_qm)]
    stages += branch_b(by_dilation[-1], slab)[0]

    pending = None
    for matmul, finish in stages:
        y = matmul()
        if pending is not None:
            pending[0](pending[1])
        pending = (finish, y)
    pending[0](pending[1])


def _proj(x, g_mix, ws, gains):
    n, s, _ = x.shape
    tm = PROJ_ROWS
    cur = lambda b, i: (b, i, 0)
    plane = lambda b, i: (b, 0, i, 0)
    in_specs = ([pl.BlockSpec((1, tm, D_MODEL), cur), _resident((1, D_MODEL))]
                + [_resident(w.shape) for w in ws]
                + [_resident((1, LANES)) for _ in gains])
    widths = (A_Q_W, A_KV_W, A_KV_W, M_W)
    out_specs = [pl.BlockSpec((1, tm, w), cur) for w in widths]
    out_shape = [jax.ShapeDtypeStruct((n, s, w), BF16) for w in widths]
    for _, dil in B_GROUPS:
        out_specs += [pl.BlockSpec((1, dil, tm // dil, B_GROUP_W), plane)] * 3
        out_shape += [jax.ShapeDtypeStruct((n, dil, s // dil, B_GROUP_W), BF16)] * 3
    return pl.pallas_call(
        _proj_kernel,
        grid=(n, s // tm),
        in_specs=in_specs,
        out_specs=out_specs,
        out_shape=out_shape,
        scratch_shapes=[pltpu.VMEM((N_DILATED_SLABS, tm, LANES), F32), pltpu.VMEM((tm, D_MODEL), BF16)],
        compiler_params=_params(2),
        name="proj",
    )(x, g_mix, *ws, *gains)


def _mem_kv_kernel(m_ref, g_ref, w_ref, gk_ref, k_ref, v_ref):
    h = _rms_rows(m_ref[...], g_ref[...]).astype(BF16)
    y = jnp.dot(h, w_ref[...], preferred_element_type=F32)
    for c in range(M_HEADS):
        sl = slice(c * LANES, (c + 1) * LANES)
        k_ref[:, sl] = _head_norm128(y[:, sl], gk_ref[...]).astype(BF16)
    v_ref[...] = y[:, M_W:].astype(BF16)


def _mem_kv(mem2d, g_mem, w_mem_kv, gk_m):
    t = mem2d.shape[0]
    row = lambda i: (i, 0)
    return pl.pallas_call(
        _mem_kv_kernel,
        grid=(t // N_MEM,),
        in_specs=[pl.BlockSpec((N_MEM, D_MODEL), row), _resident((1, D_MODEL)),
                  _resident(w_mem_kv.shape), _resident((1, LANES))],
        out_specs=[pl.BlockSpec((N_MEM, M_W), row)] * 2,
        out_shape=[jax.ShapeDtypeStruct((t, M_W), BF16)] * 2,
        compiler_params=_params(1),
        name="mem_kv",
    )(mem2d, g_mem, w_mem_kv, gk_m)


A_BLOCK = A_HALF_WIN
A_STEP = 2048
ATTN_CHUNK = 512
INTERIOR, FIRST, LAST = 0, 1, 2
LOG2E = float(np.log2(np.e))
LN2 = float(np.log(2.0))


def _attn_a_kernel(seq_len, q_ref, kp_ref, kc_ref, kn_ref, vp_ref, vc_ref, vn_ref, sink_ref,
                   o_ref, kbuf, vbuf, bias_ref):
    i = pl.program_id(1)
    n_keys = 3 * A_BLOCK
    n_blocks = seq_len // A_BLOCK

    @pl.when((pl.program_id(0) == 0) & (i == 0))
    def _():
        q_row = lax.broadcasted_iota(jnp.int32, (A_BLOCK, n_keys), 0)
        k_col = lax.broadcasted_iota(jnp.int32, (A_BLOCK, n_keys), 1)
        dist = jnp.abs(q_row + A_BLOCK - k_col)
        dist_f = dist.astype(F32)
        for head, slope in enumerate(_alibi_slopes(A_HEADS)):
            base = jnp.where(dist <= A_HALF_WIN, (-slope * LOG2E) * dist_f, NEG_INF)
            bias_ref[INTERIOR, head] = base
            bias_ref[FIRST, head] = jnp.where(k_col >= A_BLOCK, base, NEG_INF)
            bias_ref[LAST, head] = jnp.where(k_col < 2 * A_BLOCK, base, NEG_INF)

    for buf, prev, cur, nxt in ((kbuf, kp_ref, kc_ref, kn_ref), (vbuf, vp_ref, vc_ref, vn_ref)):
        buf[0:A_BLOCK, :] = prev[0]
        buf[A_BLOCK:A_BLOCK + A_STEP, :] = cur[0]
        buf[A_BLOCK + A_STEP:, :] = nxt[0]

    low = lax.broadcasted_iota(jnp.int32, (n_keys, LANES), 1) < A_HEAD_DIM
    low_q = lax.broadcasted_iota(jnp.int32, (A_BLOCK, LANES), 1) < A_HEAD_DIM
    zero = jnp.zeros((n_keys, LANES), BF16)
    nt = (((1,), (1,)), ((), ()))
    per_chunk = ATTN_CHUNK // A_BLOCK
    units = [(j, kvh) for j in range(per_chunk) for kvh in range(A_KV_HEADS)]

    def swap_halves(t):
        return jnp.concatenate([t[:, A_HEAD_DIM:], t[:, :A_HEAD_DIM]], axis=1)

    def padded(t, t_sw, kvh):
        lo, hi = (t, t_sw) if kvh == 0 else (t_sw, t)
        return jnp.concatenate([jnp.where(low, lo, zero), jnp.where(low, zero, hi)], axis=0)

    def chunk(t, carry):
        row0 = pl.multiple_of(t * ATTN_CHUNK, ATTN_CHUNK)
        logits, values = {}, {}
        for j in range(per_chunk):
            rows = pl.ds(row0 + j * A_BLOCK, A_BLOCK)
            kw = kbuf[pl.ds(row0 + j * A_BLOCK, n_keys), :]
            vw = vbuf[pl.ds(row0 + j * A_BLOCK, n_keys), :]
            kw_sw, vw_sw = swap_halves(kw), swap_halves(vw)
            for kvh in range(A_KV_HEADS):
                q2 = jnp.concatenate([q_ref[0, rows, (2 * kvh + c) * LANES:(2 * kvh + c + 1) * LANES]
                                      for c in range(2)], axis=0)
                logits[j, kvh] = lax.dot_general(q2, padded(kw, kw_sw, kvh), nt,
                                                 preferred_element_type=F32)
                values[j, kvh] = padded(vw, vw_sw, kvh)

        probs = {}
        for j, kvh in units:
            block = (i * A_STEP + row0) // A_BLOCK + j
            variant = jnp.where(block == 0, FIRST, jnp.where(block == n_blocks - 1, LAST, INTERIOR))
            s2 = logits[j, kvh]
            p_rows, r_rows = [], []
            for c in range(2):
                ps, sums, sink_gaps = [], [], []
                for half in range(2):
                    head = 2 * (2 * kvh + c) + half
                    sh = s2[c * A_BLOCK:(c + 1) * A_BLOCK, half * n_keys:(half + 1) * n_keys]
                    sh = sh + bias_ref[variant, head]
                    sink = sink_ref[head] * LOG2E
                    m = jnp.maximum(jnp.max(sh, axis=-1, keepdims=True), sink)
                    p = jnp.exp2(sh - m)
                    ps.append(p.astype(BF16))
                    sums.append(jnp.sum(p, axis=-1, keepdims=True))
                    sink_gaps.append(sink - m)
                den = jnp.where(low_q, sums[0], sums[1]) + jnp.exp2(jnp.where(low_q, sink_gaps[0], sink_gaps[1]))
                p_rows.append(jnp.concatenate(ps, axis=1))
                r_rows.append(1.0 / den)
            probs[j, kvh] = (jnp.concatenate(p_rows, axis=0), jnp.concatenate(r_rows, axis=0))

        for j, kvh in units:
            p2, r2 = probs[j, kvh]
            o2 = jnp.dot(p2, values[j, kvh], preferred_element_type=F32) * r2
            for c in range(2):
                tile = 2 * kvh + c
                o_ref[0, pl.ds(row0 + j * A_BLOCK, A_BLOCK), tile * LANES:(tile + 1) * LANES] = (
                    o2[c * A_BLOCK:(c + 1) * A_BLOCK].astype(BF16))
        return carry

    lax.fori_loop(0, A_STEP // ATTN_CHUNK, chunk, 0)


def _attn_a(qa, ka, va, sink):
    n, s, _ = qa.shape
    per_step = A_STEP // A_BLOCK
    n_blocks = s // A_BLOCK
    assert n_blocks >= 2 and s % A_STEP == 0
    cur = lambda b, i: (b, i, 0)
    prev = lambda b, i: (b, jnp.maximum(i * per_step - 1, 0), 0)
    nxt = lambda b, i: (b, jnp.minimum((i + 1) * per_step, n_blocks - 1), 0)
    kv_specs = [pl.BlockSpec((1, A_BLOCK, A_KV_W), prev), pl.BlockSpec((1, A_STEP, A_KV_W), cur),
                pl.BlockSpec((1, A_BLOCK, A_KV_W), nxt)]
    return pl.pallas_call(
        functools.partial(_attn_a_kernel, s),
        grid=(n, s // A_STEP),
        in_specs=[pl.BlockSpec((1, A_STEP, A_Q_W), cur)] + kv_specs + kv_specs
                 + [pl.BlockSpec(memory_space=pltpu.SMEM)],
        out_specs=pl.BlockSpec((1, A_STEP, A_Q_W), cur),
        out_shape=jax.ShapeDtypeStruct((n, s, A_Q_W), BF16),
        scratch_shapes=[pltpu.VMEM((A_STEP + 2 * A_BLOCK, A_KV_W), BF16)] * 2
                       + [pltpu.VMEM((3, A_HEADS, A_BLOCK, 3 * A_BLOCK), F32)],
        compiler_params=_params(2),
        name="attn_a",
    )(qa, ka, ka, ka, va, va, va, sink)


B_BLOCK = 2 * B_HALF_WIN
B_KEYS = B_BLOCK + 2 * B_HALF_WIN
B_ROWS_PER_STEP = 2048


def _attn_b_kernel(sub_len, slopes, q_ref, kp_ref, kc_ref, kn_ref, vp_ref, vc_ref, vn_ref,
                   o_ref, lse_ref, kbuf, vbuf, bias_ref):
    i = pl.program_id(2)
    n_planes, step = q_ref.shape[1], q_ref.shape[2]
    n_blocks = sub_len // B_BLOCK

    @pl.when((pl.program_id(0) == 0) & (pl.program_id(1) == 0) & (i == 0))
    def _():
        q_row = lax.broadcasted_iota(jnp.int32, (B_BLOCK, B_KEYS), 0)
        k_col = lax.broadcasted_iota(jnp.int32, (B_BLOCK, B_KEYS), 1)
        dist = jnp.abs(q_row + B_HALF_WIN - k_col)
        dist_f = dist.astype(F32)
        for head, slope in enumerate(slopes):
            base = jnp.where(dist <= B_HALF_WIN, (-slope * LOG2E) * dist_f, NEG_INF)
            first = jnp.where(k_col >= B_HALF_WIN, base, NEG_INF)
            bias_ref[0, head] = base
            bias_ref[1, head] = first
            bias_ref[2, head] = jnp.where(k_col < B_HALF_WIN + B_BLOCK, base, NEG_INF)
            bias_ref[3, head] = jnp.where(k_col < B_HALF_WIN + B_BLOCK, first, NEG_INF)

    for buf, prev, cur, nxt in ((kbuf, kp_ref, kc_ref, kn_ref), (vbuf, vp_ref, vc_ref, vn_ref)):
        for r in range(n_planes):
            buf[r, 0:B_HALF_WIN, :] = prev[0, r]
            buf[r, B_HALF_WIN:B_HALF_WIN + step, :] = cur[0, r]
            buf[r, B_HALF_WIN + step:, :] = nxt[0, r]

    lane = lax.broadcasted_iota(jnp.int32, (B_BLOCK, LANES), 1)
    lanes_per_head = LANES // B_HEADS_PER_GROUP
    heads = [slice(h * B_HEAD_DIM, (h + 1) * B_HEAD_DIM) for h in range(B_HEADS_PER_GROUP)]
    nt = (((1,), (1,)), ((), ()))
    rows_per_pass = min(step, ATTN_CHUNK)
    planes_per_pass = ATTN_CHUNK // rows_per_pass
    passes_per_plane = step // rows_per_pass
    blocks = [(dr, j) for dr in range(planes_per_pass) for j in range(rows_per_pass // B_BLOCK)]

    def one_pass(t, carry):
        if planes_per_pass == 1:
            plane0 = t // passes_per_plane
            row0 = pl.multiple_of((t % passes_per_plane) * ATTN_CHUNK, ATTN_CHUNK)
        else:
            plane0, row0 = t * planes_per_pass, 0

        logits = {}
        for dr, j in blocks:
            for h, sl in enumerate(heads):
                q = q_ref[0, plane0 + dr, pl.ds(row0 + j * B_BLOCK, B_BLOCK), sl]
                k = kbuf[plane0 + dr, pl.ds(row0 + j * B_BLOCK, B_KEYS), sl]
                logits[dr, j, h] = lax.dot_general(q, k, nt, preferred_element_type=F32)

        probs = {}
        for dr, j in blocks:
            block = (i * step + row0) // B_BLOCK + j
            variant = (block == 0).astype(jnp.int32) + 2 * (block == n_blocks - 1).astype(jnp.int32)
            m_tile = den_tile = None
            for h in range(B_HEADS_PER_GROUP):
                s = logits[dr, j, h] + bias_ref[variant, h]
                m = jnp.max(s, axis=-1, keepdims=True)
                p = jnp.exp2(s - m)
                den = jnp.sum(p, axis=-1, keepdims=True)
                probs[dr, j, h] = (p.astype(BF16), 1.0 / den)
                m_tile = m if h == 0 else jnp.where(lane >= h * lanes_per_head, m, m_tile)
                den_tile = den if h == 0 else jnp.where(lane >= h * lanes_per_head, den, den_tile)
            lse_ref[0, plane0 + dr, pl.ds(row0 + j * B_BLOCK, B_BLOCK), :] = (
                (m_tile + jnp.log2(den_tile)) * LN2)

        for dr, j in blocks:
            for h, sl in enumerate(heads):
                p, rden = probs[dr, j, h]
                v = vbuf[plane0 + dr, pl.ds(row0 + j * B_BLOCK, B_KEYS), sl]
                o = jnp.dot(p, v, preferred_element_type=F32) * rden
                o_ref[0, plane0 + dr, pl.ds(row0 + j * B_BLOCK, B_BLOCK), sl] = o.astype(BF16)
        return carry

    lax.fori_loop(0, n_planes * step // ATTN_CHUNK, one_pass, 0)


def _attn_b_group(q, k, v, gi):
    n, dil, sub, _ = q.shape
    step = min(sub, B_ROWS_PER_STEP)
    n_planes = B_ROWS_PER_STEP // step
    assert sub % step == 0 and dil % n_planes == 0 and step % B_BLOCK == 0
    assert ATTN_CHUNK % step == 0 or step % ATTN_CHUNK == 0
    halo_per_step = step // B_HALF_WIN
    last_halo = sub // B_HALF_WIN - 1
    cur = lambda b, r, i: (b, r, i, 0)
    prev = lambda b, r, i: (b, r, jnp.maximum(i * halo_per_step - 1, 0), 0)
    nxt = lambda b, r, i: (b, r, jnp.minimum((i + 1) * halo_per_step, last_halo), 0)
    kv_specs = [pl.BlockSpec((1, n_planes, B_HALF_WIN, B_GROUP_W), prev),
                pl.BlockSpec((1, n_planes, step, B_GROUP_W), cur),
                pl.BlockSpec((1, n_planes, B_HALF_WIN, B_GROUP_W), nxt)]
    all_slopes = _alibi_slopes(len(B_GROUPS) * B_HEADS_PER_GROUP)
    slopes = [all_slopes[gi * B_HEADS_PER_GROUP + h] * dil for h in range(B_HEADS_PER_GROUP)]
    return pl.pallas_call(
        functools.partial(_attn_b_kernel, sub, slopes),
        grid=(n, dil // n_planes, sub // step),
        in_specs=[pl.BlockSpec((1, n_planes, step, B_GROUP_W), cur)] + kv_specs + kv_specs,
        out_specs=[pl.BlockSpec((1, n_planes, step, B_GROUP_W), cur),
                   pl.BlockSpec((1, n_planes, step, LANES), cur)],
        out_shape=[jax.ShapeDtypeStruct((n, dil, sub, B_GROUP_W), BF16),
                   jax.ShapeDtypeStruct((n, dil, sub, LANES), F32)],
        scratch_shapes=[pltpu.VMEM((n_planes, step + 2 * B_HALF_WIN, B_GROUP_W), BF16)] * 2
                       + [pltpu.VMEM((4, B_HEADS_PER_GROUP, B_BLOCK, B_KEYS), F32)],
        compiler_params=_params(3),
        name=f"attn_b{gi}",
    )(q, k, k, k, v, v, v)


M_STEP = 2048


def _attn_m_kernel(q_ref, k_ref, v_ref, o_ref):
    heads = [slice(h * M_HEAD_DIM, (h + 1) * M_HEAD_DIM) for h in range(M_HEADS)]
    nt = (((1,), (1,)), ((), ()))

    def chunk(t, carry):
        rows = pl.ds(pl.multiple_of(t * ATTN_CHUNK, ATTN_CHUNK), ATTN_CHUNK)
        logits = [lax.dot_general(q_ref[0, rows, sl], k_ref[0, :, sl], nt, preferred_element_type=F32)
                  for sl in heads]
        probs = []
        for s in logits:
            m = jnp.max(s, axis=-1, keepdims=True)
            p = jnp.exp2(s - m)
            probs.append((p.astype(BF16), 1.0 / jnp.sum(p, axis=-1, keepdims=True)))
        for sl, (p, rden) in zip(heads, probs):
            o = jnp.dot(p, v_ref[0, :, sl], preferred_element_type=F32) * rden
            o_ref[0, rows, sl] = o.astype(BF16)
        return carry

    lax.fori_loop(0, M_STEP // ATTN_CHUNK, chunk, 0)


def _attn_m(qm, mk, mv):
    n, s, _ = qm.shape
    cur = lambda b, i: (b, i, 0)
    mem = lambda b, i: (b, 0, 0)
    return pl.pallas_call(
        _attn_m_kernel,
        grid=(n, s // M_STEP),
        in_specs=[pl.BlockSpec((1, M_STEP, M_W), cur), pl.BlockSpec((1, N_MEM, M_W), mem),
                  pl.BlockSpec((1, N_MEM, M_W), mem)],
        out_specs=pl.BlockSpec((1, M_STEP, M_W), cur),
        out_shape=jax.ShapeDtypeStruct((n, s, M_W), BF16),
        compiler_params=_params(2),
        name="attn_m",
    )(qm, mk, mv)


MERGE_ROWS = 512
assert all(dil <= DEINTERLEAVE_STRIDE ** 2 for _, dil in B_GROUPS)
N_MERGE_SLABS = sum((B_GROUP_W // LANES + 1) * (1 if dil <= DEINTERLEAVE_STRIDE else 2)
                    for _, dil in B_GROUPS if dil > 1)


def _merge_kernel(x_ref, g_ref, oa_ref, o0_ref, o1_ref, o2_ref, l0_ref, l1_ref, l2_ref, om_ref,
                  wg_ref, bg_ref, wbr_ref, wout_ref, y_ref, nat):
    tm = x_ref.shape[1]
    x = x_ref[0]
    h = _rms_rows(x, g_ref[...]).astype(BF16)
    tiles = B_GROUP_W // LANES

    o_nat, lse_nat = [], []
    slab = 0
    for (_, dil), o_ref, l_ref in zip(B_GROUPS, (o0_ref, o1_ref, o2_ref), (l0_ref, l1_ref, l2_ref)):
        if dil == 1:
            o_nat.append([o_ref[0, 0, :, c * LANES:(c + 1) * LANES].astype(F32) for c in range(tiles)])
            lse_nat.append(l_ref[0, 0])
            continue
        def plane_tile(r, c):
            if c == tiles:
                return l_ref[0, r]
            return o_ref[0, r, :, c * LANES:(c + 1) * LANES].astype(F32)

        for c in range(tiles + 1):
            if dil <= DEINTERLEAVE_STRIDE:
                for r in range(dil):
                    nat[slab + c, pl.ds(r, tm // dil, stride=dil), :] = plane_tile(r, c)
                continue
            groups = DEINTERLEAVE_STRIDE
            inner, rows = dil // groups, tm // groups
            spare = slab + tiles + 1 + c
            for g in range(groups):
                for r in range(inner):
                    nat[spare, pl.ds(g * rows + r, rows // inner, stride=inner), :] = plane_tile(g + groups * r, c)
            for g in range(groups):
                nat[slab + c, pl.ds(g, rows, stride=groups), :] = nat[spare, g * rows:(g + 1) * rows, :]
        o_nat.append([nat[slab + c] for c in range(tiles)])
        lse_nat.append(nat[slab + tiles])
        slab += (tiles + 1) * (1 if dil <= DEINTERLEAVE_STRIDE else 2)

    top = jnp.maximum(jnp.maximum(lse_nat[0], lse_nat[1]), lse_nat[2])
    es = [jnp.exp(l - top) for l in lse_nat]
    r_sum = 1.0 / (es[0] + es[1] + es[2])
    ws = [e * r_sum for e in es]
    lanes_per_head = LANES // B_HEADS_PER_GROUP
    ob_heads = []
    for hd in range(B_HEADS_PER_GROUP):
        acc = None
        for w, o in zip(ws, o_nat):
            term = w[:, hd * lanes_per_head:hd * lanes_per_head + 1] * o[hd]
            acc = term if acc is None else acc + term
        ob_heads.append(acc.astype(BF16))
    ob = jnp.concatenate(ob_heads, axis=1)

    z = None
    for b, o in enumerate((oa_ref[0], ob, om_ref[0])):
        logits = jnp.dot(h, wg_ref[:, b * D_MODEL:(b + 1) * D_MODEL], preferred_element_type=F32)
        gate = jax.nn.sigmoid(logits + bg_ref[b:b + 1, :])
        term = gate * jnp.dot(o, wbr_ref[b], preferred_element_type=F32)
        z = term if z is None else z + term
    y_ref[0] = x + jnp.dot(z.astype(BF16), wout_ref[...], preferred_element_type=F32)


def _merge(x, g_mix, oa, obs, lses, om, w_gate, b_gate, w_branch, w_out):
    n, s, _ = x.shape
    tm = MERGE_ROWS
    cur = lambda b, i: (b, i, 0)
    plane = lambda b, i: (b, 0, i, 0)
    rows = lambda w: pl.BlockSpec((1, tm, w), cur)
    planes = lambda w: [pl.BlockSpec((1, dil, tm // dil, w), plane) for _, dil in B_GROUPS]
    return pl.pallas_call(
        _merge_kernel,
        grid=(n, s // tm),
        in_specs=[rows(D_MODEL), _resident((1, D_MODEL)), rows(A_Q_W)]
                 + planes(B_GROUP_W) + planes(LANES) + [rows(M_W)]
                 + [_resident(w_gate.shape), _resident(b_gate.shape), _resident(w_branch.shape),
                    _resident(w_out.shape)],
        out_specs=rows(D_MODEL),
        out_shape=jax.ShapeDtypeStruct((n, s, D_MODEL), F32),
        scratch_shapes=[pltpu.VMEM((N_MERGE_SLABS, tm, LANES), F32)],
        compiler_params=_params(2),
        name="merge",
    )(x, g_mix, oa, *obs, *lses, om, w_gate, b_gate, w_branch, w_out)


MLP_ROWS = 1024
MLP_FF_CHUNK = 1024


def _mlp_kernel(x_ref, g_ref, wup_ref, wdn_ref, y_ref):
    x = x_ref[...]
    h = _rms_rows(x, g_ref[...]).astype(BF16)
    acc = x
    for lo in range(0, D_FF, MLP_FF_CHUNK):
        u = jnp.dot(h, wup_ref[:, lo:lo + MLP_FF_CHUNK], preferred_element_type=F32)
        a = jnp.square(jnp.maximum(u, 0.0)).astype(BF16)
        acc = acc + jnp.dot(a, wdn_ref[lo:lo + MLP_FF_CHUNK, :], preferred_element_type=F32)
    y_ref[...] = acc


def _mlp(x2d, g_mlp, w_up, w_down):
    t = x2d.shape[0]
    tm = MLP_ROWS
    row = lambda i: (i, 0)
    return pl.pallas_call(
        _mlp_kernel,
        grid=(t // tm,),
        in_specs=[pl.BlockSpec((tm, D_MODEL), row), _resident((1, D_MODEL)),
                  _resident(w_up.shape), _resident(w_down.shape)],
        out_specs=pl.BlockSpec((tm, D_MODEL), row),
        out_shape=jax.ShapeDtypeStruct((t, D_MODEL), F32),
        compiler_params=_params(1),
        name="mlp",
    )(x2d, g_mlp, w_up, w_down)


def _tile2(g):
    return jnp.concatenate([g, g]).reshape(1, LANES)


def _layer(x, mem, g_mix, g_mem, w_in, b_gate, w_mem_kv, gq_a, gk_a, sink_a, gq_b, gk_b, gq_m, gk_m,
           w_branch, w_out, g_mlp, w_up, w_down):
    n, s, _ = x.shape
    row = lambda g: g.reshape(1, -1)

    bounds = np.cumsum((0, A_Q_W, A_KV_W, A_KV_W, B_W, B_W, B_W, M_W, GATE_W))
    seg = lambda a, b: w_in[:, int(bounds[a]):int(bounds[b])].astype(BF16)
    ws = (seg(0, 1), seg(1, 3), seg(3, 4), seg(4, 5), seg(5, 6), seg(6, 7))
    w_gate = seg(7, 8)
    gains = (_tile2(gq_a) * (A_HEAD_DIM ** -0.5 * LOG2E), _tile2(gk_a),
             row(gq_b) * (B_HEAD_DIM ** -0.5 * LOG2E), row(gk_b), row(gq_m) * (M_HEAD_DIM ** -0.5 * LOG2E))

    qa, ka, va, qm, *qkv_b = _proj(x, row(g_mix), ws, gains)
    mk, mv = _mem_kv(mem.reshape(n * N_MEM, D_MODEL), row(g_mem), w_mem_kv.astype(BF16), row(gk_m))

    oa = _attn_a(qa, ka, va, sink_a)
    b_out = [_attn_b_group(*qkv_b[3 * gi:3 * gi + 3], gi) for gi in range(len(B_GROUPS))]
    om = _attn_m(qm, mk.reshape(n, N_MEM, M_W), mv.reshape(n, N_MEM, M_W))

    x1 = _merge(x, row(g_mix), oa, [o for o, _ in b_out], [l for _, l in b_out], om,
                w_gate, b_gate, w_branch.astype(BF16), w_out.astype(BF16))
    y = _mlp(x1.reshape(n * s, D_MODEL), row(g_mlp), w_up.astype(BF16), w_down.astype(BF16))
    return y.reshape(n, s, D_MODEL)


def kernel(x_prompt, x_sample, mem_prompt, mem_sample, g_mix, g_mem, w_in, b_gate, w_mem_kv, gq_a, gk_a,
           sink_a, gq_b, gk_b, gq_m, gk_m, w_branch, w_out, g_mlp, w_up, w_down):
    depth = w_in.shape[0]

    def run(x, mem):
        for l in range(depth):
            x = _layer(x, mem, g_mix[l], g_mem[l], w_in[l], b_gate[l], w_mem_kv[l], gq_a[l], gk_a[l],
                       sink_a[l], gq_b[l], gk_b[l], gq_m[l], gk_m[l], w_branch[l], w_out[l], g_mlp[l],
                       w_up[l], w_down[l])
        return x

    return (run(x_prompt, mem_prompt), run(x_sample, mem_sample))
```

```python
import functools

import numpy as np
import jax
import jax.numpy as jnp
from jax import lax
from jax.experimental import pallas as pl
from jax.experimental.pallas import tpu as pltpu

D_MODEL = 1024
N_MEM = 256
A_HEADS = 8
A_KV_HEADS = 2
A_HEAD_DIM = 64
A_HALF_WIN = 128
B_GROUPS = ((128, 1), (512, 4), (2048, 16))
B_HEADS_PER_GROUP = 4
B_HEAD_DIM = 128
M_HEADS = 4
M_HEAD_DIM = 128
N_BRANCH = 3
BRANCH_WIDTH = D_MODEL // 2
D_FF = 4 * D_MODEL
EPS = 1e-6
NEG_INF = -1e30

A_Q_W = A_HEADS * A_HEAD_DIM
A_KV_W = A_KV_HEADS * A_HEAD_DIM
B_GROUP_W = B_HEADS_PER_GROUP * B_HEAD_DIM
B_W = len(B_GROUPS) * B_GROUP_W
M_W = M_HEADS * M_HEAD_DIM
GATE_W = N_BRANCH * D_MODEL

LANES = 128
B_HALF_WIN = 64
VMEM_LIMIT_BYTES = 56 * 1024 * 1024

BF16 = jnp.bfloat16
F32 = jnp.float32


def _alibi_slopes(n):
    return [float(2.0 ** (-8.0 * (i + 1) / n)) for i in range(n)]


def _params(n_grid_axes):
    return pltpu.CompilerParams(
        dimension_semantics=("arbitrary",) * n_grid_axes,
        vmem_limit_bytes=VMEM_LIMIT_BYTES)


def _resident(shape):
    zeros = (0,) * len(shape)
    return pl.BlockSpec(shape, lambda *_: zeros, pipeline_mode=pl.Buffered(1))


def _rms_rows(x, gain):
    ms = jnp.mean(x * x, axis=-1, keepdims=True)
    return x * lax.rsqrt(ms + EPS) * gain


def _head_norm128(blk, gain):
    ms = jnp.sum(blk * blk, axis=-1, keepdims=True) * (1.0 / LANES)
    return blk * lax.rsqrt(ms + EPS) * gain


def _head_norm64(blk, gain2):
    low = lax.broadcasted_iota(jnp.int32, blk.shape, 1) < A_HEAD_DIM
    sq = blk * blk
    ss_lo = jnp.sum(jnp.where(low, sq, 0.0), axis=-1, keepdims=True)
    ss_hi = jnp.sum(jnp.where(low, 0.0, sq), axis=-1, keepdims=True)
    ms = jnp.where(low, ss_lo, ss_hi) * (1.0 / A_HEAD_DIM)
    return blk * lax.rsqrt(ms + EPS) * gain2


PROJ_ROWS = 512
DEINTERLEAVE_STRIDE = 4
N_DILATED_SLABS = sum(3 * (B_GROUP_W // LANES) for _, dil in B_GROUPS if dil > 1)


def _proj_kernel(x_ref, g_ref, wqa_ref, wkva_ref, wqb_ref, wkb_ref, wvb_ref, wqm_ref,
                 gqa_ref, gka_ref, gqb_ref, gkb_ref, gqm_ref,
                 qa_ref, ka_ref, va_ref, qm_ref, *rest):
    b_refs, ybuf = rest[:-1], rest[-1]
    tm = x_ref.shape[1]
    h = _rms_rows(x_ref[0], g_ref[...]).astype(BF16)

    def mm(w_ref, lo, hi):
        return lambda: jnp.dot(h, w_ref[:, lo:hi], preferred_element_type=F32)

    def finish_qa(y):
        for c in range(A_Q_W // LANES):
            sl = slice(c * LANES, (c + 1) * LANES)
            qa_ref[0, :, sl] = _head_norm64(y[:, sl], gqa_ref[...]).astype(BF16)

    def finish_kva(y):
        ka_ref[0] = _head_norm64(y[:, :A_KV_W], gka_ref[...]).astype(BF16)
        va_ref[0] = y[:, A_KV_W:].astype(BF16)

    def finish_qm(y):
        for c in range(M_W // LANES):
            sl = slice(c * LANES, (c + 1) * LANES)
            qm_ref[0, :, sl] = _head_norm128(y[:, sl], gqm_ref[...]).astype(BF16)

    def finish_b(o_ref, gain_ref, dil, slab):
        def finish(y):
            for c in range(B_GROUP_W // LANES):
                sl = slice(c * LANES, (c + 1) * LANES)
                blk = y[:, sl]
                if gain_ref is not None:
                    blk = _head_norm128(blk, gain_ref[...])
                if dil == 1:
                    o_ref[0, 0, :, sl] = blk.astype(BF16)
                else:
                    ybuf[slab + c] = blk
                    groups, stride = 1, dil
                    while stride > DEINTERLEAVE_STRIDE:
                        rows = tm // groups
                        parts = [ybuf[slab + c, pl.ds(g * rows + r, rows // DEINTERLEAVE_STRIDE,
                                                      stride=DEINTERLEAVE_STRIDE), :]
                                 for g in range(groups) for r in range(DEINTERLEAVE_STRIDE)]
                        ybuf[slab + c] = jnp.concatenate(parts, axis=0)
                        groups, stride = groups * DEINTERLEAVE_STRIDE, stride // DEINTERLEAVE_STRIDE
                    rows = tm // groups
                    for g in range(groups):
                        for r in range(stride):
                            o_ref[0, g + groups * r, :, sl] = (
                                ybuf[slab + c, pl.ds(g * rows + r, rows // stride, stride=stride), :].astype(BF16))
        return finish

    def branch_b(gi, slab):
        dil = B_GROUPS[gi][1]
        lo = gi * B_GROUP_W
        parts = []
        for part, (w_ref, gain_ref) in enumerate(((wqb_ref, gqb_ref), (wkb_ref, gkb_ref), (wvb_ref, None))):
            parts.append((mm(w_ref, lo, lo + B_GROUP_W), finish_b(b_refs[3 * gi + part], gain_ref, dil, slab)))
            slab += B_GROUP_W // LANES
        return parts, slab

    by_dilation = sorted(range(len(B_GROUPS)), key=lambda gi: -B_GROUPS[gi][1])
    stages, slab = [], 0
    for gi in by_dilation[:-1]:
        parts, slab = branch_b(gi, slab)
        stages += parts
    stages += [(mm(wqa_ref, 0, A_Q_W), finish_qa), (mm(wkva_ref, 0, 2 * A_KV_W), finish_kva),
               (mm(wqm_ref, 0, M_W), finish_qm)]
    stages += branch_b(by_dilation[-1], slab)[0]

    pending = None
    for matmul, finish in stages:
        y = matmul()
        if pending is not None:
            pending[0](pending[1])
        pending = (finish, y)
    pending[0](pending[1])


def _proj(x, g_mix, ws, gains):
    n, s, _ = x.shape
    tm = PROJ_ROWS
    cur = lambda b, i: (b, i, 0)
    plane = lambda b, i: (b, 0, i, 0)
    in_specs = ([pl.BlockSpec((1, tm, D_MODEL), cur), _resident((1, D_MODEL))]
                + [_resident(w.shape) for w in ws]
                + [_resident((1, LANES)) for _ in gains])
    widths = (A_Q_W, A_KV_W, A_KV_W, M_W)
    out_specs = [pl.BlockSpec((1, tm, w), cur) for w in widths]
    out_shape = [jax.ShapeDtypeStruct((n, s, w), BF16) for w in widths]
    for _, dil in B_GROUPS:
        out_specs += [pl.BlockSpec((1, dil, tm // dil, B_GROUP_W), plane)] * 3
        out_shape += [jax.ShapeDtypeStruct((n, dil, s // dil, B_GROUP_W), BF16)] * 3
    return pl.pallas_call(
        _proj_kernel,
        grid=(n, s // tm),
        in_specs=in_specs,
        out_specs=out_specs,
        out_shape=out_shape,
        scratch_shapes=[pltpu.VMEM((N_DILATED_SLABS, tm, LANES), F32)],
        compiler_params=_params(2),
        name="proj",
    )(x, g_mix, *ws, *gains)


MEM_ROWS = 1024


def _mem_kv_kernel(m_ref, g_ref, w_ref, gk_ref, k_ref, v_ref):
    h = _rms_rows(m_ref[...], g_ref[...]).astype(BF16)
    y = jnp.dot(h, w_ref[...], preferred_element_type=F32)
    for c in range(M_HEADS):
        sl = slice(c * LANES, (c + 1) * LANES)
        k_ref[:, sl] = _head_norm128(y[:, sl], gk_ref[...]).astype(BF16)
    v_ref[...] = y[:, M_W:].astype(BF16)


def _mem_kv(mem2d, g_mem, w_mem_kv, gk_m):
    t = mem2d.shape[0]
    tm = min(t, MEM_ROWS)
    row = lambda i: (i, 0)
    return pl.pallas_call(
        _mem_kv_kernel,
        grid=(t // tm,),
        in_specs=[pl.BlockSpec((tm, D_MODEL), row), _resident((1, D_MODEL)),
                  _resident(w_mem_kv.shape), _resident((1, LANES))],
        out_specs=[pl.BlockSpec((tm, M_W), row)] * 2,
        out_shape=[jax.ShapeDtypeStruct((t, M_W), BF16)] * 2,
        compiler_params=_params(1),
        name="mem_kv",
    )(mem2d, g_mem, w_mem_kv, gk_m)


A_BLOCK = A_HALF_WIN
A_STEP = 2048
ATTN_CHUNK = 512
INTERIOR, FIRST, LAST = 0, 1, 2
LOG2E = float(np.log2(np.e))
LN2 = float(np.log(2.0))

SOFTMAX_SHIFT_LIMIT = 40.0
BF16_SLACK = 1.0 + 2.0 ** -6


def _logit_bound(gain_q, gain_k, head_dim):
    return head_dim * jnp.max(jnp.abs(gain_q)) * jnp.max(jnp.abs(gain_k)) * BF16_SLACK


def _softmax_ctl(shifts):
    shifts = jnp.asarray(shifts, F32).reshape(-1)
    fixed = jnp.all(shifts <= SOFTMAX_SHIFT_LIMIT)
    return jnp.concatenate([fixed.astype(F32).reshape(1), shifts])


def _by_softmax_shift(ctl_ref, run):
    fixed = ctl_ref[0] > 0.5

    @pl.when(fixed)
    def _():
        run(True)

    @pl.when(jnp.logical_not(fixed))
    def _():
        run(False)


def _attn_a_init_bias(bias_ref, ctl_ref):
    n_keys = 3 * A_BLOCK
    q_row = lax.broadcasted_iota(jnp.int32, (A_BLOCK, n_keys), 0)
    k_col = lax.broadcasted_iota(jnp.int32, (A_BLOCK, n_keys), 1)
    dist = jnp.abs(q_row + A_BLOCK - k_col)
    dist_f = dist.astype(F32)
    for head, slope in enumerate(_alibi_slopes(A_HEADS)):
        shift = jnp.where(ctl_ref[0] > 0.5, ctl_ref[1 + head], 0.0)
        base = jnp.where(dist <= A_HALF_WIN, (-slope * LOG2E) * dist_f - shift, NEG_INF)
        bias_ref[INTERIOR, head] = base
        bias_ref[FIRST, head] = jnp.where(k_col >= A_BLOCK, base, NEG_INF)
        bias_ref[LAST, head] = jnp.where(k_col < 2 * A_BLOCK, base, NEG_INF)


def _attn_a_load_kv(kv_refs, kbuf, vbuf):
    kp_ref, kc_ref, kn_ref, vp_ref, vc_ref, vn_ref = kv_refs
    rows = kc_ref.shape[1]
    for buf, prev, cur, nxt in ((kbuf, kp_ref, kc_ref, kn_ref), (vbuf, vp_ref, vc_ref, vn_ref)):
        buf[0:A_BLOCK, :] = prev[0]
        buf[A_BLOCK:A_BLOCK + rows, :] = cur[0]
        buf[A_BLOCK + rows:, :] = nxt[0]


def _attn_a_pass(row0, first_block, n_blocks, fixed_shift, q_ref, kbuf, vbuf, bias_ref, sink_ref, ctl_ref,
                 o_ref):
    n_keys = 3 * A_BLOCK
    low = lax.broadcasted_iota(jnp.int32, (n_keys, LANES), 1) < A_HEAD_DIM
    low_q = lax.broadcasted_iota(jnp.int32, (A_BLOCK, LANES), 1) < A_HEAD_DIM
    zero = jnp.zeros((n_keys, LANES), BF16)
    nt = (((1,), (1,)), ((), ()))
    per_chunk = ATTN_CHUNK // A_BLOCK
    units = [(j, kvh) for j in range(per_chunk) for kvh in range(A_KV_HEADS)]
    logits, values, probs = {}, {}, {}

    def swap_halves(t):
        return jnp.concatenate([t[:, A_HEAD_DIM:], t[:, :A_HEAD_DIM]], axis=1)

    def padded(t, t_sw, kvh):
        lo, hi = (t, t_sw) if kvh == 0 else (t_sw, t)
        return jnp.concatenate([jnp.where(low, lo, zero), jnp.where(low, zero, hi)], axis=0)

    def logit_matmuls():
        for j in range(per_chunk):
            rows = pl.ds(row0 + j * A_BLOCK, A_BLOCK)
            kw = kbuf[pl.ds(row0 + j * A_BLOCK, n_keys), :]
            vw = vbuf[pl.ds(row0 + j * A_BLOCK, n_keys), :]
            kw_sw, vw_sw = swap_halves(kw), swap_halves(vw)
            for kvh in range(A_KV_HEADS):
                q2 = jnp.concatenate([q_ref[0, rows, (2 * kvh + c) * LANES:(2 * kvh + c + 1) * LANES]
                                      for c in range(2)], axis=0)
                logits[j, kvh] = lax.dot_general(q2, padded(kw, kw_sw, kvh), nt,
                                                 preferred_element_type=F32)
                values[j, kvh] = padded(vw, vw_sw, kvh)

    def softmaxes():
        for j, kvh in units:
            block = first_block + j
            variant = jnp.where(block == 0, FIRST, jnp.where(block == n_blocks - 1, LAST, INTERIOR))
            s2 = logits[j, kvh]
            p_rows, r_rows = [], []
            for c in range(2):
                ps, sums, sink_gaps = [], [], []
                for half in range(2):
                    head = 2 * (2 * kvh + c) + half
                    sh = s2[c * A_BLOCK:(c + 1) * A_BLOCK, half * n_keys:(half + 1) * n_keys]
                    sh = sh + bias_ref[variant, head]
                    sink = sink_ref[head] * LOG2E
                    if fixed_shift:
                        m = ctl_ref[1 + head]
                        p = jnp.exp2(sh)
                    else:
                        m = jnp.maximum(jnp.max(sh, axis=-1, keepdims=True), sink)
                        p = jnp.exp2(sh - m)
                    ps.append(p.astype(BF16))
                    sums.append(jnp.sum(p, axis=-1, keepdims=True))
                    sink_gaps.append(sink - m)
                sink_lanes = low_q[:1] if fixed_shift else low_q
                den = (jnp.where(low_q, sums[0], sums[1])
                       + jnp.exp2(jnp.where(sink_lanes, sink_gaps[0], sink_gaps[1])))
                p_rows.append(jnp.concatenate(ps, axis=1))
                r_rows.append(1.0 / den)
            probs[j, kvh] = (jnp.concatenate(p_rows, axis=0), jnp.concatenate(r_rows, axis=0))

    def value_matmuls():
        for j, kvh in units:
            p2, r2 = probs[j, kvh]
            o2 = jnp.dot(p2, values[j, kvh], preferred_element_type=F32) * r2
            for c in range(2):
                tile = 2 * kvh + c
                o_ref[0, pl.ds(row0 + j * A_BLOCK, A_BLOCK), tile * LANES:(tile + 1) * LANES] = (
                    o2[c * A_BLOCK:(c + 1) * A_BLOCK].astype(BF16))

    return logit_matmuls, softmaxes, value_matmuls


def _attn_a_kernel(seq_len, q_ref, kp_ref, kc_ref, kn_ref, vp_ref, vc_ref, vn_ref, sink_ref, ctl_ref,
                   o_ref, kbuf, vbuf, bias_ref):
    i = pl.program_id(1)
    step_rows = q_ref.shape[1]

    @pl.when((pl.program_id(0) == 0) & (i == 0))
    def _():
        _attn_a_init_bias(bias_ref, ctl_ref)

    _attn_a_load_kv((kp_ref, kc_ref, kn_ref, vp_ref, vc_ref, vn_ref), kbuf, vbuf)

    def run(fixed_shift):
        def chunk(t, carry):
            row0 = pl.multiple_of(t * ATTN_CHUNK, ATTN_CHUNK)
            for phase in _attn_a_pass(row0, (i * step_rows + row0) // A_BLOCK, seq_len // A_BLOCK,
                                      fixed_shift, q_ref, kbuf, vbuf, bias_ref, sink_ref, ctl_ref, o_ref):
                phase()
            return carry

        lax.fori_loop(0, step_rows // ATTN_CHUNK, chunk, 0)

    _by_softmax_shift(ctl_ref, run)


def _attn_a_specs(s, step_rows, position):
    per_step = step_rows // A_BLOCK
    n_blocks = s // A_BLOCK
    assert n_blocks >= 2 and s % step_rows == 0

    def cur(*ids):
        b, i = position(*ids)
        return (b, i, 0)

    def prev(*ids):
        b, i = position(*ids)
        return (b, jnp.maximum(i * per_step - 1, 0), 0)

    def nxt(*ids):
        b, i = position(*ids)
        return (b, jnp.minimum((i + 1) * per_step, n_blocks - 1), 0)

    qo_spec = pl.BlockSpec((1, step_rows, A_Q_W), cur)
    kv_specs = [pl.BlockSpec((1, A_BLOCK, A_KV_W), prev), pl.BlockSpec((1, step_rows, A_KV_W), cur),
                pl.BlockSpec((1, A_BLOCK, A_KV_W), nxt)]
    scratch = [pltpu.VMEM((step_rows + 2 * A_BLOCK, A_KV_W), BF16)] * 2 + [
        pltpu.VMEM((3, A_HEADS, A_BLOCK, 3 * A_BLOCK), F32)]
    return qo_spec, kv_specs, scratch


def _attn_a(qa, ka, va, sink, logit_bound):
    n, s, _ = qa.shape
    qo_spec, kv_specs, scratch = _attn_a_specs(s, A_STEP, lambda b, i: (b, i))
    smem = pl.BlockSpec(memory_space=pltpu.SMEM)
    ctl = _softmax_ctl(jnp.maximum(logit_bound, sink * LOG2E))
    return pl.pallas_call(
        functools.partial(_attn_a_kernel, s),
        grid=(n, s // A_STEP),
        in_specs=[qo_spec] + kv_specs + kv_specs + [smem, smem],
        out_specs=qo_spec,
        out_shape=jax.ShapeDtypeStruct((n, s, A_Q_W), BF16),
        scratch_shapes=scratch,
        compiler_params=_params(2),
        name="attn_a",
    )(qa, ka, ka, ka, va, va, va, sink, ctl)


B_BLOCK = 2 * B_HALF_WIN
B_KEYS = B_BLOCK + 2 * B_HALF_WIN
B_ROWS_PER_STEP = 2048


def _attn_b_kernel(sub_len, slopes, q_ref, kp_ref, kc_ref, kn_ref, vp_ref, vc_ref, vn_ref, ctl_ref,
                   o_ref, lse_ref, kbuf, vbuf, bias_ref):
    i = pl.program_id(2)
    n_planes, step = q_ref.shape[1], q_ref.shape[2]
    n_blocks = sub_len // B_BLOCK

    @pl.when((pl.program_id(0) == 0) & (pl.program_id(1) == 0) & (i == 0))
    def _():
        shift = jnp.where(ctl_ref[0] > 0.5, ctl_ref[1], 0.0)
        q_row = lax.broadcasted_iota(jnp.int32, (B_BLOCK, B_KEYS), 0)
        k_col = lax.broadcasted_iota(jnp.int32, (B_BLOCK, B_KEYS), 1)
        dist = jnp.abs(q_row + B_HALF_WIN - k_col)
        dist_f = dist.astype(F32)
        for head, slope in enumerate(slopes):
            base = jnp.where(dist <= B_HALF_WIN, (-slope * LOG2E) * dist_f - shift, NEG_INF)
            first = jnp.where(k_col >= B_HALF_WIN, base, NEG_INF)
            bias_ref[0, head] = base
            bias_ref[1, head] = first
            bias_ref[2, head] = jnp.where(k_col < B_HALF_WIN + B_BLOCK, base, NEG_INF)
            bias_ref[3, head] = jnp.where(k_col < B_HALF_WIN + B_BLOCK, first, NEG_INF)

    for buf, prev, cur, nxt in ((kbuf, kp_ref, kc_ref, kn_ref), (vbuf, vp_ref, vc_ref, vn_ref)):
        for r in range(n_planes):
            buf[r, 0:B_HALF_WIN, :] = prev[0, r]
            buf[r, B_HALF_WIN:B_HALF_WIN + step, :] = cur[0, r]
            buf[r, B_HALF_WIN + step:, :] = nxt[0, r]

    lane = lax.broadcasted_iota(jnp.int32, (B_BLOCK, LANES), 1)
    lanes_per_head = LANES // B_HEADS_PER_GROUP
    heads = [slice(h * B_HEAD_DIM, (h + 1) * B_HEAD_DIM) for h in range(B_HEADS_PER_GROUP)]
    nt = (((1,), (1,)), ((), ()))
    rows_per_pass = min(step, ATTN_CHUNK)
    planes_per_pass = ATTN_CHUNK // rows_per_pass
    passes_per_plane = step // rows_per_pass
    blocks = [(dr, j) for dr in range(planes_per_pass) for j in range(rows_per_pass // B_BLOCK)]

    def one_pass(fixed_shift, t, carry):
        if planes_per_pass == 1:
            plane0 = t // passes_per_plane
            row0 = pl.multiple_of((t % passes_per_plane) * ATTN_CHUNK, ATTN_CHUNK)
        else:
            plane0, row0 = t * planes_per_pass, 0

        logits = {}
        for dr, j in blocks:
            for h, sl in enumerate(heads):
                q = q_ref[0, plane0 + dr, pl.ds(row0 + j * B_BLOCK, B_BLOCK), sl]
                k = kbuf[plane0 + dr, pl.ds(row0 + j * B_BLOCK, B_KEYS), sl]
                logits[dr, j, h] = lax.dot_general(q, k, nt, preferred_element_type=F32)

        probs = {}
        for dr, j in blocks:
            block = (i * step + row0) // B_BLOCK + j
            variant = (block == 0).astype(jnp.int32) + 2 * (block == n_blocks - 1).astype(jnp.int32)
            m_tile = den_tile = None
            for h in range(B_HEADS_PER_GROUP):
                s = logits[dr, j, h] + bias_ref[variant, h]
                if fixed_shift:
                    m = ctl_ref[1]
                    p = jnp.exp2(s)
                else:
                    m = jnp.max(s, axis=-1, keepdims=True)
                    p = jnp.exp2(s - m)
                den = jnp.sum(p, axis=-1, keepdims=True)
                probs[dr, j, h] = (p.astype(BF16), 1.0 / den)
                den_tile = den if h == 0 else jnp.where(lane >= h * lanes_per_head, den, den_tile)
                if not fixed_shift:
                    m_tile = m if h == 0 else jnp.where(lane >= h * lanes_per_head, m, m_tile)
            lse_ref[0, plane0 + dr, pl.ds(row0 + j * B_BLOCK, B_BLOCK), :] = (
                ((m if fixed_shift else m_tile) + jnp.log2(den_tile)) * LN2)

        for dr, j in blocks:
            for h, sl in enumerate(heads):
                p, rden = probs[dr, j, h]
                v = vbuf[plane0 + dr, pl.ds(row0 + j * B_BLOCK, B_KEYS), sl]
                o = jnp.dot(p, v, preferred_element_type=F32) * rden
                o_ref[0, plane0 + dr, pl.ds(row0 + j * B_BLOCK, B_BLOCK), sl] = o.astype(BF16)
        return carry

    _by_softmax_shift(ctl_ref, lambda fixed_shift: lax.fori_loop(
        0, n_planes * step // ATTN_CHUNK, functools.partial(one_pass, fixed_shift), 0))


def _attn_b_group(q, k, v, gi, logit_bound):
    n, dil, sub, _ = q.shape
    step = min(sub, B_ROWS_PER_STEP)
    n_planes = B_ROWS_PER_STEP // step
    assert sub % step == 0 and dil % n_planes == 0 and step % B_BLOCK == 0
    assert ATTN_CHUNK % step == 0 or step % ATTN_CHUNK == 0
    halo_per_step = step // B_HALF_WIN
    last_halo = sub // B_HALF_WIN - 1
    cur = lambda b, r, i: (b, r, i, 0)
    prev = lambda b, r, i: (b, r, jnp.maximum(i * halo_per_step - 1, 0), 0)
    nxt = lambda b, r, i: (b, r, jnp.minimum((i + 1) * halo_per_step, last_halo), 0)
    kv_specs = [pl.BlockSpec((1, n_planes, B_HALF_WIN, B_GROUP_W), prev),
                pl.BlockSpec((1, n_planes, step, B_GROUP_W), cur),
                pl.BlockSpec((1, n_planes, B_HALF_WIN, B_GROUP_W), nxt)]
    all_slopes = _alibi_slopes(len(B_GROUPS) * B_HEADS_PER_GROUP)
    slopes = [all_slopes[gi * B_HEADS_PER_GROUP + h] * dil for h in range(B_HEADS_PER_GROUP)]
    return pl.pallas_call(
        functools.partial(_attn_b_kernel, sub, slopes),
        grid=(n, dil // n_planes, sub // step),
        in_specs=[pl.BlockSpec((1, n_planes, step, B_GROUP_W), cur)] + kv_specs + kv_specs
                 + [pl.BlockSpec(memory_space=pltpu.SMEM)],
        out_specs=[pl.BlockSpec((1, n_planes, step, B_GROUP_W), cur),
                   pl.BlockSpec((1, n_planes, step, LANES), cur)],
        out_shape=[jax.ShapeDtypeStruct((n, dil, sub, B_GROUP_W), BF16),
                   jax.ShapeDtypeStruct((n, dil, sub, LANES), F32)],
        scratch_shapes=[pltpu.VMEM((n_planes, step + 2 * B_HALF_WIN, B_GROUP_W), BF16)] * 2
                       + [pltpu.VMEM((4, B_HEADS_PER_GROUP, B_BLOCK, B_KEYS), F32)],
        compiler_params=_params(3),
        name=f"attn_b{gi}",
    )(q, k, k, k, v, v, v, _softmax_ctl(logit_bound))


M_STEP = 2048


def _attn_m_kernel(q_ref, k_ref, v_ref, ctl_ref, o_ref):
    heads = [slice(h * M_HEAD_DIM, (h + 1) * M_HEAD_DIM) for h in range(M_HEADS)]
    nt = (((1,), (1,)), ((), ()))

    def chunk(fixed_shift, t, carry):
        rows = pl.ds(pl.multiple_of(t * ATTN_CHUNK, ATTN_CHUNK), ATTN_CHUNK)
        logits = [lax.dot_general(q_ref[0, rows, sl], k_ref[0, :, sl], nt, preferred_element_type=F32)
                  for sl in heads]
        probs = []
        for s in logits:
            m = ctl_ref[1] if fixed_shift else jnp.max(s, axis=-1, keepdims=True)
            p = jnp.exp2(s - m)
            probs.append((p.astype(BF16), 1.0 / jnp.sum(p, axis=-1, keepdims=True)))
        for sl, (p, rden) in zip(heads, probs):
            o = jnp.dot(p, v_ref[0, :, sl], preferred_element_type=F32) * rden
            o_ref[0, rows, sl] = o.astype(BF16)
        return carry

    _by_softmax_shift(ctl_ref, lambda fixed_shift: lax.fori_loop(
        0, M_STEP // ATTN_CHUNK, functools.partial(chunk, fixed_shift), 0))


def _attn_m(qm, mk, mv, logit_bound):
    n, s, _ = qm.shape
    cur = lambda b, i: (b, i, 0)
    mem = lambda b, i: (b, 0, 0)
    return pl.pallas_call(
        _attn_m_kernel,
        grid=(n, s // M_STEP),
        in_specs=[pl.BlockSpec((1, M_STEP, M_W), cur), pl.BlockSpec((1, N_MEM, M_W), mem),
                  pl.BlockSpec((1, N_MEM, M_W), mem), pl.BlockSpec(memory_space=pltpu.SMEM)],
        out_specs=pl.BlockSpec((1, M_STEP, M_W), cur),
        out_shape=jax.ShapeDtypeStruct((n, s, M_W), BF16),
        compiler_params=_params(2),
        name="attn_m",
    )(qm, mk, mv, _softmax_ctl(logit_bound))


MERGE_ROWS = 512
assert all(dil <= DEINTERLEAVE_STRIDE ** 2 for _, dil in B_GROUPS)
N_MERGE_SLABS = sum((B_GROUP_W // LANES + 1) * (1 if dil <= DEINTERLEAVE_STRIDE else 2)
                    for _, dil in B_GROUPS if dil > 1)


def _merge_kernel(x_ref, g_ref, oa_ref, o0_ref, o1_ref, o2_ref, l0_ref, l1_ref, l2_ref, om_ref,
                  wg_ref, bg_ref, wbr_ref, wout_ref, y_ref, nat):
    tm = x_ref.shape[1]
    x = x_ref[0]
    h = _rms_rows(x, g_ref[...]).astype(BF16)
    tiles = B_GROUP_W // LANES

    o_nat, lse_nat = [], []
    slab = 0
    for (_, dil), o_ref, l_ref in zip(B_GROUPS, (o0_ref, o1_ref, o2_ref), (l0_ref, l1_ref, l2_ref)):
        if dil == 1:
            o_nat.append([o_ref[0, 0, :, c * LANES:(c + 1) * LANES].astype(F32) for c in range(tiles)])
            lse_nat.append(l_ref[0, 0])
            continue
        def plane_tile(r, c):
            if c == tiles:
                return l_ref[0, r]
            return o_ref[0, r, :, c * LANES:(c + 1) * LANES].astype(F32)

        for c in range(tiles + 1):
            if dil <= DEINTERLEAVE_STRIDE:
                for r in range(dil):
                    nat[slab + c, pl.ds(r, tm // dil, stride=dil), :] = plane_tile(r, c)
                continue
            groups = DEINTERLEAVE_STRIDE
            inner, rows = dil // groups, tm // groups
            spare = slab + tiles + 1 + c
            for g in range(groups):
                for r in range(inner):
                    nat[spare, pl.ds(g * rows + r, rows // inner, stride=inner), :] = plane_tile(g + groups * r, c)
            for g in range(groups):
                nat[slab + c, pl.ds(g, rows, stride=groups), :] = nat[spare, g * rows:(g + 1) * rows, :]
        o_nat.append([nat[slab + c] for c in range(tiles)])
        lse_nat.append(nat[slab + tiles])
        slab += (tiles + 1) * (1 if dil <= DEINTERLEAVE_STRIDE else 2)

    top = jnp.maximum(jnp.maximum(lse_nat[0], lse_nat[1]), lse_nat[2])
    es = [jnp.exp(l - top) for l in lse_nat]
    r_sum = 1.0 / (es[0] + es[1] + es[2])
    ws = [e * r_sum for e in es]
    lanes_per_head = LANES // B_HEADS_PER_GROUP
    ob_heads = []
    for hd in range(B_HEADS_PER_GROUP):
        acc = None
        for w, o in zip(ws, o_nat):
            term = w[:, hd * lanes_per_head:hd * lanes_per_head + 1] * o[hd]
            acc = term if acc is None else acc + term
        ob_heads.append(acc.astype(BF16))
    ob = jnp.concatenate(ob_heads, axis=1)

    def branch(b, o):
        return jnp.dot(o, wbr_ref[b], preferred_element_type=F32)

    def gate(b):
        logits = jnp.dot(h, wg_ref[:, b * D_MODEL:(b + 1) * D_MODEL], preferred_element_type=F32)
        return jax.nn.sigmoid(logits + bg_ref[b:b + 1, :])

    y_a, y_m = branch(0, oa_ref[0]), branch(2, om_ref[0])
    gates = [gate(b) for b in range(N_BRANCH)]
    z = gates[0] * y_a + gates[1] * branch(1, ob) + gates[2] * y_m
    y_ref[0] = x + jnp.dot(z.astype(BF16), wout_ref[...], preferred_element_type=F32)


def _merge(x, g_mix, oa, obs, lses, om, w_gate, b_gate, w_branch, w_out):
    n, s, _ = x.shape
    tm = MERGE_ROWS
    cur = lambda b, i: (b, i, 0)
    plane = lambda b, i: (b, 0, i, 0)
    rows = lambda w: pl.BlockSpec((1, tm, w), cur)
    planes = lambda w: [pl.BlockSpec((1, dil, tm // dil, w), plane) for _, dil in B_GROUPS]
    return pl.pallas_call(
        _merge_kernel,
        grid=(n, s // tm),
        in_specs=[rows(D_MODEL), _resident((1, D_MODEL)), rows(A_Q_W)]
                 + planes(B_GROUP_W) + planes(LANES) + [rows(M_W)]
                 + [_resident(w_gate.shape), _resident(b_gate.shape), _resident(w_branch.shape),
                    _resident(w_out.shape)],
        out_specs=rows(D_MODEL),
        out_shape=jax.ShapeDtypeStruct((n, s, D_MODEL), F32),
        scratch_shapes=[pltpu.VMEM((N_MERGE_SLABS, tm, LANES), F32)],
        compiler_params=_params(2),
        name="merge",
    )(x, g_mix, oa, *obs, *lses, om, w_gate, b_gate, w_branch, w_out)


MLP_ROWS = 1024
MLP_FF_CHUNK = 1024


def _mlp_kernel(x_ref, g_ref, wup_ref, wdn_ref, y_ref):
    x = x_ref[...]
    h = _rms_rows(x, g_ref[...]).astype(BF16)
    acc = x
    for lo in range(0, D_FF, MLP_FF_CHUNK):
        u = jnp.dot(h, wup_ref[:, lo:lo + MLP_FF_CHUNK], preferred_element_type=F32)
        a = jnp.square(jnp.maximum(u, 0.0)).astype(BF16)
        acc = acc + jnp.dot(a, wdn_ref[lo:lo + MLP_FF_CHUNK, :], preferred_element_type=F32)
    y_ref[...] = acc


def _mlp(x2d, g_mlp, w_up, w_down):
    t = x2d.shape[0]
    tm = MLP_ROWS
    row = lambda i: (i, 0)
    return pl.pallas_call(
        _mlp_kernel,
        grid=(t // tm,),
        in_specs=[pl.BlockSpec((tm, D_MODEL), row), _resident((1, D_MODEL)),
                  _resident(w_up.shape), _resident(w_down.shape)],
        out_specs=pl.BlockSpec((tm, D_MODEL), row),
        out_shape=jax.ShapeDtypeStruct((t, D_MODEL), F32),
        compiler_params=_params(1),
        name="mlp",
    )(x2d, g_mlp, w_up, w_down)


def _tile2(g):
    return jnp.concatenate([g, g]).reshape(1, LANES)


def _row(g):
    return g.reshape(1, -1)


def _layer(x, mem, g_mix, g_mem, w_in, b_gate, w_mem_kv, gq_a, gk_a, sink_a, gq_b, gk_b, gq_m, gk_m,
           w_branch, w_out, g_mlp, w_up, w_down):
    n, s, _ = x.shape
    bounds = np.cumsum((0, A_Q_W, A_KV_W, A_KV_W, B_W, B_W, B_W, M_W, GATE_W))
    seg = lambda a, b: w_in[:, int(bounds[a]):int(bounds[b])].astype(BF16)
    ws = (seg(0, 1), seg(1, 3), seg(3, 4), seg(4, 5), seg(5, 6), seg(6, 7))
    w_gate = seg(7, 8)
    gq_a, gq_b, gq_m = (gq_a * (A_HEAD_DIM ** -0.5 * LOG2E), gq_b * (B_HEAD_DIM ** -0.5 * LOG2E),
                        gq_m * (M_HEAD_DIM ** -0.5 * LOG2E))
    gains = (_tile2(gq_a), _tile2(gk_a), _row(gq_b), _row(gk_b), _row(gq_m))

    qa, ka, va, qm, *qkv_b = _proj(x, _row(g_mix), ws, gains)
    mk, mv = _mem_kv(mem.reshape(n * N_MEM, D_MODEL), _row(g_mem), w_mem_kv.astype(BF16), _row(gk_m))

    oa = _attn_a(qa, ka, va, sink_a, _logit_bound(gq_a, gk_a, A_HEAD_DIM))
    bound_b = _logit_bound(gq_b, gk_b, B_HEAD_DIM)
    b_out = [_attn_b_group(*qkv_b[3 * gi:3 * gi + 3], gi, bound_b) for gi in range(len(B_GROUPS))]
    om = _attn_m(qm, mk.reshape(n, N_MEM, M_W), mv.reshape(n, N_MEM, M_W),
                 _logit_bound(gq_m, gk_m, M_HEAD_DIM))

    x1 = _merge(x, _row(g_mix), oa, [o for o, _ in b_out], [l for _, l in b_out], om,
                w_gate, b_gate, w_branch.astype(BF16), w_out.astype(BF16))
    y = _mlp(x1.reshape(n * s, D_MODEL), _row(g_mlp), w_up.astype(BF16), w_down.astype(BF16))
    return y.reshape(n, s, D_MODEL)


def kernel(x_prompt, x_sample, mem_prompt, mem_sample, g_mix, g_mem, w_in, b_gate, w_mem_kv, gq_a, gk_a,
           sink_a, gq_b, gk_b, gq_m, gk_m, w_branch, w_out, g_mlp, w_up, w_down):
    depth = w_in.shape[0]

    def run(x, mem):
        for l in range(depth):
            x = _layer(x, mem, g_mix[l], g_mem[l], w_in[l], b_gate[l], w_mem_kv[l], gq_a[l], gk_a[l],
                       sink_a[l], gq_b[l], gk_b[l], gq_m[l], gk_m[l], w_branch[l], w_out[l], g_mlp[l],
                       w_up[l], w_down[l])
        return x

    return (run(x_prompt, mem_prompt), run(x_sample, mem_sample))
```

```python
import functools

import numpy as np
import jax
import jax.numpy as jnp
from jax import lax
from jax.experimental import pallas as pl
from jax.experimental.pallas import tpu as pltpu

D_MODEL = 1024
N_MEM = 256
A_HEADS = 8
A_KV_HEADS = 2
A_HEAD_DIM = 64
A_HALF_WIN = 128
B_GROUPS = ((128, 1), (512, 4), (2048, 16))
B_HEADS_PER_GROUP = 4
B_HEAD_DIM = 128
M_HEADS = 4
M_HEAD_DIM = 128
N_BRANCH = 3
BRANCH_WIDTH = D_MODEL // 2
D_FF = 4 * D_MODEL
EPS = 1e-6
NEG_INF = -1e30

A_Q_W = A_HEADS * A_HEAD_DIM
A_KV_W = A_KV_HEADS * A_HEAD_DIM
B_GROUP_W = B_HEADS_PER_GROUP * B_HEAD_DIM
B_W = len(B_GROUPS) * B_GROUP_W
M_W = M_HEADS * M_HEAD_DIM
GATE_W = N_BRANCH * D_MODEL

LANES = 128
B_HALF_WIN = 64
VMEM_LIMIT_BYTES = 56 * 1024 * 1024

BF16 = jnp.bfloat16
F32 = jnp.float32


def _alibi_slopes(n):
    return [float(2.0 ** (-8.0 * (i + 1) / n)) for i in range(n)]


def _params(n_grid_axes):
    return pltpu.CompilerParams(
        dimension_semantics=("arbitrary",) * n_grid_axes,
        vmem_limit_bytes=VMEM_LIMIT_BYTES)


def _resident(shape):
    zeros = (0,) * len(shape)
    return pl.BlockSpec(shape, lambda *_: zeros, pipeline_mode=pl.Buffered(1))


def _rms_rows(x, gain):
    ms = jnp.mean(x * x, axis=-1, keepdims=True)
    return x * lax.rsqrt(ms + EPS) * gain


def _head_norm128(blk, gain):
    ms = jnp.sum(blk * blk, axis=-1, keepdims=True) * (1.0 / LANES)
    return blk * lax.rsqrt(ms + EPS) * gain


def _head_norm64(blk, gain2):
    low = lax.broadcasted_iota(jnp.int32, blk.shape, 1) < A_HEAD_DIM
    sq = blk * blk
    ss_lo = jnp.sum(jnp.where(low, sq, 0.0), axis=-1, keepdims=True)
    ss_hi = jnp.sum(jnp.where(low, 0.0, sq), axis=-1, keepdims=True)
    ms = jnp.where(low, ss_lo, ss_hi) * (1.0 / A_HEAD_DIM)
    return blk * lax.rsqrt(ms + EPS) * gain2


PROJ_ROWS = 512
DEINTERLEAVE_STRIDE = 4
N_DILATED_SLABS = sum(3 * (B_GROUP_W // LANES) for _, dil in B_GROUPS if dil > 1)


def _proj_kernel(x_ref, g_ref, wqa_ref, wkva_ref, wqb_ref, wkb_ref, wvb_ref, wqm_ref,
                 gqa_ref, gka_ref, gqb_ref, gkb_ref, gqm_ref,
                 qa_ref, ka_ref, va_ref, qm_ref, *rest):
    b_refs, ybuf = rest[:-1], rest[-1]
    tm = x_ref.shape[1]
    h = _rms_rows(x_ref[0], g_ref[...]).astype(BF16)

    def mm(w_ref, lo, hi):
        return lambda: jnp.dot(h, w_ref[:, lo:hi], preferred_element_type=F32)

    def finish_qa(y):
        for c in range(A_Q_W // LANES):
            sl = slice(c * LANES, (c + 1) * LANES)
            qa_ref[0, :, sl] = _head_norm64(y[:, sl], gqa_ref[...]).astype(BF16)

    def finish_kva(y):
        ka_ref[0] = _head_norm64(y[:, :A_KV_W], gka_ref[...]).astype(BF16)
        va_ref[0] = y[:, A_KV_W:].astype(BF16)

    def finish_qm(y):
        for c in range(M_W // LANES):
            sl = slice(c * LANES, (c + 1) * LANES)
            qm_ref[0, :, sl] = _head_norm128(y[:, sl], gqm_ref[...]).astype(BF16)

    def finish_b(o_ref, gain_ref, dil, slab):
        def finish(y):
            for c in range(B_GROUP_W // LANES):
                sl = slice(c * LANES, (c + 1) * LANES)
                blk = y[:, sl]
                if gain_ref is not None:
                    blk = _head_norm128(blk, gain_ref[...])
                if dil == 1:
                    o_ref[0, 0, :, sl] = blk.astype(BF16)
                else:
                    ybuf[slab + c] = blk
                    groups, stride = 1, dil
                    while stride > DEINTERLEAVE_STRIDE:
                        rows = tm // groups
                        parts = [ybuf[slab + c, pl.ds(g * rows + r, rows // DEINTERLEAVE_STRIDE,
                                                      stride=DEINTERLEAVE_STRIDE), :]
                                 for g in range(groups) for r in range(DEINTERLEAVE_STRIDE)]
                        ybuf[slab + c] = jnp.concatenate(parts, axis=0)
                        groups, stride = groups * DEINTERLEAVE_STRIDE, stride // DEINTERLEAVE_STRIDE
                    rows = tm // groups
                    for g in range(groups):
                        for r in range(stride):
                            o_ref[0, g + groups * r, :, sl] = (
                                ybuf[slab + c, pl.ds(g * rows + r, rows // stride, stride=stride), :].astype(BF16))
        return finish

    def branch_b(gi, slab):
        dil = B_GROUPS[gi][1]
        lo = gi * B_GROUP_W
        parts = []
        for part, (w_ref, gain_ref) in enumerate(((wqb_ref, gqb_ref), (wkb_ref, gkb_ref), (wvb_ref, None))):
            parts.append((mm(w_ref, lo, lo + B_GROUP_W), finish_b(b_refs[3 * gi + part], gain_ref, dil, slab)))
            slab += B_GROUP_W // LANES
        return parts, slab

    by_dilation = sorted(range(len(B_GROUPS)), key=lambda gi: -B_GROUPS[gi][1])
    stages, slab = [], 0
    for gi in by_dilation[:-1]:
        parts, slab = branch_b(gi, slab)
        stages += parts
    stages += [(mm(wqa_ref, 0, A_Q_W), finish_qa), (mm(wkva_ref, 0, 2 * A_KV_W), finish_kva),
               (mm(wqm_ref, 0, M_W), finish_qm)]
    stages += branch_b(by_dilation[-1], slab)[0]

    pending = None
    for matmul, finish in stages:
        y = matmul()
        if pending is not None:
            pending[0](pending[1])
        pending = (finish, y)
    pending[0](pending[1])


def _proj(x, g_mix, ws, gains):
    n, s, _ = x.shape
    tm = PROJ_ROWS
    cur = lambda b, i: (b, i, 0)
    plane = lambda b, i: (b, 0, i, 0)
    in_specs = ([pl.BlockSpec((1, tm, D_MODEL), cur), _resident((1, D_MODEL))]
                + [_resident(w.shape) for w in ws]
                + [_resident((1, LANES)) for _ in gains])
    widths = (A_Q_W, A_KV_W, A_KV_W, M_W)
    out_specs = [pl.BlockSpec((1, tm, w), cur) for w in widths]
    out_shape = [jax.ShapeDtypeStruct((n, s, w), BF16) for w in widths]
    for _, dil in B_GROUPS:
        out_specs += [pl.BlockSpec((1, dil, tm // dil, B_GROUP_W), plane)] * 3
        out_shape += [jax.ShapeDtypeStruct((n, dil, s // dil, B_GROUP_W), BF16)] * 3
    return pl.pallas_call(
        _proj_kernel,
        grid=(n, s // tm),
        in_specs=in_specs,
        out_specs=out_specs,
        out_shape=out_shape,
        scratch_shapes=[pltpu.VMEM((N_DILATED_SLABS, tm, LANES), F32)],
        compiler_params=_params(2),
        name="proj",
    )(x, g_mix, *ws, *gains)


MEM_ROWS = 1024


def _mem_kv_kernel(m_ref, g_ref, w_ref, gk_ref, k_ref, v_ref):
    h = _rms_rows(m_ref[...], g_ref[...]).astype(BF16)
    y = jnp.dot(h, w_ref[...], preferred_element_type=F32)
    for c in range(M_HEADS):
        sl = slice(c * LANES, (c + 1) * LANES)
        k_ref[:, sl] = _head_norm128(y[:, sl], gk_ref[...]).astype(BF16)
    v_ref[...] = y[:, M_W:].astype(BF16)


def _mem_kv(mem2d, g_mem, w_mem_kv, gk_m):
    t = mem2d.shape[0]
    tm = min(t, MEM_ROWS)
    row = lambda i: (i, 0)
    return pl.pallas_call(
        _mem_kv_kernel,
        grid=(t // tm,),
        in_specs=[pl.BlockSpec((tm, D_MODEL), row), _resident((1, D_MODEL)),
                  _resident(w_mem_kv.shape), _resident((1, LANES))],
        out_specs=[pl.BlockSpec((tm, M_W), row)] * 2,
        out_shape=[jax.ShapeDtypeStruct((t, M_W), BF16)] * 2,
        compiler_params=_params(1),
        name="mem_kv",
    )(mem2d, g_mem, w_mem_kv, gk_m)


A_BLOCK = A_HALF_WIN
A_STEP = 2048
ATTN_CHUNK = 512
INTERIOR, FIRST, LAST = 0, 1, 2
LOG2E = float(np.log2(np.e))
LN2 = float(np.log(2.0))

SOFTMAX_SHIFT_LIMIT = 40.0
BF16_SLACK = 1.0 + 2.0 ** -6


def _logit_bound(gain_q, gain_k, head_dim):
    return head_dim * jnp.max(jnp.abs(gain_q)) * jnp.max(jnp.abs(gain_k)) * BF16_SLACK


def _softmax_ctl(shifts):
    shifts = jnp.asarray(shifts, F32).reshape(-1)
    fixed = jnp.all(shifts <= SOFTMAX_SHIFT_LIMIT)
    return jnp.concatenate([fixed.astype(F32).reshape(1), shifts])


def _by_softmax_shift(ctl_ref, run):
    fixed = ctl_ref[0] > 0.5

    @pl.when(fixed)
    def _():
        run(True)

    @pl.when(jnp.logical_not(fixed))
    def _():
        run(False)


def _attn_a_init_bias(bias_ref, ctl_ref):
    n_keys = 3 * A_BLOCK
    q_row = lax.broadcasted_iota(jnp.int32, (A_BLOCK, n_keys), 0)
    k_col = lax.broadcasted_iota(jnp.int32, (A_BLOCK, n_keys), 1)
    dist = jnp.abs(q_row + A_BLOCK - k_col)
    dist_f = dist.astype(F32)
    for head, slope in enumerate(_alibi_slopes(A_HEADS)):
        shift = jnp.where(ctl_ref[0] > 0.5, ctl_ref[1 + head], 0.0)
        base = jnp.where(dist <= A_HALF_WIN, (-slope * LOG2E) * dist_f - shift, NEG_INF)
        bias_ref[INTERIOR, head] = base
        bias_ref[FIRST, head] = jnp.where(k_col >= A_BLOCK, base, NEG_INF)
        bias_ref[LAST, head] = jnp.where(k_col < 2 * A_BLOCK, base, NEG_INF)


def _attn_a_load_kv(kv_refs, kbuf, vbuf):
    kp_ref, kc_ref, kn_ref, vp_ref, vc_ref, vn_ref = kv_refs
    rows = kc_ref.shape[1]
    for buf, prev, cur, nxt in ((kbuf, kp_ref, kc_ref, kn_ref), (vbuf, vp_ref, vc_ref, vn_ref)):
        buf[0:A_BLOCK, :] = prev[0]
        buf[A_BLOCK:A_BLOCK + rows, :] = cur[0]
        buf[A_BLOCK + rows:, :] = nxt[0]


def _attn_a_pass(row0, first_block, n_blocks, fixed_shift, q_ref, kbuf, vbuf, bias_ref, sink_ref, ctl_ref,
                 o_ref):
    n_keys = 3 * A_BLOCK
    low = lax.broadcasted_iota(jnp.int32, (n_keys, LANES), 1) < A_HEAD_DIM
    low_q = lax.broadcasted_iota(jnp.int32, (A_BLOCK, LANES), 1) < A_HEAD_DIM
    zero = jnp.zeros((n_keys, LANES), BF16)
    nt = (((1,), (1,)), ((), ()))
    per_chunk = ATTN_CHUNK // A_BLOCK
    units = [(j, kvh) for j in range(per_chunk) for kvh in range(A_KV_HEADS)]
    logits, values, probs = {}, {}, {}

    def swap_halves(t):
        return jnp.concatenate([t[:, A_HEAD_DIM:], t[:, :A_HEAD_DIM]], axis=1)

    def padded(t, t_sw, kvh):
        lo, hi = (t, t_sw) if kvh == 0 else (t_sw, t)
        return jnp.concatenate([jnp.where(low, lo, zero), jnp.where(low, zero, hi)], axis=0)

    def logit_matmuls():
        for j in range(per_chunk):
            rows = pl.ds(row0 + j * A_BLOCK, A_BLOCK)
            kw = kbuf[pl.ds(row0 + j * A_BLOCK, n_keys), :]
            vw = vbuf[pl.ds(row0 + j * A_BLOCK, n_keys), :]
            kw_sw, vw_sw = swap_halves(kw), swap_halves(vw)
            for kvh in range(A_KV_HEADS):
                q2 = jnp.concatenate([q_ref[0, rows, (2 * kvh + c) * LANES:(2 * kvh + c + 1) * LANES]
                                      for c in range(2)], axis=0)
                logits[j, kvh] = lax.dot_general(q2, padded(kw, kw_sw, kvh), nt,
                                                 preferred_element_type=F32)
                values[j, kvh] = padded(vw, vw_sw, kvh)

    def softmaxes():
        for j, kvh in units:
            block = first_block + j
            variant = jnp.where(block == 0, FIRST, jnp.where(block == n_blocks - 1, LAST, INTERIOR))
            s2 = logits[j, kvh]
            p_rows, r_rows = [], []
            for c in range(2):
                ps, sums, sink_gaps = [], [], []
                for half in range(2):
                    head = 2 * (2 * kvh + c) + half
                    sh = s2[c * A_BLOCK:(c + 1) * A_BLOCK, half * n_keys:(half + 1) * n_keys]
                    sh = sh + bias_ref[variant, head]
                    sink = sink_ref[head] * LOG2E
                    if fixed_shift:
                        m = ctl_ref[1 + head]
                        p = jnp.exp2(sh)
                    else:
                        m = jnp.maximum(jnp.max(sh, axis=-1, keepdims=True), sink)
                        p = jnp.exp2(sh - m)
                    ps.append(p.astype(BF16))
                    sums.append(jnp.sum(p, axis=-1, keepdims=True))
                    sink_gaps.append(sink - m)
                sink_lanes = low_q[:1] if fixed_shift else low_q
                den = (jnp.where(low_q, sums[0], sums[1])
                       + jnp.exp2(jnp.where(sink_lanes, sink_gaps[0], sink_gaps[1])))
                p_rows.append(jnp.concatenate(ps, axis=1))
                r_rows.append(1.0 / den)
            probs[j, kvh] = (jnp.concatenate(p_rows, axis=0), jnp.concatenate(r_rows, axis=0))

    def value_matmuls():
        for j, kvh in units:
            p2, r2 = probs[j, kvh]
            o2 = jnp.dot(p2, values[j, kvh], preferred_element_type=F32) * r2
            for c in range(2):
                tile = 2 * kvh + c
                o_ref[0, pl.ds(row0 + j * A_BLOCK, A_BLOCK), tile * LANES:(tile + 1) * LANES] = (
                    o2[c * A_BLOCK:(c + 1) * A_BLOCK].astype(BF16))

    return logit_matmuls, softmaxes, value_matmuls


def _attn_a_kernel(seq_len, q_ref, kp_ref, kc_ref, kn_ref, vp_ref, vc_ref, vn_ref, sink_ref, ctl_ref,
                   o_ref, kbuf, vbuf, bias_ref):
    i = pl.program_id(1)
    step_rows = q_ref.shape[1]

    @pl.when((pl.program_id(0) == 0) & (i == 0))
    def _():
        _attn_a_init_bias(bias_ref, ctl_ref)

    _attn_a_load_kv((kp_ref, kc_ref, kn_ref, vp_ref, vc_ref, vn_ref), kbuf, vbuf)

    def run(fixed_shift):
        def chunk(t, carry):
            row0 = pl.multiple_of(t * ATTN_CHUNK, ATTN_CHUNK)
            for phase in _attn_a_pass(row0, (i * step_rows + row0) // A_BLOCK, seq_len // A_BLOCK,
                                      fixed_shift, q_ref, kbuf, vbuf, bias_ref, sink_ref, ctl_ref, o_ref):
                phase()
            return carry

        lax.fori_loop(0, step_rows // ATTN_CHUNK, chunk, 0)

    _by_softmax_shift(ctl_ref, run)


def _attn_a_specs(s, step_rows, position):
    per_step = step_rows // A_BLOCK
    n_blocks = s // A_BLOCK
    assert n_blocks >= 2 and s % step_rows == 0

    def cur(*ids):
        b, i = position(*ids)
        return (b, i, 0)

    def prev(*ids):
        b, i = position(*ids)
        return (b, jnp.maximum(i * per_step - 1, 0), 0)

    def nxt(*ids):
        b, i = position(*ids)
        return (b, jnp.minimum((i + 1) * per_step, n_blocks - 1), 0)

    qo_spec = pl.BlockSpec((1, step_rows, A_Q_W), cur)
    kv_specs = [pl.BlockSpec((1, A_BLOCK, A_KV_W), prev), pl.BlockSpec((1, step_rows, A_KV_W), cur),
                pl.BlockSpec((1, A_BLOCK, A_KV_W), nxt)]
    scratch = [pltpu.VMEM((step_rows + 2 * A_BLOCK, A_KV_W), BF16)] * 2 + [
        pltpu.VMEM((3, A_HEADS, A_BLOCK, 3 * A_BLOCK), F32)]
    return qo_spec, kv_specs, scratch


def _attn_a(qa, ka, va, sink, logit_bound):
    n, s, _ = qa.shape
    qo_spec, kv_specs, scratch = _attn_a_specs(s, A_STEP, lambda b, i: (b, i))
    smem = pl.BlockSpec(memory_space=pltpu.SMEM)
    ctl = _softmax_ctl(jnp.maximum(logit_bound, sink * LOG2E))
    return pl.pallas_call(
        functools.partial(_attn_a_kernel, s),
        grid=(n, s // A_STEP),
        in_specs=[qo_spec] + kv_specs + kv_specs + [smem, smem],
        out_specs=qo_spec,
        out_shape=jax.ShapeDtypeStruct((n, s, A_Q_W), BF16),
        scratch_shapes=scratch,
        compiler_params=_params(2),
        name="attn_a",
    )(qa, ka, ka, ka, va, va, va, sink, ctl)


B_BLOCK = 2 * B_HALF_WIN
B_KEYS = B_BLOCK + 2 * B_HALF_WIN
B_ROWS_PER_STEP = 2048


def _attn_b_kernel(sub_len, slopes, q_ref, kp_ref, kc_ref, kn_ref, vp_ref, vc_ref, vn_ref, ctl_ref,
                   o_ref, lse_ref, kbuf, vbuf, bias_ref):
    i = pl.program_id(2)
    n_planes, step = q_ref.shape[1], q_ref.shape[2]
    n_blocks = sub_len // B_BLOCK

    @pl.when((pl.program_id(0) == 0) & (pl.program_id(1) == 0) & (i == 0))
    def _():
        shift = jnp.where(ctl_ref[0] > 0.5, ctl_ref[1], 0.0)
        q_row = lax.broadcasted_iota(jnp.int32, (B_BLOCK, B_KEYS), 0)
        k_col = lax.broadcasted_iota(jnp.int32, (B_BLOCK, B_KEYS), 1)
        dist = jnp.abs(q_row + B_HALF_WIN - k_col)
        dist_f = dist.astype(F32)
        for head, slope in enumerate(slopes):
            base = jnp.where(dist <= B_HALF_WIN, (-slope * LOG2E) * dist_f - shift, NEG_INF)
            first = jnp.where(k_col >= B_HALF_WIN, base, NEG_INF)
            bias_ref[0, head] = base
            bias_ref[1, head] = first
            bias_ref[2, head] = jnp.where(k_col < B_HALF_WIN + B_BLOCK, base, NEG_INF)
            bias_ref[3, head] = jnp.where(k_col < B_HALF_WIN + B_BLOCK, first, NEG_INF)

    for buf, prev, cur, nxt in ((kbuf, kp_ref, kc_ref, kn_ref), (vbuf, vp_ref, vc_ref, vn_ref)):
        for r in range(n_planes):
            buf[r, 0:B_HALF_WIN, :] = prev[0, r]
            buf[r, B_HALF_WIN:B_HALF_WIN + step, :] = cur[0, r]
            buf[r, B_HALF_WIN + step:, :] = nxt[0, r]

    lane = lax.broadcasted_iota(jnp.int32, (B_BLOCK, LANES), 1)
    lanes_per_head = LANES // B_HEADS_PER_GROUP
    heads = [slice(h * B_HEAD_DIM, (h + 1) * B_HEAD_DIM) for h in range(B_HEADS_PER_GROUP)]
    nt = (((1,), (1,)), ((), ()))
    rows_per_pass = min(step, ATTN_CHUNK)
    planes_per_pass = ATTN_CHUNK // rows_per_pass
    passes_per_plane = step // rows_per_pass
    blocks = [(dr, j) for dr in range(planes_per_pass) for j in range(rows_per_pass // B_BLOCK)]

    def one_pass(fixed_shift, t, carry):
        if planes_per_pass == 1:
            plane0 = t // passes_per_plane
            row0 = pl.multiple_of((t % passes_per_plane) * ATTN_CHUNK, ATTN_CHUNK)
        else:
            plane0, row0 = t * planes_per_pass, 0

        logits = {}
        for dr, j in blocks:
            for h, sl in enumerate(heads):
                q = q_ref[0, plane0 + dr, pl.ds(row0 + j * B_BLOCK, B_BLOCK), sl]
                k = kbuf[plane0 + dr, pl.ds(row0 + j * B_BLOCK, B_KEYS), sl]
                logits[dr, j, h] = lax.dot_general(q, k, nt, preferred_element_type=F32)

        probs = {}
        for dr, j in blocks:
            block = (i * step + row0) // B_BLOCK + j
            variant = (block == 0).astype(jnp.int32) + 2 * (block == n_blocks - 1).astype(jnp.int32)
            m_tile = den_tile = None
            for h in range(B_HEADS_PER_GROUP):
                s = logits[dr, j, h] + bias_ref[variant, h]
                if fixed_shift:
                    m = ctl_ref[1]
                    p = jnp.exp2(s)
                else:
                    m = jnp.max(s, axis=-1, keepdims=True)
                    p = jnp.exp2(s - m)
                den = jnp.sum(p, axis=-1, keepdims=True)
                probs[dr, j, h] = (p.astype(BF16), 1.0 / den)
                den_tile = den if h == 0 else jnp.where(lane >= h * lanes_per_head, den, den_tile)
                if not fixed_shift:
                    m_tile = m if h == 0 else jnp.where(lane >= h * lanes_per_head, m, m_tile)
            lse_ref[0, plane0 + dr, pl.ds(row0 + j * B_BLOCK, B_BLOCK), :] = (
                ((m if fixed_shift else m_tile) + jnp.log2(den_tile)) * LN2)

        for dr, j in blocks:
            for h, sl in enumerate(heads):
                p, rden = probs[dr, j, h]
                v = vbuf[plane0 + dr, pl.ds(row0 + j * B_BLOCK, B_KEYS), sl]
                o = jnp.dot(p, v, preferred_element_type=F32) * rden
                o_ref[0, plane0 + dr, pl.ds(row0 + j * B_BLOCK, B_BLOCK), sl] = o.astype(BF16)
        return carry

    _by_softmax_shift(ctl_ref, lambda fixed_shift: lax.fori_loop(
        0, n_planes * step // ATTN_CHUNK, functools.partial(one_pass, fixed_shift), 0))


def _attn_b_group(q, k, v, gi, logit_bound):
    n, dil, sub, _ = q.shape
    step = min(sub, B_ROWS_PER_STEP)
    n_planes = B_ROWS_PER_STEP // step
    assert sub % step == 0 and dil % n_planes == 0 and step % B_BLOCK == 0
    assert ATTN_CHUNK % step == 0 or step % ATTN_CHUNK == 0
    halo_per_step = step // B_HALF_WIN
    last_halo = sub // B_HALF_WIN - 1
    cur = lambda b, r, i: (b, r, i, 0)
    prev = lambda b, r, i: (b, r, jnp.maximum(i * halo_per_step - 1, 0), 0)
    nxt = lambda b, r, i: (b, r, jnp.minimum((i + 1) * halo_per_step, last_halo), 0)
    kv_specs = [pl.BlockSpec((1, n_planes, B_HALF_WIN, B_GROUP_W), prev),
                pl.BlockSpec((1, n_planes, step, B_GROUP_W), cur),
                pl.BlockSpec((1, n_planes, B_HALF_WIN, B_GROUP_W), nxt)]
    all_slopes = _alibi_slopes(len(B_GROUPS) * B_HEADS_PER_GROUP)
    slopes = [all_slopes[gi * B_HEADS_PER_GROUP + h] * dil for h in range(B_HEADS_PER_GROUP)]
    return pl.pallas_call(
        functools.partial(_attn_b_kernel, sub, slopes),
        grid=(n, dil // n_planes, sub // step),
        in_specs=[pl.BlockSpec((1, n_planes, step, B_GROUP_W), cur)] + kv_specs + kv_specs
                 + [pl.BlockSpec(memory_space=pltpu.SMEM)],
        out_specs=[pl.BlockSpec((1, n_planes, step, B_GROUP_W), cur),
                   pl.BlockSpec((1, n_planes, step, LANES), cur)],
        out_shape=[jax.ShapeDtypeStruct((n, dil, sub, B_GROUP_W), BF16),
                   jax.ShapeDtypeStruct((n, dil, sub, LANES), F32)],
        scratch_shapes=[pltpu.VMEM((n_planes, step + 2 * B_HALF_WIN, B_GROUP_W), BF16)] * 2
                       + [pltpu.VMEM((4, B_HEADS_PER_GROUP, B_BLOCK, B_KEYS), F32)],
        compiler_params=_params(3),
        name=f"attn_b{gi}",
    )(q, k, k, k, v, v, v, _softmax_ctl(logit_bound))


M_STEP = 2048


def _attn_m_kernel(q_ref, k_ref, v_ref, ctl_ref, o_ref):
    heads = [slice(h * M_HEAD_DIM, (h + 1) * M_HEAD_DIM) for h in range(M_HEADS)]
    nt = (((1,), (1,)), ((), ()))

    def chunk(fixed_shift, t, carry):
        rows = pl.ds(pl.multiple_of(t * ATTN_CHUNK, ATTN_CHUNK), ATTN_CHUNK)
        logits = [lax.dot_general(q_ref[0, rows, sl], k_ref[0, :, sl], nt, preferred_element_type=F32)
                  for sl in heads]
        probs = []
        for s in logits:
            m = ctl_ref[1] if fixed_shift else jnp.max(s, axis=-1, keepdims=True)
            p = jnp.exp2(s - m)
            probs.append((p.astype(BF16), 1.0 / jnp.sum(p, axis=-1, keepdims=True)))
        for sl, (p, rden) in zip(heads, probs):
            o = jnp.dot(p, v_ref[0, :, sl], preferred_element_type=F32) * rden
            o_ref[0, rows, sl] = o.astype(BF16)
        return carry

    _by_softmax_shift(ctl_ref, lambda fixed_shift: lax.fori_loop(
        0, M_STEP // ATTN_CHUNK, functools.partial(chunk, fixed_shift), 0))


def _attn_m(qm, mk, mv, logit_bound):
    n, s, _ = qm.shape
    cur = lambda b, i: (b, i, 0)
    mem = lambda b, i: (b, 0, 0)
    return pl.pallas_call(
        _attn_m_kernel,
        grid=(n, s // M_STEP),
        in_specs=[pl.BlockSpec((1, M_STEP, M_W), cur), pl.BlockSpec((1, N_MEM, M_W), mem),
                  pl.BlockSpec((1, N_MEM, M_W), mem), pl.BlockSpec(memory_space=pltpu.SMEM)],
        out_specs=pl.BlockSpec((1, M_STEP, M_W), cur),
        out_shape=jax.ShapeDtypeStruct((n, s, M_W), BF16),
        compiler_params=_params(2),
        name="attn_m",
    )(qm, mk, mv, _softmax_ctl(logit_bound))


MERGE_ROWS = 512
assert all(dil <= DEINTERLEAVE_STRIDE ** 2 for _, dil in B_GROUPS)
N_MERGE_SLABS = sum((B_GROUP_W // LANES + 1) * (1 if dil <= DEINTERLEAVE_STRIDE else 2)
                    for _, dil in B_GROUPS if dil > 1)


def _merge_kernel(x_ref, g_ref, oa_ref, o0_ref, o1_ref, o2_ref, l0_ref, l1_ref, l2_ref, om_ref,
                  wg_ref, bg_ref, wbr_ref, wout_ref, y_ref, nat):
    tm = x_ref.shape[1]
    x = x_ref[0]
    h = _rms_rows(x, g_ref[...]).astype(BF16)
    tiles = B_GROUP_W // LANES

    o_nat, lse_nat = [], []
    slab = 0
    for (_, dil), o_ref, l_ref in zip(B_GROUPS, (o0_ref, o1_ref, o2_ref), (l0_ref, l1_ref, l2_ref)):
        if dil == 1:
            o_nat.append([o_ref[0, 0, :, c * LANES:(c + 1) * LANES].astype(F32) for c in range(tiles)])
            lse_nat.append(l_ref[0, 0])
            continue
        def plane_tile(r, c):
            if c == tiles:
                return l_ref[0, r]
            return o_ref[0, r, :, c * LANES:(c + 1) * LANES].astype(F32)

        for c in range(tiles + 1):
            if dil <= DEINTERLEAVE_STRIDE:
                for r in range(dil):
                    nat[slab + c, pl.ds(r, tm // dil, stride=dil), :] = plane_tile(r, c)
                continue
            groups = DEINTERLEAVE_STRIDE
            inner, rows = dil // groups, tm // groups
            spare = slab + tiles + 1 + c
            for g in range(groups):
                for r in range(inner):
                    nat[spare, pl.ds(g * rows + r, rows // inner, stride=inner), :] = plane_tile(g + groups * r, c)
            for g in range(groups):
                nat[slab + c, pl.ds(g, rows, stride=groups), :] = nat[spare, g * rows:(g + 1) * rows, :]
        o_nat.append([nat[slab + c] for c in range(tiles)])
        lse_nat.append(nat[slab + tiles])
        slab += (tiles + 1) * (1 if dil <= DEINTERLEAVE_STRIDE else 2)

    top = jnp.maximum(jnp.maximum(lse_nat[0], lse_nat[1]), lse_nat[2])
    es = [jnp.exp(l - top) for l in lse_nat]
    r_sum = 1.0 / (es[0] + es[1] + es[2])
    ws = [e * r_sum for e in es]
    lanes_per_head = LANES // B_HEADS_PER_GROUP
    ob_heads = []
    for hd in range(B_HEADS_PER_GROUP):
        acc = None
        for w, o in zip(ws, o_nat):
            term = w[:, hd * lanes_per_head:hd * lanes_per_head + 1] * o[hd]
            acc = term if acc is None else acc + term
        ob_heads.append(acc.astype(BF16))
    ob = jnp.concatenate(ob_heads, axis=1)

    z = None
    for b, o in enumerate((oa_ref[0], ob, om_ref[0])):
        logits = jnp.dot(h, wg_ref[:, b * D_MODEL:(b + 1) * D_MODEL], preferred_element_type=F32)
        gate = jax.nn.sigmoid(logits + bg_ref[b:b + 1, :])
        term = gate * jnp.dot(o, wbr_ref[b], preferred_element_type=F32)
        z = term if z is None else z + term
    y_ref[0] = x + jnp.dot(z.astype(BF16), wout_ref[...], preferred_element_type=F32)


def _merge(x, g_mix, oa, obs, lses, om, w_gate, b_gate, w_branch, w_out):
    n, s, _ = x.shape
    tm = MERGE_ROWS
    cur = lambda b, i: (b, i, 0)
    plane = lambda b, i: (b, 0, i, 0)
    rows = lambda w: pl.BlockSpec((1, tm, w), cur)
    planes = lambda w: [pl.BlockSpec((1, dil, tm // dil, w), plane) for _, dil in B_GROUPS]
    return pl.pallas_call(
        _merge_kernel,
        grid=(n, s // tm),
        in_specs=[rows(D_MODEL), _resident((1, D_MODEL)), rows(A_Q_W)]
                 + planes(B_GROUP_W) + planes(LANES) + [rows(M_W)]
                 + [_resident(w_gate.shape), _resident(b_gate.shape), _resident(w_branch.shape),
                    _resident(w_out.shape)],
        out_specs=rows(D_MODEL),
        out_shape=jax.ShapeDtypeStruct((n, s, D_MODEL), F32),
        scratch_shapes=[pltpu.VMEM((N_MERGE_SLABS, tm, LANES), F32)],
        compiler_params=_params(2),
        name="merge",
    )(x, g_mix, oa, *obs, *lses, om, w_gate, b_gate, w_branch, w_out)


MLP_ROWS = 1024
MLP_FF_CHUNK = 1024


def _mlp_kernel(x_ref, g_ref, wup_ref, wdn_ref, y_ref):
    x = x_ref[...]
    h = _rms_rows(x, g_ref[...]).astype(BF16)
    acc = x
    for lo in range(0, D_FF, MLP_FF_CHUNK):
        u = jnp.dot(h, wup_ref[:, lo:lo + MLP_FF_CHUNK], preferred_element_type=F32)
        a = jnp.square(jnp.maximum(u, 0.0)).astype(BF16)
        acc = acc + jnp.dot(a, wdn_ref[lo:lo + MLP_FF_CHUNK, :], preferred_element_type=F32)
    y_ref[...] = acc


def _mlp(x2d, g_mlp, w_up, w_down):
    t = x2d.shape[0]
    tm = MLP_ROWS
    row = lambda i: (i, 0)
    return pl.pallas_call(
        _mlp_kernel,
        grid=(t // tm,),
        in_specs=[pl.BlockSpec((tm, D_MODEL), row), _resident((1, D_MODEL)),
                  _resident(w_up.shape), _resident(w_down.shape)],
        out_specs=pl.BlockSpec((tm, D_MODEL), row),
        out_shape=jax.ShapeDtypeStruct((t, D_MODEL), F32),
        compiler_params=_params(1),
        name="mlp",
    )(x2d, g_mlp, w_up, w_down)


def _tile2(g):
    return jnp.concatenate([g, g]).reshape(1, LANES)


def _row(g):
    return g.reshape(1, -1)


def _layer(x, mem, g_mix, g_mem, w_in, b_gate, w_mem_kv, gq_a, gk_a, sink_a, gq_b, gk_b, gq_m, gk_m,
           w_branch, w_out, g_mlp, w_up, w_down):
    n, s, _ = x.shape
    bounds = np.cumsum((0, A_Q_W, A_KV_W, A_KV_W, B_W, B_W, B_W, M_W, GATE_W))
    seg = lambda a, b: w_in[:, int(bounds[a]):int(bounds[b])].astype(BF16)
    ws = (seg(0, 1), seg(1, 3), seg(3, 4), seg(4, 5), seg(5, 6), seg(6, 7))
    w_gate = seg(7, 8)
    gq_a, gq_b, gq_m = (gq_a * (A_HEAD_DIM ** -0.5 * LOG2E), gq_b * (B_HEAD_DIM ** -0.5 * LOG2E),
                        gq_m * (M_HEAD_DIM ** -0.5 * LOG2E))
    gains = (_tile2(gq_a), _tile2(gk_a), _row(gq_b), _row(gk_b), _row(gq_m))

    qa, ka, va, qm, *qkv_b = _proj(x, _row(g_mix), ws, gains)
    mk, mv = _mem_kv(mem.reshape(n * N_MEM, D_MODEL), _row(g_mem), w_mem_kv.astype(BF16), _row(gk_m))

    oa = _attn_a(qa, ka, va, sink_a, _logit_bound(gq_a, gk_a, A_HEAD_DIM))
    bound_b = _logit_bound(gq_b, gk_b, B_HEAD_DIM)
    b_out = [_attn_b_group(*qkv_b[3 * gi:3 * gi + 3], gi, bound_b) for gi in range(len(B_GROUPS))]
    om = _attn_m(qm, mk.reshape(n, N_MEM, M_W), mv.reshape(n, N_MEM, M_W),
                 _logit_bound(gq_m, gk_m, M_HEAD_DIM))

    x1 = _merge(x, _row(g_mix), oa, [o for o, _ in b_out], [l for _, l in b_out], om,
                w_gate, b_gate, w_branch.astype(BF16), w_out.astype(BF16))
    y = _mlp(x1.reshape(n * s, D_MODEL), _row(g_mlp), w_up.astype(BF16), w_down.astype(BF16))
    return y.reshape(n, s, D_MODEL)


def kernel(x_prompt, x_sample, mem_prompt, mem_sample, g_mix, g_mem, w_in, b_gate, w_mem_kv, gq_a, gk_a,
           sink_a, gq_b, gk_b, gq_m, gk_m, w_branch, w_out, g_mlp, w_up, w_down):
    depth = w_in.shape[0]

    def run(x, mem):
        for l in range(depth):
            x = _layer(x, mem, g_mix[l], g_mem[l], w_in[l], b_gate[l], w_mem_kv[l], gq_a[l], gk_a[l],
                       sink_a[l], gq_b[l], gk_b[l], gq_m[l], gk_m[l], w_branch[l], w_out[l], g_mlp[l],
                       w_up[l], w_down[l])
        return x

    return (run(x_prompt, mem_prompt), run(x_sample, mem_sample))
```

```python
import functools

import numpy as np
import jax
import jax.numpy as jnp
from jax import lax
from jax.experimental import pallas as pl
from jax.experimental.pallas import tpu as pltpu

D_MODEL = 1024
N_MEM = 256
A_HEADS = 8
A_KV_HEADS = 2
A_HEAD_DIM = 64
A_HALF_WIN = 128
B_GROUPS = ((128, 1), (512, 4), (2048, 16))
B_HEADS_PER_GROUP = 4
B_HEAD_DIM = 128
M_HEADS = 4
M_HEAD_DIM = 128
N_BRANCH = 3
BRANCH_WIDTH = D_MODEL // 2
D_FF = 4 * D_MODEL
EPS = 1e-6
NEG_INF = -1e30

A_Q_W = A_HEADS * A_HEAD_DIM
A_KV_W = A_KV_HEADS * A_HEAD_DIM
B_GROUP_W = B_HEADS_PER_GROUP * B_HEAD_DIM
B_W = len(B_GROUPS) * B_GROUP_W
M_W = M_HEADS * M_HEAD_DIM
GATE_W = N_BRANCH * D_MODEL

LANES = 128
B_HALF_WIN = 64
VMEM_LIMIT_BYTES = 56 * 1024 * 1024

BF16 = jnp.bfloat16
F32 = jnp.float32


def _alibi_slopes(n):
    return [float(2.0 ** (-8.0 * (i + 1) / n)) for i in range(n)]


def _params(n_grid_axes):
    return pltpu.CompilerParams(
        dimension_semantics=("arbitrary",) * n_grid_axes,
        vmem_limit_bytes=VMEM_LIMIT_BYTES)


def _resident(shape):
    zeros = (0,) * len(shape)
    return pl.BlockSpec(shape, lambda *_: zeros, pipeline_mode=pl.Buffered(1))


def _rms_rows(x, gain):
    ms = jnp.mean(x * x, axis=-1, keepdims=True)
    return x * lax.rsqrt(ms + EPS) * gain


def _head_norm128(blk, gain):
    ms = jnp.sum(blk * blk, axis=-1, keepdims=True) * (1.0 / LANES)
    return blk * lax.rsqrt(ms + EPS) * gain


def _head_norm64(blk, gain2):
    low = lax.broadcasted_iota(jnp.int32, blk.shape, 1) < A_HEAD_DIM
    sq = blk * blk
    ss_lo = jnp.sum(jnp.where(low, sq, 0.0), axis=-1, keepdims=True)
    ss_hi = jnp.sum(jnp.where(low, 0.0, sq), axis=-1, keepdims=True)
    ms = jnp.where(low, ss_lo, ss_hi) * (1.0 / A_HEAD_DIM)
    return blk * lax.rsqrt(ms + EPS) * gain2


PROJ_ROWS = 512
AM_QA, AM_QM, AM_KA, AM_VA = 0, A_Q_W, A_Q_W + M_W, A_Q_W + M_W + A_KV_W
AM_W = A_Q_W + M_W + 2 * A_KV_W
B_QKV_W = 3 * B_GROUP_W
DEINTERLEAVE_STRIDE = 4
N_DILATED_SLABS = sum(3 * (B_GROUP_W // LANES) for _, dil in B_GROUPS if dil > 1)


def _proj_kernel(x_ref, g_ref, wqa_ref, wkva_ref, wqb_ref, wkb_ref, wvb_ref, wqm_ref,
                 gqa_ref, gka_ref, gqb_ref, gkb_ref, gqm_ref,
                 am_ref, *rest):
    b_refs, ybuf = rest[:-1], rest[-1]
    tm = x_ref.shape[1]
    h = _rms_rows(x_ref[0], g_ref[...]).astype(BF16)

    def mm(w_ref, lo, hi):
        return lambda: jnp.dot(h, w_ref[:, lo:hi], preferred_element_type=F32)

    def finish_qa(y):
        for c in range(A_Q_W // LANES):
            sl = slice(c * LANES, (c + 1) * LANES)
            am_ref[0, :, AM_QA + c * LANES:AM_QA + (c + 1) * LANES] = (
                _head_norm64(y[:, sl], gqa_ref[...]).astype(BF16))

    def finish_kva(y):
        am_ref[0, :, AM_KA:AM_KA + A_KV_W] = _head_norm64(y[:, :A_KV_W], gka_ref[...]).astype(BF16)
        am_ref[0, :, AM_VA:AM_VA + A_KV_W] = y[:, A_KV_W:].astype(BF16)

    def finish_qm(y):
        for c in range(M_W // LANES):
            sl = slice(c * LANES, (c + 1) * LANES)
            am_ref[0, :, AM_QM + c * LANES:AM_QM + (c + 1) * LANES] = (
                _head_norm128(y[:, sl], gqm_ref[...]).astype(BF16))

    def finish_b(o_ref, col0, gain_ref, dil, slab):
        def finish(y):
            for c in range(B_GROUP_W // LANES):
                sl = slice(c * LANES, (c + 1) * LANES)
                out = slice(col0 + c * LANES, col0 + (c + 1) * LANES)
                blk = y[:, sl]
                if gain_ref is not None:
                    blk = _head_norm128(blk, gain_ref[...])
                if dil == 1:
                    o_ref[0, 0, :, out] = blk.astype(BF16)
                else:
                    ybuf[slab + c] = blk
                    groups, stride = 1, dil
                    while stride > DEINTERLEAVE_STRIDE:
                        rows = tm // groups
                        parts = [ybuf[slab + c, pl.ds(g * rows + r, rows // DEINTERLEAVE_STRIDE,
                                                      stride=DEINTERLEAVE_STRIDE), :]
                                 for g in range(groups) for r in range(DEINTERLEAVE_STRIDE)]
                        ybuf[slab + c] = jnp.concatenate(parts, axis=0)
                        groups, stride = groups * DEINTERLEAVE_STRIDE, stride // DEINTERLEAVE_STRIDE
                    rows = tm // groups
                    for g in range(groups):
                        for r in range(stride):
                            o_ref[0, g + groups * r, :, out] = (
                                ybuf[slab + c, pl.ds(g * rows + r, rows // stride, stride=stride), :].astype(BF16))
        return finish

    def branch_b(gi, slab):
        dil = B_GROUPS[gi][1]
        lo = gi * B_GROUP_W
        parts = []
        for part, (w_ref, gain_ref) in enumerate(((wqb_ref, gqb_ref), (wkb_ref, gkb_ref), (wvb_ref, None))):
            parts.append((mm(w_ref, lo, lo + B_GROUP_W),
                          finish_b(b_refs[gi], part * B_GROUP_W, gain_ref, dil, slab)))
            slab += B_GROUP_W // LANES
        return parts, slab

    by_dilation = sorted(range(len(B_GROUPS)), key=lambda gi: -B_GROUPS[gi][1])
    stages, slab = [], 0
    for gi in by_dilation[:-1]:
        parts, slab = branch_b(gi, slab)
        stages += parts
    stages += [(mm(wqa_ref, 0, A_Q_W), finish_qa), (mm(wkva_ref, 0, 2 * A_KV_W), finish_kva),
               (mm(wqm_ref, 0, M_W), finish_qm)]
    stages += branch_b(by_dilation[-1], slab)[0]

    pending = None
    for matmul, finish in stages:
        y = matmul()
        if pending is not None:
            pending[0](pending[1])
        pending = (finish, y)
    pending[0](pending[1])


def _proj(x, g_mix, ws, gains):
    n, s, _ = x.shape
    tm = PROJ_ROWS
    cur = lambda b, i: (b, i, 0)
    plane = lambda b, i: (b, 0, i, 0)
    in_specs = ([pl.BlockSpec((1, tm, D_MODEL), cur), _resident((1, D_MODEL))]
                + [_resident(w.shape) for w in ws]
                + [_resident((1, LANES)) for _ in gains])
    out_specs = [pl.BlockSpec((1, tm, AM_W), cur)]
    out_shape = [jax.ShapeDtypeStruct((n, s, AM_W), BF16)]
    for _, dil in B_GROUPS:
        out_specs.append(pl.BlockSpec((1, dil, tm // dil, B_QKV_W), plane))
        out_shape.append(jax.ShapeDtypeStruct((n, dil, s // dil, B_QKV_W), BF16))
    return pl.pallas_call(
        _proj_kernel,
        grid=(n, s // tm),
        in_specs=in_specs,
        out_specs=out_specs,
        out_shape=out_shape,
        scratch_shapes=[pltpu.VMEM((N_DILATED_SLABS, tm, LANES), F32)],
        compiler_params=_params(2),
        name="proj",
    )(x, g_mix, *ws, *gains)


MEM_ROWS = 1024


def _mem_kv_kernel(m_ref, g_ref, w_ref, gk_ref, k_ref, v_ref):
    h = _rms_rows(m_ref[...], g_ref[...]).astype(BF16)
    y = jnp.dot(h, w_ref[...], preferred_element_type=F32)
    for c in range(M_HEADS):
        sl = slice(c * LANES, (c + 1) * LANES)
        k_ref[:, sl] = _head_norm128(y[:, sl], gk_ref[...]).astype(BF16)
    v_ref[...] = y[:, M_W:].astype(BF16)


def _mem_kv(mem2d, g_mem, w_mem_kv, gk_m):
    t = mem2d.shape[0]
    tm = min(t, MEM_ROWS)
    row = lambda i: (i, 0)
    return pl.pallas_call(
        _mem_kv_kernel,
        grid=(t // tm,),
        in_specs=[pl.BlockSpec((tm, D_MODEL), row), _resident((1, D_MODEL)),
                  _resident(w_mem_kv.shape), _resident((1, LANES))],
        out_specs=[pl.BlockSpec((tm, M_W), row)] * 2,
        out_shape=[jax.ShapeDtypeStruct((t, M_W), BF16)] * 2,
        compiler_params=_params(1),
        name="mem_kv",
    )(mem2d, g_mem, w_mem_kv, gk_m)


A_BLOCK = A_HALF_WIN
A_STEP = 2048
ATTN_CHUNK = 512
INTERIOR, FIRST, LAST = 0, 1, 2
LOG2E = float(np.log2(np.e))
LN2 = float(np.log(2.0))

SOFTMAX_SHIFT_LIMIT = 40.0
BF16_SLACK = 1.0 + 2.0 ** -6


def _logit_bound(gain_q, gain_k, head_dim):
    return head_dim * jnp.max(jnp.abs(gain_q)) * jnp.max(jnp.abs(gain_k)) * BF16_SLACK


def _softmax_ctl(shifts):
    shifts = jnp.asarray(shifts, F32).reshape(-1)
    fixed = jnp.all(shifts <= SOFTMAX_SHIFT_LIMIT)
    return jnp.concatenate([fixed.astype(F32).reshape(1), shifts])


def _by_softmax_shift(ctl_ref, run):
    fixed = ctl_ref[0] > 0.5

    @pl.when(fixed)
    def _():
        run(True)

    @pl.when(jnp.logical_not(fixed))
    def _():
        run(False)


def _attn_a_init_bias(bias_ref, ctl_ref):
    n_keys = 3 * A_BLOCK
    q_row = lax.broadcasted_iota(jnp.int32, (A_BLOCK, n_keys), 0)
    k_col = lax.broadcasted_iota(jnp.int32, (A_BLOCK, n_keys), 1)
    dist = jnp.abs(q_row + A_BLOCK - k_col)
    dist_f = dist.astype(F32)
    for head, slope in enumerate(_alibi_slopes(A_HEADS)):
        shift = jnp.where(ctl_ref[0] > 0.5, ctl_ref[1 + head], 0.0)
        base = jnp.where(dist <= A_HALF_WIN, (-slope * LOG2E) * dist_f - shift, NEG_INF)
        bias_ref[INTERIOR, head] = base
        bias_ref[FIRST, head] = jnp.where(k_col >= A_BLOCK, base, NEG_INF)
        bias_ref[LAST, head] = jnp.where(k_col < 2 * A_BLOCK, base, NEG_INF)


def _attn_a_load_kv(kv_refs, kbuf, vbuf):
    kp_ref, kc_ref, kn_ref, vp_ref, vc_ref, vn_ref = kv_refs
    rows = kc_ref.shape[1]
    for buf, prev, cur, nxt in ((kbuf, kp_ref, kc_ref, kn_ref), (vbuf, vp_ref, vc_ref, vn_ref)):
        buf[0:A_BLOCK, :] = prev[0]
        buf[A_BLOCK:A_BLOCK + rows, :] = cur[0]
        buf[A_BLOCK + rows:, :] = nxt[0]


def _attn_a_pass(row0, first_block, n_blocks, fixed_shift, q_ref, kbuf, vbuf, bias_ref, sink_ref, ctl_ref,
                 o_ref):
    n_keys = 3 * A_BLOCK
    low = lax.broadcasted_iota(jnp.int32, (n_keys, LANES), 1) < A_HEAD_DIM
    low_q = lax.broadcasted_iota(jnp.int32, (A_BLOCK, LANES), 1) < A_HEAD_DIM
    zero = jnp.zeros((n_keys, LANES), BF16)
    nt = (((1,), (1,)), ((), ()))
    per_chunk = ATTN_CHUNK // A_BLOCK
    units = [(j, kvh) for j in range(per_chunk) for kvh in range(A_KV_HEADS)]
    logits, values, probs = {}, {}, {}

    def swap_halves(t):
        return jnp.concatenate([t[:, A_HEAD_DIM:], t[:, :A_HEAD_DIM]], axis=1)

    def padded(t, t_sw, kvh):
        lo, hi = (t, t_sw) if kvh == 0 else (t_sw, t)
        return jnp.concatenate([jnp.where(low, lo, zero), jnp.where(low, zero, hi)], axis=0)

    def logit_matmuls():
        for j in range(per_chunk):
            rows = pl.ds(row0 + j * A_BLOCK, A_BLOCK)
            kw = kbuf[pl.ds(row0 + j * A_BLOCK, n_keys), :]
            vw = vbuf[pl.ds(row0 + j * A_BLOCK, n_keys), :]
            kw_sw, vw_sw = swap_halves(kw), swap_halves(vw)
            for kvh in range(A_KV_HEADS):
                q2 = jnp.concatenate([q_ref[0, rows, (2 * kvh + c) * LANES:(2 * kvh + c + 1) * LANES]
                                      for c in range(2)], axis=0)
                logits[j, kvh] = lax.dot_general(q2, padded(kw, kw_sw, kvh), nt,
                                                 preferred_element_type=F32)
                values[j, kvh] = padded(vw, vw_sw, kvh)

    def softmaxes():
        for j, kvh in units:
            block = first_block + j
            variant = jnp.where(block == 0, FIRST, jnp.where(block == n_blocks - 1, LAST, INTERIOR))
            s2 = logits[j, kvh]
            p_rows, r_rows = [], []
            for c in range(2):
                ps, sums, sink_gaps = [], [], []
                for half in range(2):
                    head = 2 * (2 * kvh + c) + half
                    sh = s2[c * A_BLOCK:(c + 1) * A_BLOCK, half * n_keys:(half + 1) * n_keys]
                    sh = sh + bias_ref[variant, head]
                    sink = sink_ref[head] * LOG2E
                    if fixed_shift:
                        m = ctl_ref[1 + head]
                        p = jnp.exp2(sh)
                    else:
                        m = jnp.maximum(jnp.max(sh, axis=-1, keepdims=True), sink)
                        p = jnp.exp2(sh - m)
                    ps.append(p.astype(BF16))
                    sums.append(jnp.sum(p, axis=-1, keepdims=True))
                    sink_gaps.append(sink - m)
                sink_lanes = low_q[:1] if fixed_shift else low_q
                den = (jnp.where(low_q, sums[0], sums[1])
                       + jnp.exp2(jnp.where(sink_lanes, sink_gaps[0], sink_gaps[1])))
                p_rows.append(jnp.concatenate(ps, axis=1))
                r_rows.append(1.0 / den)
            probs[j, kvh] = (jnp.concatenate(p_rows, axis=0), jnp.concatenate(r_rows, axis=0))

    def value_matmuls():
        for j, kvh in units:
            p2, r2 = probs[j, kvh]
            o2 = jnp.dot(p2, values[j, kvh], preferred_element_type=F32) * r2
            for c in range(2):
                tile = 2 * kvh + c
                o_ref[0, pl.ds(row0 + j * A_BLOCK, A_BLOCK), tile * LANES:(tile + 1) * LANES] = (
                    o2[c * A_BLOCK:(c + 1) * A_BLOCK].astype(BF16))

    return logit_matmuls, softmaxes, value_matmuls


def _attn_a_kernel(seq_len, q_ref, kp_ref, kc_ref, kn_ref, vp_ref, vc_ref, vn_ref, sink_ref, ctl_ref,
                   o_ref, kbuf, vbuf, bias_ref):
    i = pl.program_id(1)
    step_rows = q_ref.shape[1]

    @pl.when((pl.program_id(0) == 0) & (i == 0))
    def _():
        _attn_a_init_bias(bias_ref, ctl_ref)

    _attn_a_load_kv((kp_ref, kc_ref, kn_ref, vp_ref, vc_ref, vn_ref), kbuf, vbuf)

    def run(fixed_shift):
        def chunk(t, carry):
            row0 = pl.multiple_of(t * ATTN_CHUNK, ATTN_CHUNK)
            for phase in _attn_a_pass(row0, (i * step_rows + row0) // A_BLOCK, seq_len // A_BLOCK,
                                      fixed_shift, q_ref, kbuf, vbuf, bias_ref, sink_ref, ctl_ref, o_ref):
                phase()
            return carry

        lax.fori_loop(0, step_rows // ATTN_CHUNK, chunk, 0)

    _by_softmax_shift(ctl_ref, run)


def _attn_a_specs(s, step_rows, position):
    per_step = step_rows // A_BLOCK
    n_blocks = s // A_BLOCK
    assert n_blocks >= 2 and s % step_rows == 0

    def cur(col):
        def index(*ids):
            b, i = position(*ids)
            return (b, i, col)
        return index

    def prev(col):
        def index(*ids):
            b, i = position(*ids)
            return (b, jnp.maximum(i * per_step - 1, 0), col)
        return index

    def nxt(col):
        def index(*ids):
            b, i = position(*ids)
            return (b, jnp.minimum((i + 1) * per_step, n_blocks - 1), col)
        return index

    q_spec = pl.BlockSpec((1, step_rows, A_Q_W), cur(AM_QA // A_Q_W))
    o_spec = pl.BlockSpec((1, step_rows, A_Q_W), cur(0))
    kv_specs = [[pl.BlockSpec((1, A_BLOCK, A_KV_W), prev(col // A_KV_W)),
                 pl.BlockSpec((1, step_rows, A_KV_W), cur(col // A_KV_W)),
                 pl.BlockSpec((1, A_BLOCK, A_KV_W), nxt(col // A_KV_W))] for col in (AM_KA, AM_VA)]
    scratch = [pltpu.VMEM((step_rows + 2 * A_BLOCK, A_KV_W), BF16)] * 2 + [
        pltpu.VMEM((3, A_HEADS, A_BLOCK, 3 * A_BLOCK), F32)]
    return q_spec, o_spec, kv_specs, scratch


def _attn_a(am, sink, logit_bound):
    n, s, _ = am.shape
    q_spec, o_spec, kv_specs, scratch = _attn_a_specs(s, A_STEP, lambda b, i: (b, i))
    smem = pl.BlockSpec(memory_space=pltpu.SMEM)
    ctl = _softmax_ctl(jnp.maximum(logit_bound, sink * LOG2E))
    return pl.pallas_call(
        functools.partial(_attn_a_kernel, s),
        grid=(n, s // A_STEP),
        in_specs=[q_spec] + kv_specs[0] + kv_specs[1] + [smem, smem],
        out_specs=o_spec,
        out_shape=jax.ShapeDtypeStruct((n, s, A_Q_W), BF16),
        scratch_shapes=scratch,
        compiler_params=_params(2),
        name="attn_a",
    )(*[am] * 7, sink, ctl)


B_BLOCK = 2 * B_HALF_WIN
B_KEYS = B_BLOCK + 2 * B_HALF_WIN
B_ROWS_PER_STEP = 2048


def _attn_b_kernel(sub_len, slopes, q_ref, kp_ref, kc_ref, kn_ref, vp_ref, vc_ref, vn_ref, ctl_ref,
                   o_ref, lse_ref, kbuf, vbuf, bias_ref):
    i = pl.program_id(2)
    n_planes, step = q_ref.shape[1], q_ref.shape[2]
    n_blocks = sub_len // B_BLOCK

    @pl.when((pl.program_id(0) == 0) & (pl.program_id(1) == 0) & (i == 0))
    def _():
        shift = jnp.where(ctl_ref[0] > 0.5, ctl_ref[1], 0.0)
        q_row = lax.broadcasted_iota(jnp.int32, (B_BLOCK, B_KEYS), 0)
        k_col = lax.broadcasted_iota(jnp.int32, (B_BLOCK, B_KEYS), 1)
        dist = jnp.abs(q_row + B_HALF_WIN - k_col)
        dist_f = dist.astype(F32)
        for head, slope in enumerate(slopes):
            base = jnp.where(dist <= B_HALF_WIN, (-slope * LOG2E) * dist_f - shift, NEG_INF)
            first = jnp.where(k_col >= B_HALF_WIN, base, NEG_INF)
            bias_ref[0, head] = base
            bias_ref[1, head] = first
            bias_ref[2, head] = jnp.where(k_col < B_HALF_WIN + B_BLOCK, base, NEG_INF)
            bias_ref[3, head] = jnp.where(k_col < B_HALF_WIN + B_BLOCK, first, NEG_INF)

    for buf, prev, cur, nxt in ((kbuf, kp_ref, kc_ref, kn_ref), (vbuf, vp_ref, vc_ref, vn_ref)):
        for r in range(n_planes):
            buf[r, 0:B_HALF_WIN, :] = prev[0, r]
            buf[r, B_HALF_WIN:B_HALF_WIN + step, :] = cur[0, r]
            buf[r, B_HALF_WIN + step:, :] = nxt[0, r]

    lane = lax.broadcasted_iota(jnp.int32, (B_BLOCK, LANES), 1)
    lanes_per_head = LANES // B_HEADS_PER_GROUP
    heads = [slice(h * B_HEAD_DIM, (h + 1) * B_HEAD_DIM) for h in range(B_HEADS_PER_GROUP)]
    nt = (((1,), (1,)), ((), ()))
    rows_per_pass = min(step, ATTN_CHUNK)
    planes_per_pass = ATTN_CHUNK // rows_per_pass
    passes_per_plane = step // rows_per_pass
    blocks = [(dr, j) for dr in range(planes_per_pass) for j in range(rows_per_pass // B_BLOCK)]

    def one_pass(fixed_shift, t, carry):
        if planes_per_pass == 1:
            plane0 = t // passes_per_plane
            row0 = pl.multiple_of((t % passes_per_plane) * ATTN_CHUNK, ATTN_CHUNK)
        else:
            plane0, row0 = t * planes_per_pass, 0

        logits = {}
        for dr, j in blocks:
            for h, sl in enumerate(heads):
                q = q_ref[0, plane0 + dr, pl.ds(row0 + j * B_BLOCK, B_BLOCK), sl]
                k = kbuf[plane0 + dr, pl.ds(row0 + j * B_BLOCK, B_KEYS), sl]
                logits[dr, j, h] = lax.dot_general(q, k, nt, preferred_element_type=F32)

        probs = {}
        for dr, j in blocks:
            block = (i * step + row0) // B_BLOCK + j
            variant = (block == 0).astype(jnp.int32) + 2 * (block == n_blocks - 1).astype(jnp.int32)
            m_tile = den_tile = None
            for h in range(B_HEADS_PER_GROUP):
                s = logits[dr, j, h] + bias_ref[variant, h]
                if fixed_shift:
                    m = ctl_ref[1]
                    p = jnp.exp2(s)
                else:
                    m = jnp.max(s, axis=-1, keepdims=True)
                    p = jnp.exp2(s - m)
                den = jnp.sum(p, axis=-1, keepdims=True)
                probs[dr, j, h] = (p.astype(BF16), 1.0 / den)
                den_tile = den if h == 0 else jnp.where(lane >= h * lanes_per_head, den, den_tile)
                if not fixed_shift:
                    m_tile = m if h == 0 else jnp.where(lane >= h * lanes_per_head, m, m_tile)
            lse_ref[0, plane0 + dr, pl.ds(row0 + j * B_BLOCK, B_BLOCK), :] = (
                ((m if fixed_shift else m_tile) + jnp.log2(den_tile)) * LN2)

        for dr, j in blocks:
            for h, sl in enumerate(heads):
                p, rden = probs[dr, j, h]
                v = vbuf[plane0 + dr, pl.ds(row0 + j * B_BLOCK, B_KEYS), sl]
                o = jnp.dot(p, v, preferred_element_type=F32) * rden
                o_ref[0, plane0 + dr, pl.ds(row0 + j * B_BLOCK, B_BLOCK), sl] = o.astype(BF16)
        return carry

    _by_softmax_shift(ctl_ref, lambda fixed_shift: lax.fori_loop(
        0, n_planes * step // ATTN_CHUNK, functools.partial(one_pass, fixed_shift), 0))


def _attn_b_group(qkv, gi, logit_bound):
    n, dil, sub, _ = qkv.shape
    step = min(sub, B_ROWS_PER_STEP)
    n_planes = B_ROWS_PER_STEP // step
    assert sub % step == 0 and dil % n_planes == 0 and step % B_BLOCK == 0
    assert ATTN_CHUNK % step == 0 or step % ATTN_CHUNK == 0
    halo_per_step = step // B_HALF_WIN
    last_halo = sub // B_HALF_WIN - 1
    cur = lambda col: lambda b, r, i: (b, r, i, col)
    prev = lambda col: lambda b, r, i: (b, r, jnp.maximum(i * halo_per_step - 1, 0), col)
    nxt = lambda col: lambda b, r, i: (b, r, jnp.minimum((i + 1) * halo_per_step, last_halo), col)
    kv_specs = [[pl.BlockSpec((1, n_planes, B_HALF_WIN, B_GROUP_W), prev(col)),
                 pl.BlockSpec((1, n_planes, step, B_GROUP_W), cur(col)),
                 pl.BlockSpec((1, n_planes, B_HALF_WIN, B_GROUP_W), nxt(col))] for col in (1, 2)]
    all_slopes = _alibi_slopes(len(B_GROUPS) * B_HEADS_PER_GROUP)
    slopes = [all_slopes[gi * B_HEADS_PER_GROUP + h] * dil for h in range(B_HEADS_PER_GROUP)]
    return pl.pallas_call(
        functools.partial(_attn_b_kernel, sub, slopes),
        grid=(n, dil // n_planes, sub // step),
        in_specs=[pl.BlockSpec((1, n_planes, step, B_GROUP_W), cur(0))] + kv_specs[0] + kv_specs[1]
                 + [pl.BlockSpec(memory_space=pltpu.SMEM)],
        out_specs=[pl.BlockSpec((1, n_planes, step, B_GROUP_W), cur(0)),
                   pl.BlockSpec((1, n_planes, step, LANES), cur(0))],
        out_shape=[jax.ShapeDtypeStruct((n, dil, sub, B_GROUP_W), BF16),
                   jax.ShapeDtypeStruct((n, dil, sub, LANES), F32)],
        scratch_shapes=[pltpu.VMEM((n_planes, step + 2 * B_HALF_WIN, B_GROUP_W), BF16)] * 2
                       + [pltpu.VMEM((4, B_HEADS_PER_GROUP, B_BLOCK, B_KEYS), F32)],
        compiler_params=_params(3),
        name=f"attn_b{gi}",
    )(*[qkv] * 7, _softmax_ctl(logit_bound))


M_STEP = 2048


def _attn_m_kernel(q_ref, k_ref, v_ref, ctl_ref, o_ref):
    heads = [slice(h * M_HEAD_DIM, (h + 1) * M_HEAD_DIM) for h in range(M_HEADS)]
    nt = (((1,), (1,)), ((), ()))

    def chunk(fixed_shift, t, carry):
        rows = pl.ds(pl.multiple_of(t * ATTN_CHUNK, ATTN_CHUNK), ATTN_CHUNK)
        logits = [lax.dot_general(q_ref[0, rows, sl], k_ref[0, :, sl], nt, preferred_element_type=F32)
                  for sl in heads]
        probs = []
        for s in logits:
            m = ctl_ref[1] if fixed_shift else jnp.max(s, axis=-1, keepdims=True)
            p = jnp.exp2(s - m)
            probs.append((p.astype(BF16), 1.0 / jnp.sum(p, axis=-1, keepdims=True)))
        for sl, (p, rden) in zip(heads, probs):
            o = jnp.dot(p, v_ref[0, :, sl], preferred_element_type=F32) * rden
            o_ref[0, rows, sl] = o.astype(BF16)
        return carry

    _by_softmax_shift(ctl_ref, lambda fixed_shift: lax.fori_loop(
        0, M_STEP // ATTN_CHUNK, functools.partial(chunk, fixed_shift), 0))


def _attn_m(am, mk, mv, logit_bound):
    n, s, _ = am.shape
    cur = lambda b, i: (b, i, 0)
    q_cols = lambda b, i: (b, i, AM_QM // M_W)
    mem = lambda b, i: (b, 0, 0)
    return pl.pallas_call(
        _attn_m_kernel,
        grid=(n, s // M_STEP),
        in_specs=[pl.BlockSpec((1, M_STEP, M_W), q_cols), pl.BlockSpec((1, N_MEM, M_W), mem),
                  pl.BlockSpec((1, N_MEM, M_W), mem), pl.BlockSpec(memory_space=pltpu.SMEM)],
        out_specs=pl.BlockSpec((1, M_STEP, M_W), cur),
        out_shape=jax.ShapeDtypeStruct((n, s, M_W), BF16),
        compiler_params=_params(2),
        name="attn_m",
    )(am, mk, mv, _softmax_ctl(logit_bound))


MERGE_ROWS = 512
assert all(dil <= DEINTERLEAVE_STRIDE ** 2 for _, dil in B_GROUPS)
N_MERGE_SLABS = sum((B_GROUP_W // LANES + 1) * (1 if dil <= DEINTERLEAVE_STRIDE else 2)
                    for _, dil in B_GROUPS if dil > 1)


def _merge_kernel(x_ref, g_ref, oa_ref, o0_ref, o1_ref, o2_ref, l0_ref, l1_ref, l2_ref, om_ref,
                  wg_ref, bg_ref, wbr_ref, wout_ref, y_ref, nat):
    tm = x_ref.shape[1]
    x = x_ref[0]
    h = _rms_rows(x, g_ref[...]).astype(BF16)
    tiles = B_GROUP_W // LANES

    o_nat, lse_nat = [], []
    slab = 0
    for (_, dil), o_ref, l_ref in zip(B_GROUPS, (o0_ref, o1_ref, o2_ref), (l0_ref, l1_ref, l2_ref)):
        if dil == 1:
            o_nat.append([o_ref[0, 0, :, c * LANES:(c + 1) * LANES].astype(F32) for c in range(tiles)])
            lse_nat.append(l_ref[0, 0])
            continue
        def plane_tile(r, c):
            if c == tiles:
                return l_ref[0, r]
            return o_ref[0, r, :, c * LANES:(c + 1) * LANES].astype(F32)

        for c in range(tiles + 1):
            if dil <= DEINTERLEAVE_STRIDE:
                for r in range(dil):
                    nat[slab + c, pl.ds(r, tm // dil, stride=dil), :] = plane_tile(r, c)
                continue
            groups = DEINTERLEAVE_STRIDE
            inner, rows = dil // groups, tm // groups
            spare = slab + tiles + 1 + c
            for g in range(groups):
                for r in range(inner):
                    nat[spare, pl.ds(g * rows + r, rows // inner, stride=inner), :] = plane_tile(g + groups * r, c)
            for g in range(groups):
                nat[slab + c, pl.ds(g, rows, stride=groups), :] = nat[spare, g * rows:(g + 1) * rows, :]
        o_nat.append([nat[slab + c] for c in range(tiles)])
        lse_nat.append(nat[slab + tiles])
        slab += (tiles + 1) * (1 if dil <= DEINTERLEAVE_STRIDE else 2)

    top = jnp.maximum(jnp.maximum(lse_nat[0], lse_nat[1]), lse_nat[2])
    es = [jnp.exp(l - top) for l in lse_nat]
    r_sum = 1.0 / (es[0] + es[1] + es[2])
    ws = [e * r_sum for e in es]
    lanes_per_head = LANES // B_HEADS_PER_GROUP
    ob_heads = []
    for hd in range(B_HEADS_PER_GROUP):
        acc = None
        for w, o in zip(ws, o_nat):
            term = w[:, hd * lanes_per_head:hd * lanes_per_head + 1] * o[hd]
            acc = term if acc is None else acc + term
        ob_heads.append(acc.astype(BF16))
    ob = jnp.concatenate(ob_heads, axis=1)

    z = None
    for b, o in enumerate((oa_ref[0], ob, om_ref[0])):
        logits = jnp.dot(h, wg_ref[:, b * D_MODEL:(b + 1) * D_MODEL], preferred_element_type=F32)
        gate = jax.nn.sigmoid(logits + bg_ref[b:b + 1, :])
        term = gate * jnp.dot(o, wbr_ref[b], preferred_element_type=F32)
        z = term if z is None else z + term
    y_ref[0] = x + jnp.dot(z.astype(BF16), wout_ref[...], preferred_element_type=F32)


def _merge(x, g_mix, oa, obs, lses, om, w_gate, b_gate, w_branch, w_out):
    n, s, _ = x.shape
    tm = MERGE_ROWS
    cur = lambda b, i: (b, i, 0)
    plane = lambda b, i: (b, 0, i, 0)
    rows = lambda w: pl.BlockSpec((1, tm, w), cur)
    planes = lambda w: [pl.BlockSpec((1, dil, tm // dil, w), plane) for _, dil in B_GROUPS]
    return pl.pallas_call(
        _merge_kernel,
        grid=(n, s // tm),
        in_specs=[rows(D_MODEL), _resident((1, D_MODEL)), rows(A_Q_W)]
                 + planes(B_GROUP_W) + planes(LANES) + [rows(M_W)]
                 + [_resident(w_gate.shape), _resident(b_gate.shape), _resident(w_branch.shape),
                    _resident(w_out.shape)],
        out_specs=rows(D_MODEL),
        out_shape=jax.ShapeDtypeStruct((n, s, D_MODEL), F32),
        scratch_shapes=[pltpu.VMEM((N_MERGE_SLABS, tm, LANES), F32)],
        compiler_params=_params(2),
        name="merge",
    )(x, g_mix, oa, *obs, *lses, om, w_gate, b_gate, w_branch, w_out)


MLP_ROWS = 1024
MLP_FF_CHUNK = 1024


def _mlp_kernel(x_ref, g_ref, wup_ref, wdn_ref, y_ref):
    x = x_ref[...]
    h = _rms_rows(x, g_ref[...]).astype(BF16)
    acc = x
    for lo in range(0, D_FF, MLP_FF_CHUNK):
        u = jnp.dot(h, wup_ref[:, lo:lo + MLP_FF_CHUNK], preferred_element_type=F32)
        a = jnp.square(jnp.maximum(u, 0.0)).astype(BF16)
        acc = acc + jnp.dot(a, wdn_ref[lo:lo + MLP_FF_CHUNK, :], preferred_element_type=F32)
    y_ref[...] = acc


def _mlp(x2d, g_mlp, w_up, w_down):
    t = x2d.shape[0]
    tm = MLP_ROWS
    row = lambda i: (i, 0)
    return pl.pallas_call(
        _mlp_kernel,
        grid=(t // tm,),
        in_specs=[pl.BlockSpec((tm, D_MODEL), row), _resident((1, D_MODEL)),
                  _resident(w_up.shape), _resident(w_down.shape)],
        out_specs=pl.BlockSpec((tm, D_MODEL), row),
        out_shape=jax.ShapeDtypeStruct((t, D_MODEL), F32),
        compiler_params=_params(1),
        name="mlp",
    )(x2d, g_mlp, w_up, w_down)


def _tile2(g):
    return jnp.concatenate([g, g]).reshape(1, LANES)


def _row(g):
    return g.reshape(1, -1)


def _layer(x, mem, g_mix, g_mem, w_in, b_gate, w_mem_kv, gq_a, gk_a, sink_a, gq_b, gk_b, gq_m, gk_m,
           w_branch, w_out, g_mlp, w_up, w_down):
    n, s, _ = x.shape
    bounds = np.cumsum((0, A_Q_W, A_KV_W, A_KV_W, B_W, B_W, B_W, M_W, GATE_W))
    seg = lambda a, b: w_in[:, int(bounds[a]):int(bounds[b])].astype(BF16)
    ws = (seg(0, 1), seg(1, 3), seg(3, 4), seg(4, 5), seg(5, 6), seg(6, 7))
    w_gate = seg(7, 8)
    gq_a, gq_b, gq_m = (gq_a * (A_HEAD_DIM ** -0.5 * LOG2E), gq_b * (B_HEAD_DIM ** -0.5 * LOG2E),
                        gq_m * (M_HEAD_DIM ** -0.5 * LOG2E))
    gains = (_tile2(gq_a), _tile2(gk_a), _row(gq_b), _row(gk_b), _row(gq_m))

    am, *qkv_b = _proj(x, _row(g_mix), ws, gains)
    mk, mv = _mem_kv(mem.reshape(n * N_MEM, D_MODEL), _row(g_mem), w_mem_kv.astype(BF16), _row(gk_m))

    oa = _attn_a(am, sink_a, _logit_bound(gq_a, gk_a, A_HEAD_DIM))
    bound_b = _logit_bound(gq_b, gk_b, B_HEAD_DIM)
    b_out = [_attn_b_group(qkv_b[gi], gi, bound_b) for gi in range(len(B_GROUPS))]
    om = _attn_m(am, mk.reshape(n, N_MEM, M_W), mv.reshape(n, N_MEM, M_W),
                 _logit_bound(gq_m, gk_m, M_HEAD_DIM))

    x1 = _merge(x, _row(g_mix), oa, [o for o, _ in b_out], [l for _, l in b_out], om,
                w_gate, b_gate, w_branch.astype(BF16), w_out.astype(BF16))
    y = _mlp(x1.reshape(n * s, D_MODEL), _row(g_mlp), w_up.astype(BF16), w_down.astype(BF16))
    return y.reshape(n, s, D_MODEL)


def kernel(x_prompt, x_sample, mem_prompt, mem_sample, g_mix, g_mem, w_in, b_gate, w_mem_kv, gq_a, gk_a,
           sink_a, gq_b, gk_b, gq_m, gk_m, w_branch, w_out, g_mlp, w_up, w_down):
    depth = w_in.shape[0]

    def run(x, mem):
        for l in range(depth):
            x = _layer(x, mem, g_mix[l], g_mem[l], w_in[l], b_gate[l], w_mem_kv[l], gq_a[l], gk_a[l],
                       sink_a[l], gq_b[l], gk_b[l], gq_m[l], gk_m[l], w_branch[l], w_out[l], g_mlp[l],
                       w_up[l], w_down[l])
        return x

    return (run(x_prompt, mem_prompt), run(x_sample, mem_sample))
```

```python
import functools

import numpy as np
import jax
import jax.numpy as jnp
from jax import lax
from jax.experimental import pallas as pl
from jax.experimental.pallas import tpu as pltpu

D_MODEL = 1024
N_MEM = 256
A_HEADS = 8
A_KV_HEADS = 2
A_HEAD_DIM = 64
A_HALF_WIN = 128
B_GROUPS = ((128, 1), (512, 4), (2048, 16))
B_HEADS_PER_GROUP = 4
B_HEAD_DIM = 128
M_HEADS = 4
M_HEAD_DIM = 128
N_BRANCH = 3
BRANCH_WIDTH = D_MODEL // 2
D_FF = 4 * D_MODEL
EPS = 1e-6
NEG_INF = -1e30

A_Q_W = A_HEADS * A_HEAD_DIM
A_KV_W = A_KV_HEADS * A_HEAD_DIM
B_GROUP_W = B_HEADS_PER_GROUP * B_HEAD_DIM
B_W = len(B_GROUPS) * B_GROUP_W
M_W = M_HEADS * M_HEAD_DIM
GATE_W = N_BRANCH * D_MODEL

LANES = 128
B_HALF_WIN = 64
VMEM_LIMIT_BYTES = 56 * 1024 * 1024

BF16 = jnp.bfloat16
F32 = jnp.float32


def _alibi_slopes(n):
    return [float(2.0 ** (-8.0 * (i + 1) / n)) for i in range(n)]


def _params(n_grid_axes):
    return pltpu.CompilerParams(
        dimension_semantics=("arbitrary",) * n_grid_axes,
        vmem_limit_bytes=VMEM_LIMIT_BYTES)


def _resident(shape):
    zeros = (0,) * len(shape)
    return pl.BlockSpec(shape, lambda *_: zeros, pipeline_mode=pl.Buffered(1))


def _rms_rows(x, gain):
    ms = jnp.mean(x * x, axis=-1, keepdims=True)
    return x * lax.rsqrt(ms + EPS) * gain


def _head_norm128(blk, gain):
    ms = jnp.sum(blk * blk, axis=-1, keepdims=True) * (1.0 / LANES)
    return blk * lax.rsqrt(ms + EPS) * gain


def _head_norm64(blk, gain2):
    low = lax.broadcasted_iota(jnp.int32, blk.shape, 1) < A_HEAD_DIM
    sq = blk * blk
    ss_lo = jnp.sum(jnp.where(low, sq, 0.0), axis=-1, keepdims=True)
    ss_hi = jnp.sum(jnp.where(low, 0.0, sq), axis=-1, keepdims=True)
    ms = jnp.where(low, ss_lo, ss_hi) * (1.0 / A_HEAD_DIM)
    return blk * lax.rsqrt(ms + EPS) * gain2


PROJ_ROWS = 512
AM_QA, AM_QM, AM_KA, AM_VA = 0, A_Q_W, A_Q_W + M_W, A_Q_W + M_W + A_KV_W
AM_W = A_Q_W + M_W + 2 * A_KV_W
B_QKV_W = 3 * B_GROUP_W
DEINTERLEAVE_STRIDE = 4
N_DILATED_SLABS = sum(3 * (B_GROUP_W // LANES) for _, dil in B_GROUPS if dil > 1)


def _proj_kernel(x_ref, g_ref, wqa_ref, wkva_ref, wqb_ref, wkb_ref, wvb_ref, wqm_ref,
                 gqa_ref, gka_ref, gqb_ref, gkb_ref, gqm_ref,
                 am_ref, *rest):
    b_refs, ybuf = rest[:-1], rest[-1]
    tm = x_ref.shape[1]
    h = _rms_rows(x_ref[0], g_ref[...]).astype(BF16)

    def mm(w_ref, lo, hi):
        return lambda: jnp.dot(h, w_ref[:, lo:hi], preferred_element_type=F32)

    def finish_qa(y):
        for c in range(A_Q_W // LANES):
            sl = slice(c * LANES, (c + 1) * LANES)
            am_ref[0, :, AM_QA + c * LANES:AM_QA + (c + 1) * LANES] = (
                _head_norm64(y[:, sl], gqa_ref[...]).astype(BF16))

    def finish_kva(y):
        am_ref[0, :, AM_KA:AM_KA + A_KV_W] = _head_norm64(y[:, :A_KV_W], gka_ref[...]).astype(BF16)
        am_ref[0, :, AM_VA:AM_VA + A_KV_W] = y[:, A_KV_W:].astype(BF16)

    def finish_qm(y):
        for c in range(M_W // LANES):
            sl = slice(c * LANES, (c + 1) * LANES)
            am_ref[0, :, AM_QM + c * LANES:AM_QM + (c + 1) * LANES] = (
                _head_norm128(y[:, sl], gqm_ref[...]).astype(BF16))

    def finish_b(o_ref, col0, gain_ref, dil, slab):
        def finish(y):
            for c in range(B_GROUP_W // LANES):
                sl = slice(c * LANES, (c + 1) * LANES)
                out = slice(col0 + c * LANES, col0 + (c + 1) * LANES)
                blk = y[:, sl]
                if gain_ref is not None:
                    blk = _head_norm128(blk, gain_ref[...])
                if dil == 1:
                    o_ref[0, 0, :, out] = blk.astype(BF16)
                else:
                    ybuf[slab + c] = blk
                    groups, stride = 1, dil
                    while stride > DEINTERLEAVE_STRIDE:
                        rows = tm // groups
                        parts = [ybuf[slab + c, pl.ds(g * rows + r, rows // DEINTERLEAVE_STRIDE,
                                                      stride=DEINTERLEAVE_STRIDE), :]
                                 for g in range(groups) for r in range(DEINTERLEAVE_STRIDE)]
                        ybuf[slab + c] = jnp.concatenate(parts, axis=0)
                        groups, stride = groups * DEINTERLEAVE_STRIDE, stride // DEINTERLEAVE_STRIDE
                    rows = tm // groups
                    for g in range(groups):
                        for r in range(stride):
                            o_ref[0, g + groups * r, :, out] = (
                                ybuf[slab + c, pl.ds(g * rows + r, rows // stride, stride=stride), :].astype(BF16))
        return finish

    def branch_b(gi, slab):
        dil = B_GROUPS[gi][1]
        lo = gi * B_GROUP_W
        parts = []
        for part, (w_ref, gain_ref) in enumerate(((wqb_ref, gqb_ref), (wkb_ref, gkb_ref), (wvb_ref, None))):
            parts.append((mm(w_ref, lo, lo + B_GROUP_W),
                          finish_b(b_refs[gi], part * B_GROUP_W, gain_ref, dil, slab)))
            slab += B_GROUP_W // LANES
        return parts, slab

    by_dilation = sorted(range(len(B_GROUPS)), key=lambda gi: -B_GROUPS[gi][1])
    stages, slab = [], 0
    for gi in by_dilation[:-1]:
        parts, slab = branch_b(gi, slab)
        stages += parts
    stages += [(mm(wqa_ref, 0, A_Q_W), finish_qa), (mm(wkva_ref, 0, 2 * A_KV_W), finish_kva),
               (mm(wqm_ref, 0, M_W), finish_qm)]
    stages += branch_b(by_dilation[-1], slab)[0]

    pending = None
    for matmul, finish in stages:
        y = matmul()
        if pending is not None:
            pending[0](pending[1])
        pending = (finish, y)
    pending[0](pending[1])


def _proj(x, g_mix, ws, gains):
    n, s, _ = x.shape
    tm = PROJ_ROWS
    cur = lambda b, i: (b, i, 0)
    plane = lambda b, i: (b, 0, i, 0)
    in_specs = ([pl.BlockSpec((1, tm, D_MODEL), cur), _resident((1, D_MODEL))]
                + [_resident(w.shape) for w in ws]
                + [_resident((1, LANES)) for _ in gains])
    out_specs = [pl.BlockSpec((1, tm, AM_W), cur)]
    out_shape = [jax.ShapeDtypeStruct((n, s, AM_W), BF16)]
    for _, dil in B_GROUPS:
        out_specs.append(pl.BlockSpec((1, dil, tm // dil, B_QKV_W), plane))
        out_shape.append(jax.ShapeDtypeStruct((n, dil, s // dil, B_QKV_W), BF16))
    return pl.pallas_call(
        _proj_kernel,
        grid=(n, s // tm),
        in_specs=in_specs,
        out_specs=out_specs,
        out_shape=out_shape,
        scratch_shapes=[pltpu.VMEM((N_DILATED_SLABS, tm, LANES), F32)],
        compiler_params=_params(2),
        name="proj",
    )(x, g_mix, *ws, *gains)


MEM_ROWS = 1024


def _mem_kv_kernel(m_ref, g_ref, w_ref, gk_ref, k_ref, v_ref):
    h = _rms_rows(m_ref[...], g_ref[...]).astype(BF16)
    y = jnp.dot(h, w_ref[...], preferred_element_type=F32)
    for c in range(M_HEADS):
        sl = slice(c * LANES, (c + 1) * LANES)
        k_ref[:, sl] = _head_norm128(y[:, sl], gk_ref[...]).astype(BF16)
    v_ref[...] = y[:, M_W:].astype(BF16)


def _mem_kv(mem2d, g_mem, w_mem_kv, gk_m):
    t = mem2d.shape[0]
    tm = min(t, MEM_ROWS)
    row = lambda i: (i, 0)
    return pl.pallas_call(
        _mem_kv_kernel,
        grid=(t // tm,),
        in_specs=[pl.BlockSpec((tm, D_MODEL), row), _resident((1, D_MODEL)),
                  _resident(w_mem_kv.shape), _resident((1, LANES))],
        out_specs=[pl.BlockSpec((tm, M_W), row)] * 2,
        out_shape=[jax.ShapeDtypeStruct((t, M_W), BF16)] * 2,
        compiler_params=_params(1),
        name="mem_kv",
    )(mem2d, g_mem, w_mem_kv, gk_m)


A_BLOCK = A_HALF_WIN
A_STEP = 2048
ATTN_CHUNK = 1024
INTERIOR, FIRST, LAST = 0, 1, 2
LOG2E = float(np.log2(np.e))
LN2 = float(np.log(2.0))

SOFTMAX_SHIFT_LIMIT = 40.0
BF16_SLACK = 1.0 + 2.0 ** -6


def _logit_bound(gain_q, gain_k, head_dim):
    return head_dim * jnp.max(jnp.abs(gain_q)) * jnp.max(jnp.abs(gain_k)) * BF16_SLACK


def _softmax_ctl(shifts):
    shifts = jnp.asarray(shifts, F32).reshape(-1)
    fixed = jnp.all(shifts <= SOFTMAX_SHIFT_LIMIT)
    return jnp.concatenate([fixed.astype(F32).reshape(1), shifts])


def _by_softmax_shift(ctl_ref, run):
    fixed = ctl_ref[0] > 0.5

    @pl.when(fixed)
    def _():
        run(True)

    @pl.when(jnp.logical_not(fixed))
    def _():
        run(False)


def _attn_a_init_bias(bias_ref, ctl_ref):
    n_keys = 3 * A_BLOCK
    q_row = lax.broadcasted_iota(jnp.int32, (A_BLOCK, n_keys), 0)
    k_col = lax.broadcasted_iota(jnp.int32, (A_BLOCK, n_keys), 1)
    dist = jnp.abs(q_row + A_BLOCK - k_col)
    dist_f = dist.astype(F32)
    for head, slope in enumerate(_alibi_slopes(A_HEADS)):
        shift = jnp.where(ctl_ref[0] > 0.5, ctl_ref[1 + head], 0.0)
        base = jnp.where(dist <= A_HALF_WIN, (-slope * LOG2E) * dist_f - shift, NEG_INF)
        bias_ref[INTERIOR, head] = base
        bias_ref[FIRST, head] = jnp.where(k_col >= A_BLOCK, base, NEG_INF)
        bias_ref[LAST, head] = jnp.where(k_col < 2 * A_BLOCK, base, NEG_INF)


def _attn_a_load_kv(kv_refs, kbuf, vbuf):
    kp_ref, kc_ref, kn_ref, vp_ref, vc_ref, vn_ref = kv_refs
    rows = kc_ref.shape[1]
    for buf, prev, cur, nxt in ((kbuf, kp_ref, kc_ref, kn_ref), (vbuf, vp_ref, vc_ref, vn_ref)):
        buf[0:A_BLOCK, :] = prev[0]
        buf[A_BLOCK:A_BLOCK + rows, :] = cur[0]
        buf[A_BLOCK + rows:, :] = nxt[0]


def _attn_a_pass(row0, first_block, n_blocks, fixed_shift, q_ref, kbuf, vbuf, bias_ref, sink_ref, ctl_ref,
                 o_ref):
    n_keys = 3 * A_BLOCK
    low = lax.broadcasted_iota(jnp.int32, (n_keys, LANES), 1) < A_HEAD_DIM
    low_q = lax.broadcasted_iota(jnp.int32, (A_BLOCK, LANES), 1) < A_HEAD_DIM
    zero = jnp.zeros((n_keys, LANES), BF16)
    nt = (((1,), (1,)), ((), ()))
    per_chunk = ATTN_CHUNK // A_BLOCK
    units = [(j, kvh) for j in range(per_chunk) for kvh in range(A_KV_HEADS)]
    logits, values, probs = {}, {}, {}

    def swap_halves(t):
        return jnp.concatenate([t[:, A_HEAD_DIM:], t[:, :A_HEAD_DIM]], axis=1)

    def padded(t, t_sw, kvh):
        lo, hi = (t, t_sw) if kvh == 0 else (t_sw, t)
        return jnp.concatenate([jnp.where(low, lo, zero), jnp.where(low, zero, hi)], axis=0)

    def logit_matmuls():
        for j in range(per_chunk):
            rows = pl.ds(row0 + j * A_BLOCK, A_BLOCK)
            kw = kbuf[pl.ds(row0 + j * A_BLOCK, n_keys), :]
            vw = vbuf[pl.ds(row0 + j * A_BLOCK, n_keys), :]
            kw_sw, vw_sw = swap_halves(kw), swap_halves(vw)
            for kvh in range(A_KV_HEADS):
                q2 = jnp.concatenate([q_ref[0, rows, (2 * kvh + c) * LANES:(2 * kvh + c + 1) * LANES]
                                      for c in range(2)], axis=0)
                logits[j, kvh] = lax.dot_general(q2, padded(kw, kw_sw, kvh), nt,
                                                 preferred_element_type=F32)
                values[j, kvh] = padded(vw, vw_sw, kvh)

    def softmaxes():
        for j, kvh in units:
            block = first_block + j
            variant = jnp.where(block == 0, FIRST, jnp.where(block == n_blocks - 1, LAST, INTERIOR))
            s2 = logits[j, kvh]
            p_rows, r_rows = [], []
            for c in range(2):
                ps, sums, sink_gaps = [], [], []
                for half in range(2):
                    head = 2 * (2 * kvh + c) + half
                    sh = s2[c * A_BLOCK:(c + 1) * A_BLOCK, half * n_keys:(half + 1) * n_keys]
                    sh = sh + bias_ref[variant, head]
                    sink = sink_ref[head] * LOG2E
                    if fixed_shift:
                        m = ctl_ref[1 + head]
                        p = jnp.exp2(sh)
                    else:
                        m = jnp.maximum(jnp.max(sh, axis=-1, keepdims=True), sink)
                        p = jnp.exp2(sh - m)
                    ps.append(p.astype(BF16))
                    sums.append(jnp.sum(p, axis=-1, keepdims=True))
                    sink_gaps.append(sink - m)
                sink_lanes = low_q[:1] if fixed_shift else low_q
                den = (jnp.where(low_q, sums[0], sums[1])
                       + jnp.exp2(jnp.where(sink_lanes, sink_gaps[0], sink_gaps[1])))
                p_rows.append(jnp.concatenate(ps, axis=1))
                r_rows.append(1.0 / den)
            probs[j, kvh] = (jnp.concatenate(p_rows, axis=0), jnp.concatenate(r_rows, axis=0))

    def value_matmuls():
        for j, kvh in units:
            p2, r2 = probs[j, kvh]
            o2 = jnp.dot(p2, values[j, kvh], preferred_element_type=F32) * r2
            for c in range(2):
                tile = 2 * kvh + c
                o_ref[0, pl.ds(row0 + j * A_BLOCK, A_BLOCK), tile * LANES:(tile + 1) * LANES] = (
                    o2[c * A_BLOCK:(c + 1) * A_BLOCK].astype(BF16))

    return logit_matmuls, softmaxes, value_matmuls


def _attn_a_kernel(seq_len, q_ref, kp_ref, kc_ref, kn_ref, vp_ref, vc_ref, vn_ref, sink_ref, ctl_ref,
                   o_ref, kbuf, vbuf, bias_ref):
    i = pl.program_id(1)
    step_rows = q_ref.shape[1]

    @pl.when((pl.program_id(0) == 0) & (i == 0))
    def _():
        _attn_a_init_bias(bias_ref, ctl_ref)

    _attn_a_load_kv((kp_ref, kc_ref, kn_ref, vp_ref, vc_ref, vn_ref), kbuf, vbuf)

    def run(fixed_shift):
        def chunk(t, carry):
            row0 = pl.multiple_of(t * ATTN_CHUNK, ATTN_CHUNK)
            for phase in _attn_a_pass(row0, (i * step_rows + row0) // A_BLOCK, seq_len // A_BLOCK,
                                      fixed_shift, q_ref, kbuf, vbuf, bias_ref, sink_ref, ctl_ref, o_ref):
                phase()
            return carry

        lax.fori_loop(0, step_rows // ATTN_CHUNK, chunk, 0)

    _by_softmax_shift(ctl_ref, run)


def _attn_a_specs(s, step_rows, position):
    per_step = step_rows // A_BLOCK
    n_blocks = s // A_BLOCK
    assert n_blocks >= 2 and s % step_rows == 0

    def cur(col):
        def index(*ids):
            b, i = position(*ids)
            return (b, i, col)
        return index

    def prev(col):
        def index(*ids):
            b, i = position(*ids)
            return (b, jnp.maximum(i * per_step - 1, 0), col)
        return index

    def nxt(col):
        def index(*ids):
            b, i = position(*ids)
            return (b, jnp.minimum((i + 1) * per_step, n_blocks - 1), col)
        return index

    q_spec = pl.BlockSpec((1, step_rows, A_Q_W), cur(AM_QA // A_Q_W))
    o_spec = pl.BlockSpec((1, step_rows, A_Q_W), cur(0))
    kv_specs = [[pl.BlockSpec((1, A_BLOCK, A_KV_W), prev(col // A_KV_W)),
                 pl.BlockSpec((1, step_rows, A_KV_W), cur(col // A_KV_W)),
                 pl.BlockSpec((1, A_BLOCK, A_KV_W), nxt(col // A_KV_W))] for col in (AM_KA, AM_VA)]
    scratch = [pltpu.VMEM((step_rows + 2 * A_BLOCK, A_KV_W), BF16)] * 2 + [
        pltpu.VMEM((3, A_HEADS, A_BLOCK, 3 * A_BLOCK), F32)]
    return q_spec, o_spec, kv_specs, scratch


def _attn_a(am, sink, logit_bound):
    n, s, _ = am.shape
    q_spec, o_spec, kv_specs, scratch = _attn_a_specs(s, A_STEP, lambda b, i: (b, i))
    smem = pl.BlockSpec(memory_space=pltpu.SMEM)
    ctl = _softmax_ctl(jnp.maximum(logit_bound, sink * LOG2E))
    return pl.pallas_call(
        functools.partial(_attn_a_kernel, s),
        grid=(n, s // A_STEP),
        in_specs=[q_spec] + kv_specs[0] + kv_specs[1] + [smem, smem],
        out_specs=o_spec,
        out_shape=jax.ShapeDtypeStruct((n, s, A_Q_W), BF16),
        scratch_shapes=scratch,
        compiler_params=_params(2),
        name="attn_a",
    )(*[am] * 7, sink, ctl)


B_BLOCK = 2 * B_HALF_WIN
B_KEYS = B_BLOCK + 2 * B_HALF_WIN
B_ROWS_PER_STEP = 2048


def _attn_b_kernel(sub_len, slopes, q_ref, kp_ref, kc_ref, kn_ref, vp_ref, vc_ref, vn_ref, ctl_ref,
                   o_ref, lse_ref, kbuf, vbuf, bias_ref):
    i = pl.program_id(2)
    n_planes, step = q_ref.shape[1], q_ref.shape[2]
    n_blocks = sub_len // B_BLOCK

    @pl.when((pl.program_id(0) == 0) & (pl.program_id(1) == 0) & (i == 0))
    def _():
        shift = jnp.where(ctl_ref[0] > 0.5, ctl_ref[1], 0.0)
        q_row = lax.broadcasted_iota(jnp.int32, (B_BLOCK, B_KEYS), 0)
        k_col = lax.broadcasted_iota(jnp.int32, (B_BLOCK, B_KEYS), 1)
        dist = jnp.abs(q_row + B_HALF_WIN - k_col)
        dist_f = dist.astype(F32)
        for head, slope in enumerate(slopes):
            base = jnp.where(dist <= B_HALF_WIN, (-slope * LOG2E) * dist_f - shift, NEG_INF)
            first = jnp.where(k_col >= B_HALF_WIN, base, NEG_INF)
            bias_ref[0, head] = base
            bias_ref[1, head] = first
            bias_ref[2, head] = jnp.where(k_col < B_HALF_WIN + B_BLOCK, base, NEG_INF)
            bias_ref[3, head] = jnp.where(k_col < B_HALF_WIN + B_BLOCK, first, NEG_INF)

    for buf, prev, cur, nxt in ((kbuf, kp_ref, kc_ref, kn_ref), (vbuf, vp_ref, vc_ref, vn_ref)):
        for r in range(n_planes):
            buf[r, 0:B_HALF_WIN, :] = prev[0, r]
            buf[r, B_HALF_WIN:B_HALF_WIN + step, :] = cur[0, r]
            buf[r, B_HALF_WIN + step:, :] = nxt[0, r]

    lane = lax.broadcasted_iota(jnp.int32, (B_BLOCK, LANES), 1)
    lanes_per_head = LANES // B_HEADS_PER_GROUP
    heads = [slice(h * B_HEAD_DIM, (h + 1) * B_HEAD_DIM) for h in range(B_HEADS_PER_GROUP)]
    nt = (((1,), (1,)), ((), ()))
    rows_per_pass = min(step, ATTN_CHUNK)
    planes_per_pass = ATTN_CHUNK // rows_per_pass
    passes_per_plane = step // rows_per_pass
    blocks = [(dr, j) for dr in range(planes_per_pass) for j in range(rows_per_pass // B_BLOCK)]

    def one_pass(fixed_shift, t, carry):
        if planes_per_pass == 1:
            plane0 = t // passes_per_plane
            row0 = pl.multiple_of((t % passes_per_plane) * ATTN_CHUNK, ATTN_CHUNK)
        else:
            plane0, row0 = t * planes_per_pass, 0

        logits = {}
        for dr, j in blocks:
            for h, sl in enumerate(heads):
                q = q_ref[0, plane0 + dr, pl.ds(row0 + j * B_BLOCK, B_BLOCK), sl]
                k = kbuf[plane0 + dr, pl.ds(row0 + j * B_BLOCK, B_KEYS), sl]
                logits[dr, j, h] = lax.dot_general(q, k, nt, preferred_element_type=F32)

        probs = {}
        for dr, j in blocks:
            block = (i * step + row0) // B_BLOCK + j
            variant = (block == 0).astype(jnp.int32) + 2 * (block == n_blocks - 1).astype(jnp.int32)
            m_tile = den_tile = None
            for h in range(B_HEADS_PER_GROUP):
                s = logits[dr, j, h] + bias_ref[variant, h]
                if fixed_shift:
                    m = ctl_ref[1]
                    p = jnp.exp2(s)
                else:
                    m = jnp.max(s, axis=-1, keepdims=True)
                    p = jnp.exp2(s - m)
                den = jnp.sum(p, axis=-1, keepdims=True)
                probs[dr, j, h] = (p.astype(BF16), 1.0 / den)
                den_tile = den if h == 0 else jnp.where(lane >= h * lanes_per_head, den, den_tile)
                if not fixed_shift:
                    m_tile = m if h == 0 else jnp.where(lane >= h * lanes_per_head, m, m_tile)
            lse_ref[0, plane0 + dr, pl.ds(row0 + j * B_BLOCK, B_BLOCK), :] = (
                ((m if fixed_shift else m_tile) + jnp.log2(den_tile)) * LN2)

        for dr, j in blocks:
            for h, sl in enumerate(heads):
                p, rden = probs[dr, j, h]
                v = vbuf[plane0 + dr, pl.ds(row0 + j * B_BLOCK, B_KEYS), sl]
                o = jnp.dot(p, v, preferred_element_type=F32) * rden
                o_ref[0, plane0 + dr, pl.ds(row0 + j * B_BLOCK, B_BLOCK), sl] = o.astype(BF16)
        return carry

    _by_softmax_shift(ctl_ref, lambda fixed_shift: lax.fori_loop(
        0, n_planes * step // ATTN_CHUNK, functools.partial(one_pass, fixed_shift), 0))


def _attn_b_group(qkv, gi, logit_bound):
    n, dil, sub, _ = qkv.shape
    step = min(sub, B_ROWS_PER_STEP)
    n_planes = B_ROWS_PER_STEP // step
    assert sub % step == 0 and dil % n_planes == 0 and step % B_BLOCK == 0
    assert ATTN_CHUNK % step == 0 or step % ATTN_CHUNK == 0
    halo_per_step = step // B_HALF_WIN
    last_halo = sub // B_HALF_WIN - 1
    cur = lambda col: lambda b, r, i: (b, r, i, col)
    prev = lambda col: lambda b, r, i: (b, r, jnp.maximum(i * halo_per_step - 1, 0), col)
    nxt = lambda col: lambda b, r, i: (b, r, jnp.minimum((i + 1) * halo_per_step, last_halo), col)
    kv_specs = [[pl.BlockSpec((1, n_planes, B_HALF_WIN, B_GROUP_W), prev(col)),
                 pl.BlockSpec((1, n_planes, step, B_GROUP_W), cur(col)),
                 pl.BlockSpec((1, n_planes, B_HALF_WIN, B_GROUP_W), nxt(col))] for col in (1, 2)]
    all_slopes = _alibi_slopes(len(B_GROUPS) * B_HEADS_PER_GROUP)
    slopes = [all_slopes[gi * B_HEADS_PER_GROUP + h] * dil for h in range(B_HEADS_PER_GROUP)]
    return pl.pallas_call(
        functools.partial(_attn_b_kernel, sub, slopes),
        grid=(n, dil // n_planes, sub // step),
        in_specs=[pl.BlockSpec((1, n_planes, step, B_GROUP_W), cur(0))] + kv_specs[0] + kv_specs[1]
                 + [pl.BlockSpec(memory_space=pltpu.SMEM)],
        out_specs=[pl.BlockSpec((1, n_planes, step, B_GROUP_W), cur(0)),
                   pl.BlockSpec((1, n_planes, step, LANES), cur(0))],
        out_shape=[jax.ShapeDtypeStruct((n, dil, sub, B_GROUP_W), BF16),
                   jax.ShapeDtypeStruct((n, dil, sub, LANES), F32)],
        scratch_shapes=[pltpu.VMEM((n_planes, step + 2 * B_HALF_WIN, B_GROUP_W), BF16)] * 2
                       + [pltpu.VMEM((4, B_HEADS_PER_GROUP, B_BLOCK, B_KEYS), F32)],
        compiler_params=_params(3),
        name=f"attn_b{gi}",
    )(*[qkv] * 7, _softmax_ctl(logit_bound))


M_STEP = 2048


def _attn_m_kernel(q_ref, k_ref, v_ref, ctl_ref, o_ref):
    heads = [slice(h * M_HEAD_DIM, (h + 1) * M_HEAD_DIM) for h in range(M_HEADS)]
    nt = (((1,), (1,)), ((), ()))

    def chunk(fixed_shift, t, carry):
        rows = pl.ds(pl.multiple_of(t * ATTN_CHUNK, ATTN_CHUNK), ATTN_CHUNK)
        logits = [lax.dot_general(q_ref[0, rows, sl], k_ref[0, :, sl], nt, preferred_element_type=F32)
                  for sl in heads]
        probs = []
        for s in logits:
            m = ctl_ref[1] if fixed_shift else jnp.max(s, axis=-1, keepdims=True)
            p = jnp.exp2(s - m)
            probs.append((p.astype(BF16), 1.0 / jnp.sum(p, axis=-1, keepdims=True)))
        for sl, (p, rden) in zip(heads, probs):
            o = jnp.dot(p, v_ref[0, :, sl], preferred_element_type=F32) * rden
            o_ref[0, rows, sl] = o.astype(BF16)
        return carry

    _by_softmax_shift(ctl_ref, lambda fixed_shift: lax.fori_loop(
        0, M_STEP // ATTN_CHUNK, functools.partial(chunk, fixed_shift), 0))


def _attn_m(am, mk, mv, logit_bound):
    n, s, _ = am.shape
    cur = lambda b, i: (b, i, 0)
    q_cols = lambda b, i: (b, i, AM_QM // M_W)
    mem = lambda b, i: (b, 0, 0)
    return pl.pallas_call(
        _attn_m_kernel,
        grid=(n, s // M_STEP),
        in_specs=[pl.BlockSpec((1, M_STEP, M_W), q_cols), pl.BlockSpec((1, N_MEM, M_W), mem),
                  pl.BlockSpec((1, N_MEM, M_W), mem), pl.BlockSpec(memory_space=pltpu.SMEM)],
        out_specs=pl.BlockSpec((1, M_STEP, M_W), cur),
        out_shape=jax.ShapeDtypeStruct((n, s, M_W), BF16),
        compiler_params=_params(2),
        name="attn_m",
    )(am, mk, mv, _softmax_ctl(logit_bound))


MERGE_ROWS = 512
assert all(dil <= DEINTERLEAVE_STRIDE ** 2 for _, dil in B_GROUPS)
N_MERGE_SLABS = sum((B_GROUP_W // LANES + 1) * (1 if dil <= DEINTERLEAVE_STRIDE else 2)
                    for _, dil in B_GROUPS if dil > 1)


def _merge_kernel(x_ref, g_ref, oa_ref, o0_ref, o1_ref, o2_ref, l0_ref, l1_ref, l2_ref, om_ref,
                  wg_ref, bg_ref, wbr_ref, wout_ref, y_ref, nat):
    tm = x_ref.shape[1]
    x = x_ref[0]
    h = _rms_rows(x, g_ref[...]).astype(BF16)
    tiles = B_GROUP_W // LANES

    o_nat, lse_nat = [], []
    slab = 0
    for (_, dil), o_ref, l_ref in zip(B_GROUPS, (o0_ref, o1_ref, o2_ref), (l0_ref, l1_ref, l2_ref)):
        if dil == 1:
            o_nat.append([o_ref[0, 0, :, c * LANES:(c + 1) * LANES].astype(F32) for c in range(tiles)])
            lse_nat.append(l_ref[0, 0])
            continue
        def plane_tile(r, c):
            if c == tiles:
                return l_ref[0, r]
            return o_ref[0, r, :, c * LANES:(c + 1) * LANES].astype(F32)

        for c in range(tiles + 1):
            if dil <= DEINTERLEAVE_STRIDE:
                for r in range(dil):
                    nat[slab + c, pl.ds(r, tm // dil, stride=dil), :] = plane_tile(r, c)
                continue
            groups = DEINTERLEAVE_STRIDE
            inner, rows = dil // groups, tm // groups
            spare = slab + tiles + 1 + c
            for g in range(groups):
                for r in range(inner):
                    nat[spare, pl.ds(g * rows + r, rows // inner, stride=inner), :] = plane_tile(g + groups * r, c)
            for g in range(groups):
                nat[slab + c, pl.ds(g, rows, stride=groups), :] = nat[spare, g * rows:(g + 1) * rows, :]
        o_nat.append([nat[slab + c] for c in range(tiles)])
        lse_nat.append(nat[slab + tiles])
        slab += (tiles + 1) * (1 if dil <= DEINTERLEAVE_STRIDE else 2)

    top = jnp.maximum(jnp.maximum(lse_nat[0], lse_nat[1]), lse_nat[2])
    es = [jnp.exp(l - top) for l in lse_nat]
    r_sum = 1.0 / (es[0] + es[1] + es[2])
    ws = [e * r_sum for e in es]
    lanes_per_head = LANES // B_HEADS_PER_GROUP
    ob_heads = []
    for hd in range(B_HEADS_PER_GROUP):
        acc = None
        for w, o in zip(ws, o_nat):
            term = w[:, hd * lanes_per_head:hd * lanes_per_head + 1] * o[hd]
            acc = term if acc is None else acc + term
        ob_heads.append(acc.astype(BF16))
    ob = jnp.concatenate(ob_heads, axis=1)

    z = None
    for b, o in enumerate((oa_ref[0], ob, om_ref[0])):
        logits = jnp.dot(h, wg_ref[:, b * D_MODEL:(b + 1) * D_MODEL], preferred_element_type=F32)
        gate = jax.nn.sigmoid(logits + bg_ref[b:b + 1, :])
        term = gate * jnp.dot(o, wbr_ref[b], preferred_element_type=F32)
        z = term if z is None else z + term
    y_ref[0] = x + jnp.dot(z.astype(BF16), wout_ref[...], preferred_element_type=F32)


def _merge(x, g_mix, oa, obs, lses, om, w_gate, b_gate, w_branch, w_out):
    n, s, _ = x.shape
    tm = MERGE_ROWS
    cur = lambda b, i: (b, i, 0)
    plane = lambda b, i: (b, 0, i, 0)
    rows = lambda w: pl.BlockSpec((1, tm, w), cur)
    planes = lambda w: [pl.BlockSpec((1, dil, tm // dil, w), plane) for _, dil in B_GROUPS]
    return pl.pallas_call(
        _merge_kernel,
        grid=(n, s // tm),
        in_specs=[rows(D_MODEL), _resident((1, D_MODEL)), rows(A_Q_W)]
                 + planes(B_GROUP_W) + planes(LANES) + [rows(M_W)]
                 + [_resident(w_gate.shape), _resident(b_gate.shape), _resident(w_branch.shape),
                    _resident(w_out.shape)],
        out_specs=rows(D_MODEL),
        out_shape=jax.ShapeDtypeStruct((n, s, D_MODEL), F32),
        scratch_shapes=[pltpu.VMEM((N_MERGE_SLABS, tm, LANES), F32)],
        compiler_params=_params(2),
        name="merge",
    )(x, g_mix, oa, *obs, *lses, om, w_gate, b_gate, w_branch, w_out)


MLP_ROWS = 1024
MLP_FF_CHUNK = 1024


def _mlp_kernel(x_ref, g_ref, wup_ref, wdn_ref, y_ref):
    x = x_ref[...]
    h = _rms_rows(x, g_ref[...]).astype(BF16)
    acc = x
    for lo in range(0, D_FF, MLP_FF_CHUNK):
        u = jnp.dot(h, wup_ref[:, lo:lo + MLP_FF_CHUNK], preferred_element_type=F32)
        a = jnp.square(jnp.maximum(u, 0.0)).astype(BF16)
        acc = acc + jnp.dot(a, wdn_ref[lo:lo + MLP_FF_CHUNK, :], preferred_element_type=F32)
    y_ref[...] = acc


def _mlp(x2d, g_mlp, w_up, w_down):
    t = x2d.shape[0]
    tm = MLP_ROWS
    row = lambda i: (i, 0)
    return pl.pallas_call(
        _mlp_kernel,
        grid=(t // tm,),
        in_specs=[pl.BlockSpec((tm, D_MODEL), row), _resident((1, D_MODEL)),
                  _resident(w_up.shape), _resident(w_down.shape)],
        out_specs=pl.BlockSpec((tm, D_MODEL), row),
        out_shape=jax.ShapeDtypeStruct((t, D_MODEL), F32),
        compiler_params=_params(1),
        name="mlp",
    )(x2d, g_mlp, w_up, w_down)


def _tile2(g):
    return jnp.concatenate([g, g]).reshape(1, LANES)


def _row(g):
    return g.reshape(1, -1)


def _layer(x, mem, g_mix, g_mem, w_in, b_gate, w_mem_kv, gq_a, gk_a, sink_a, gq_b, gk_b, gq_m, gk_m,
           w_branch, w_out, g_mlp, w_up, w_down):
    n, s, _ = x.shape
    bounds = np.cumsum((0, A_Q_W, A_KV_W, A_KV_W, B_W, B_W, B_W, M_W, GATE_W))
    seg = lambda a, b: w_in[:, int(bounds[a]):int(bounds[b])].astype(BF16)
    ws = (seg(0, 1), seg(1, 3), seg(3, 4), seg(4, 5), seg(5, 6), seg(6, 7))
    w_gate = seg(7, 8)
    gq_a, gq_b, gq_m = (gq_a * (A_HEAD_DIM ** -0.5 * LOG2E), gq_b * (B_HEAD_DIM ** -0.5 * LOG2E),
                        gq_m * (M_HEAD_DIM ** -0.5 * LOG2E))
    gains = (_tile2(gq_a), _tile2(gk_a), _row(gq_b), _row(gk_b), _row(gq_m))

    am, *qkv_b = _proj(x, _row(g_mix), ws, gains)
    mk, mv = _mem_kv(mem.reshape(n * N_MEM, D_MODEL), _row(g_mem), w_mem_kv.astype(BF16), _row(gk_m))

    oa = _attn_a(am, sink_a, _logit_bound(gq_a, gk_a, A_HEAD_DIM))
    bound_b = _logit_bound(gq_b, gk_b, B_HEAD_DIM)
    b_out = [_attn_b_group(qkv_b[gi], gi, bound_b) for gi in range(len(B_GROUPS))]
    om = _attn_m(am, mk.reshape(n, N_MEM, M_W), mv.reshape(n, N_MEM, M_W),
                 _logit_bound(gq_m, gk_m, M_HEAD_DIM))

    x1 = _merge(x, _row(g_mix), oa, [o for o, _ in b_out], [l for _, l in b_out], om,
                w_gate, b_gate, w_branch.astype(BF16), w_out.astype(BF16))
    y = _mlp(x1.reshape(n * s, D_MODEL), _row(g_mlp), w_up.astype(BF16), w_down.astype(BF16))
    return y.reshape(n, s, D_MODEL)


def kernel(x_prompt, x_sample, mem_prompt, mem_sample, g_mix, g_mem, w_in, b_gate, w_mem_kv, gq_a, gk_a,
           sink_a, gq_b, gk_b, gq_m, gk_m, w_branch, w_out, g_mlp, w_up, w_down):
    depth = w_in.shape[0]

    def run(x, mem):
        for l in range(depth):
            x = _layer(x, mem, g_mix[l], g_mem[l], w_in[l], b_gate[l], w_mem_kv[l], gq_a[l], gk_a[l],
                       sink_a[l], gq_b[l], gk_b[l], gq_m[l], gk_m[l], w_branch[l], w_out[l], g_mlp[l],
                       w_up[l], w_down[l])
        return x

    return (run(x_prompt, mem_prompt), run(x_sample, mem_sample))
```

```python
import functools

import numpy as np
import jax
import jax.numpy as jnp
from jax import lax
from jax.experimental import pallas as pl
from jax.experimental.pallas import tpu as pltpu

D_MODEL = 1024
N_MEM = 256
A_HEADS = 8
A_KV_HEADS = 2
A_HEAD_DIM = 64
A_HALF_WIN = 128
B_GROUPS = ((128, 1), (512, 4), (2048, 16))
B_HEADS_PER_GROUP = 4
B_HEAD_DIM = 128
M_HEADS = 4
M_HEAD_DIM = 128
N_BRANCH = 3
BRANCH_WIDTH = D_MODEL // 2
D_FF = 4 * D_MODEL
EPS = 1e-6
NEG_INF = -1e30

A_Q_W = A_HEADS * A_HEAD_DIM
A_KV_W = A_KV_HEADS * A_HEAD_DIM
B_GROUP_W = B_HEADS_PER_GROUP * B_HEAD_DIM
B_W = len(B_GROUPS) * B_GROUP_W
M_W = M_HEADS * M_HEAD_DIM
GATE_W = N_BRANCH * D_MODEL

LANES = 128
B_HALF_WIN = 64
VMEM_LIMIT_BYTES = 56 * 1024 * 1024

BF16 = jnp.bfloat16
F32 = jnp.float32


def _alibi_slopes(n):
    return [float(2.0 ** (-8.0 * (i + 1) / n)) for i in range(n)]


def _params(n_grid_axes):
    return pltpu.CompilerParams(
        dimension_semantics=("arbitrary",) * n_grid_axes,
        vmem_limit_bytes=VMEM_LIMIT_BYTES)


def _resident(shape):
    zeros = (0,) * len(shape)
    return pl.BlockSpec(shape, lambda *_: zeros, pipeline_mode=pl.Buffered(1))


def _rms_rows(x, gain):
    ms = jnp.mean(x * x, axis=-1, keepdims=True)
    return x * lax.rsqrt(ms + EPS) * gain


def _head_norm128(blk, gain):
    ms = jnp.sum(blk * blk, axis=-1, keepdims=True) * (1.0 / LANES)
    return blk * lax.rsqrt(ms + EPS) * gain


def _head_norm64(blk, gain2):
    low = lax.broadcasted_iota(jnp.int32, blk.shape, 1) < A_HEAD_DIM
    sq = blk * blk
    ss_lo = jnp.sum(jnp.where(low, sq, 0.0), axis=-1, keepdims=True)
    ss_hi = jnp.sum(jnp.where(low, 0.0, sq), axis=-1, keepdims=True)
    ms = jnp.where(low, ss_lo, ss_hi) * (1.0 / A_HEAD_DIM)
    return blk * lax.rsqrt(ms + EPS) * gain2


PROJ_ROWS = 1024
AM_QA, AM_QM, AM_KA, AM_VA = 0, A_Q_W, A_Q_W + M_W, A_Q_W + M_W + A_KV_W
AM_W = A_Q_W + M_W + 2 * A_KV_W
B_QKV_W = 3 * B_GROUP_W
DEINTERLEAVE_STRIDE = 4
N_DILATED_SLABS = 2 * (B_GROUP_W // LANES)


def _proj_kernel(x_ref, g_ref, wqa_ref, wkva_ref, wqb_ref, wkb_ref, wvb_ref, wqm_ref,
                 gqa_ref, gka_ref, gqb_ref, gkb_ref, gqm_ref,
                 am_ref, *rest):
    b_refs, ybuf = rest[:-1], rest[-1]
    tm = x_ref.shape[1]
    h = _rms_rows(x_ref[0], g_ref[...]).astype(BF16)

    def mm(w_ref, lo, hi):
        return lambda: jnp.dot(h, w_ref[:, lo:hi], preferred_element_type=F32)

    def finish_qa(y):
        for c in range(A_Q_W // LANES):
            sl = slice(c * LANES, (c + 1) * LANES)
            am_ref[0, :, AM_QA + c * LANES:AM_QA + (c + 1) * LANES] = (
                _head_norm64(y[:, sl], gqa_ref[...]).astype(BF16))

    def finish_kva(y):
        am_ref[0, :, AM_KA:AM_KA + A_KV_W] = _head_norm64(y[:, :A_KV_W], gka_ref[...]).astype(BF16)
        am_ref[0, :, AM_VA:AM_VA + A_KV_W] = y[:, A_KV_W:].astype(BF16)

    def finish_qm(y):
        for c in range(M_W // LANES):
            sl = slice(c * LANES, (c + 1) * LANES)
            am_ref[0, :, AM_QM + c * LANES:AM_QM + (c + 1) * LANES] = (
                _head_norm128(y[:, sl], gqm_ref[...]).astype(BF16))

    def finish_b(o_ref, col0, gain_ref, dil, slab):
        def finish(y):
            for c in range(B_GROUP_W // LANES):
                sl = slice(c * LANES, (c + 1) * LANES)
                out = slice(col0 + c * LANES, col0 + (c + 1) * LANES)
                blk = y[:, sl]
                if gain_ref is not None:
                    blk = _head_norm128(blk, gain_ref[...])
                if dil == 1:
                    o_ref[0, 0, :, out] = blk.astype(BF16)
                else:
                    ybuf[slab + c] = blk
                    groups, stride = 1, dil
                    while stride > DEINTERLEAVE_STRIDE:
                        rows = tm // groups
                        parts = [ybuf[slab + c, pl.ds(g * rows + r, rows // DEINTERLEAVE_STRIDE,
                                                      stride=DEINTERLEAVE_STRIDE), :]
                                 for g in range(groups) for r in range(DEINTERLEAVE_STRIDE)]
                        ybuf[slab + c] = jnp.concatenate(parts, axis=0)
                        groups, stride = groups * DEINTERLEAVE_STRIDE, stride // DEINTERLEAVE_STRIDE
                    rows = tm // groups
                    for g in range(groups):
                        for r in range(stride):
                            o_ref[0, g + groups * r, :, out] = (
                                ybuf[slab + c, pl.ds(g * rows + r, rows // stride, stride=stride), :].astype(BF16))
        return finish

    def branch_b(gi, slab):
        dil = B_GROUPS[gi][1]
        lo = gi * B_GROUP_W
        parts = []
        for part, (w_ref, gain_ref) in enumerate(((wqb_ref, gqb_ref), (wkb_ref, gkb_ref), (wvb_ref, None))):
            parts.append((mm(w_ref, lo, lo + B_GROUP_W),
                          finish_b(b_refs[gi], part * B_GROUP_W, gain_ref, dil, slab)))
            slab = (slab + B_GROUP_W // LANES) % N_DILATED_SLABS
        return parts, slab

    by_dilation = sorted(range(len(B_GROUPS)), key=lambda gi: -B_GROUPS[gi][1])
    stages, slab = [], 0
    for gi in by_dilation[:-1]:
        parts, slab = branch_b(gi, slab)
        stages += parts
    stages += [(mm(wqa_ref, 0, A_Q_W), finish_qa), (mm(wkva_ref, 0, 2 * A_KV_W), finish_kva),
               (mm(wqm_ref, 0, M_W), finish_qm)]
    stages += branch_b(by_dilation[-1], slab)[0]

    pending = None
    for matmul, finish in stages:
        y = matmul()
        if pending is not None:
            pending[0](pending[1])
        pending = (finish, y)
    pending[0](pending[1])


def _proj(x, g_mix, ws, gains):
    n, s, _ = x.shape
    tm = PROJ_ROWS
    cur = lambda b, i: (b, i, 0)
    plane = lambda b, i: (b, 0, i, 0)
    in_specs = ([pl.BlockSpec((1, tm, D_MODEL), cur), _resident((1, D_MODEL))]
                + [_resident(w.shape) for w in ws]
                + [_resident((1, LANES)) for _ in gains])
    out_specs = [pl.BlockSpec((1, tm, AM_W), cur)]
    out_shape = [jax.ShapeDtypeStruct((n, s, AM_W), BF16)]
    for _, dil in B_GROUPS:
        out_specs.append(pl.BlockSpec((1, dil, tm // dil, B_QKV_W), plane))
        out_shape.append(jax.ShapeDtypeStruct((n, dil, s // dil, B_QKV_W), BF16))
    return pl.pallas_call(
        _proj_kernel,
        grid=(n, s // tm),
        in_specs=in_specs,
        out_specs=out_specs,
        out_shape=out_shape,
        scratch_shapes=[pltpu.VMEM((N_DILATED_SLABS, tm, LANES), F32)],
        compiler_params=_params(2),
        name="proj",
    )(x, g_mix, *ws, *gains)


MEM_ROWS = 1024


def _mem_kv_kernel(m_ref, g_ref, w_ref, gk_ref, k_ref, v_ref):
    h = _rms_rows(m_ref[...], g_ref[...]).astype(BF16)
    y = jnp.dot(h, w_ref[...], preferred_element_type=F32)
    for c in range(M_HEADS):
        sl = slice(c * LANES, (c + 1) * LANES)
        k_ref[:, sl] = _head_norm128(y[:, sl], gk_ref[...]).astype(BF16)
    v_ref[...] = y[:, M_W:].astype(BF16)


def _mem_kv(mem2d, g_mem, w_mem_kv, gk_m):
    t = mem2d.shape[0]
    tm = min(t, MEM_ROWS)
    row = lambda i: (i, 0)
    return pl.pallas_call(
        _mem_kv_kernel,
        grid=(t // tm,),
        in_specs=[pl.BlockSpec((tm, D_MODEL), row), _resident((1, D_MODEL)),
                  _resident(w_mem_kv.shape), _resident((1, LANES))],
        out_specs=[pl.BlockSpec((tm, M_W), row)] * 2,
        out_shape=[jax.ShapeDtypeStruct((t, M_W), BF16)] * 2,
        compiler_params=_params(1),
        name="mem_kv",
    )(mem2d, g_mem, w_mem_kv, gk_m)


A_BLOCK = A_HALF_WIN
A_STEP = 4096
ATTN_CHUNK = 1024
INTERIOR, FIRST, LAST = 0, 1, 2
LOG2E = float(np.log2(np.e))
LN2 = float(np.log(2.0))

SOFTMAX_SHIFT_LIMIT = 40.0
BF16_SLACK = 1.0 + 2.0 ** -6


def _logit_bound(gain_q, gain_k, head_dim):
    return head_dim * jnp.max(jnp.abs(gain_q)) * jnp.max(jnp.abs(gain_k)) * BF16_SLACK


def _softmax_ctl(shifts):
    shifts = jnp.asarray(shifts, F32).reshape(-1)
    fixed = jnp.all(shifts <= SOFTMAX_SHIFT_LIMIT)
    return jnp.concatenate([fixed.astype(F32).reshape(1), shifts])


def _by_softmax_shift(ctl_ref, run):
    fixed = ctl_ref[0] > 0.5

    @pl.when(fixed)
    def _():
        run(True)

    @pl.when(jnp.logical_not(fixed))
    def _():
        run(False)


def _attn_a_init_bias(bias_ref, ctl_ref):
    n_keys = 3 * A_BLOCK
    q_row = lax.broadcasted_iota(jnp.int32, (A_BLOCK, n_keys), 0)
    k_col = lax.broadcasted_iota(jnp.int32, (A_BLOCK, n_keys), 1)
    dist = jnp.abs(q_row + A_BLOCK - k_col)
    dist_f = dist.astype(F32)
    for head, slope in enumerate(_alibi_slopes(A_HEADS)):
        shift = jnp.where(ctl_ref[0] > 0.5, ctl_ref[1 + head], 0.0)
        base = jnp.where(dist <= A_HALF_WIN, (-slope * LOG2E) * dist_f - shift, NEG_INF)
        bias_ref[INTERIOR, head] = base
        bias_ref[FIRST, head] = jnp.where(k_col >= A_BLOCK, base, NEG_INF)
        bias_ref[LAST, head] = jnp.where(k_col < 2 * A_BLOCK, base, NEG_INF)


def _attn_a_load_kv(kv_refs, kbuf, vbuf):
    kp_ref, kc_ref, kn_ref, vp_ref, vc_ref, vn_ref = kv_refs
    rows = kc_ref.shape[1]
    for buf, prev, cur, nxt in ((kbuf, kp_ref, kc_ref, kn_ref), (vbuf, vp_ref, vc_ref, vn_ref)):
        buf[0:A_BLOCK, :] = prev[0]
        buf[A_BLOCK:A_BLOCK + rows, :] = cur[0]
        buf[A_BLOCK + rows:, :] = nxt[0]


def _attn_a_pass(row0, first_block, n_blocks, fixed_shift, q_ref, kbuf, vbuf, bias_ref, sink_ref, ctl_ref,
                 o_ref):
    n_keys = 3 * A_BLOCK
    low = lax.broadcasted_iota(jnp.int32, (n_keys, LANES), 1) < A_HEAD_DIM
    low_q = lax.broadcasted_iota(jnp.int32, (A_BLOCK, LANES), 1) < A_HEAD_DIM
    zero = jnp.zeros((n_keys, LANES), BF16)
    nt = (((1,), (1,)), ((), ()))
    per_chunk = ATTN_CHUNK // A_BLOCK
    units = [(j, kvh) for j in range(per_chunk) for kvh in range(A_KV_HEADS)]
    logits, values, probs = {}, {}, {}

    def swap_halves(t):
        return jnp.concatenate([t[:, A_HEAD_DIM:], t[:, :A_HEAD_DIM]], axis=1)

    def padded(t, t_sw, kvh):
        lo, hi = (t, t_sw) if kvh == 0 else (t_sw, t)
        return jnp.concatenate([jnp.where(low, lo, zero), jnp.where(low, zero, hi)], axis=0)

    def logit_matmuls():
        for j in range(per_chunk):
            rows = pl.ds(row0 + j * A_BLOCK, A_BLOCK)
            kw = kbuf[pl.ds(row0 + j * A_BLOCK, n_keys), :]
            vw = vbuf[pl.ds(row0 + j * A_BLOCK, n_keys), :]
            kw_sw, vw_sw = swap_halves(kw), swap_halves(vw)
            for kvh in range(A_KV_HEADS):
                q2 = jnp.concatenate([q_ref[0, rows, (2 * kvh + c) * LANES:(2 * kvh + c + 1) * LANES]
                                      for c in range(2)], axis=0)
                logits[j, kvh] = lax.dot_general(q2, padded(kw, kw_sw, kvh), nt,
                                                 preferred_element_type=F32)
                values[j, kvh] = padded(vw, vw_sw, kvh)

    def softmaxes():
        for j, kvh in units:
            block = first_block + j
            variant = jnp.where(block == 0, FIRST, jnp.where(block == n_blocks - 1, LAST, INTERIOR))
            s2 = logits[j, kvh]
            p_rows, r_rows = [], []
            for c in range(2):
                ps, sums, sink_gaps = [], [], []
                for half in range(2):
                    head = 2 * (2 * kvh + c) + half
                    sh = s2[c * A_BLOCK:(c + 1) * A_BLOCK, half * n_keys:(half + 1) * n_keys]
                    sh = sh + bias_ref[variant, head]
                    sink = sink_ref[head] * LOG2E
                    if fixed_shift:
                        m = ctl_ref[1 + head]
                        p = jnp.exp2(sh)
                    else:
                        m = jnp.maximum(jnp.max(sh, axis=-1, keepdims=True), sink)
                        p = jnp.exp2(sh - m)
                    ps.append(p.astype(BF16))
                    sums.append(jnp.sum(p, axis=-1, keepdims=True))
                    sink_gaps.append(sink - m)
                sink_lanes = low_q[:1] if fixed_shift else low_q
                den = (jnp.where(low_q, sums[0], sums[1])
                       + jnp.exp2(jnp.where(sink_lanes, sink_gaps[0], sink_gaps[1])))
                p_rows.append(jnp.concatenate(ps, axis=1))
                r_rows.append(1.0 / den)
            probs[j, kvh] = (jnp.concatenate(p_rows, axis=0), jnp.concatenate(r_rows, axis=0))

    def value_matmuls():
        for j, kvh in units:
            p2, r2 = probs[j, kvh]
            o2 = jnp.dot(p2, values[j, kvh], preferred_element_type=F32) * r2
            for c in range(2):
                tile = 2 * kvh + c
                o_ref[0, pl.ds(row0 + j * A_BLOCK, A_BLOCK), tile * LANES:(tile + 1) * LANES] = (
                    o2[c * A_BLOCK:(c + 1) * A_BLOCK].astype(BF16))

    return logit_matmuls, softmaxes, value_matmuls


def _attn_a_kernel(seq_len, q_ref, kp_ref, kc_ref, kn_ref, vp_ref, vc_ref, vn_ref, sink_ref, ctl_ref,
                   o_ref, kbuf, vbuf, bias_ref):
    i = pl.program_id(1)
    step_rows = q_ref.shape[1]

    @pl.when((pl.program_id(0) == 0) & (i == 0))
    def _():
        _attn_a_init_bias(bias_ref, ctl_ref)

    _attn_a_load_kv((kp_ref, kc_ref, kn_ref, vp_ref, vc_ref, vn_ref), kbuf, vbuf)

    def run(fixed_shift):
        def chunk(t, carry):
            row0 = pl.multiple_of(t * ATTN_CHUNK, ATTN_CHUNK)
            for phase in _attn_a_pass(row0, (i * step_rows + row0) // A_BLOCK, seq_len // A_BLOCK,
                                      fixed_shift, q_ref, kbuf, vbuf, bias_ref, sink_ref, ctl_ref, o_ref):
                phase()
            return carry

        lax.fori_loop(0, step_rows // ATTN_CHUNK, chunk, 0)

    _by_softmax_shift(ctl_ref, run)


def _attn_a_specs(s, step_rows, position):
    per_step = step_rows // A_BLOCK
    n_blocks = s // A_BLOCK
    assert n_blocks >= 2 and s % step_rows == 0

    def cur(col):
        def index(*ids):
            b, i = position(*ids)
            return (b, i, col)
        return index

    def prev(col):
        def index(*ids):
            b, i = position(*ids)
            return (b, jnp.maximum(i * per_step - 1, 0), col)
        return index

    def nxt(col):
        def index(*ids):
            b, i = position(*ids)
            return (b, jnp.minimum((i + 1) * per_step, n_blocks - 1), col)
        return index

    q_spec = pl.BlockSpec((1, step_rows, A_Q_W), cur(AM_QA // A_Q_W))
    o_spec = pl.BlockSpec((1, step_rows, A_Q_W), cur(0))
    kv_specs = [[pl.BlockSpec((1, A_BLOCK, A_KV_W), prev(col // A_KV_W)),
                 pl.BlockSpec((1, step_rows, A_KV_W), cur(col // A_KV_W)),
                 pl.BlockSpec((1, A_BLOCK, A_KV_W), nxt(col // A_KV_W))] for col in (AM_KA, AM_VA)]
    scratch = [pltpu.VMEM((step_rows + 2 * A_BLOCK, A_KV_W), BF16)] * 2 + [
        pltpu.VMEM((3, A_HEADS, A_BLOCK, 3 * A_BLOCK), F32)]
    return q_spec, o_spec, kv_specs, scratch


def _attn_a(am, sink, logit_bound):
    n, s, _ = am.shape
    step = min(s, A_STEP)
    q_spec, o_spec, kv_specs, scratch = _attn_a_specs(s, step, lambda b, i: (b, i))
    smem = pl.BlockSpec(memory_space=pltpu.SMEM)
    ctl = _softmax_ctl(jnp.maximum(logit_bound, sink * LOG2E))
    return pl.pallas_call(
        functools.partial(_attn_a_kernel, s),
        grid=(n, s // step),
        in_specs=[q_spec] + kv_specs[0] + kv_specs[1] + [smem, smem],
        out_specs=o_spec,
        out_shape=jax.ShapeDtypeStruct((n, s, A_Q_W), BF16),
        scratch_shapes=scratch,
        compiler_params=_params(2),
        name="attn_a",
    )(*[am] * 7, sink, ctl)


B_BLOCK = 2 * B_HALF_WIN
B_KEYS = B_BLOCK + 2 * B_HALF_WIN
B_ROWS_PER_STEP = 2048


def _attn_b_kernel(sub_len, slopes, q_ref, kp_ref, kc_ref, kn_ref, vp_ref, vc_ref, vn_ref, ctl_ref,
                   o_ref, lse_ref, kbuf, vbuf, bias_ref):
    i = pl.program_id(2)
    n_planes, step = q_ref.shape[1], q_ref.shape[2]
    n_blocks = sub_len // B_BLOCK

    @pl.when((pl.program_id(0) == 0) & (pl.program_id(1) == 0) & (i == 0))
    def _():
        shift = jnp.where(ctl_ref[0] > 0.5, ctl_ref[1], 0.0)
        q_row = lax.broadcasted_iota(jnp.int32, (B_BLOCK, B_KEYS), 0)
        k_col = lax.broadcasted_iota(jnp.int32, (B_BLOCK, B_KEYS), 1)
        dist = jnp.abs(q_row + B_HALF_WIN - k_col)
        dist_f = dist.astype(F32)
        for head, slope in enumerate(slopes):
            base = jnp.where(dist <= B_HALF_WIN, (-slope * LOG2E) * dist_f - shift, NEG_INF)
            first = jnp.where(k_col >= B_HALF_WIN, base, NEG_INF)
            bias_ref[0, head] = base
            bias_ref[1, head] = first
            bias_ref[2, head] = jnp.where(k_col < B_HALF_WIN + B_BLOCK, base, NEG_INF)
            bias_ref[3, head] = jnp.where(k_col < B_HALF_WIN + B_BLOCK, first, NEG_INF)

    for buf, prev, cur, nxt in ((kbuf, kp_ref, kc_ref, kn_ref), (vbuf, vp_ref, vc_ref, vn_ref)):
        for r in range(n_planes):
            buf[r, 0:B_HALF_WIN, :] = prev[0, r]
            buf[r, B_HALF_WIN:B_HALF_WIN + step, :] = cur[0, r]
            buf[r, B_HALF_WIN + step:, :] = nxt[0, r]

    lane = lax.broadcasted_iota(jnp.int32, (B_BLOCK, LANES), 1)
    lanes_per_head = LANES // B_HEADS_PER_GROUP
    heads = [slice(h * B_HEAD_DIM, (h + 1) * B_HEAD_DIM) for h in range(B_HEADS_PER_GROUP)]
    nt = (((1,), (1,)), ((), ()))
    rows_per_pass = min(step, ATTN_CHUNK)
    planes_per_pass = ATTN_CHUNK // rows_per_pass
    passes_per_plane = step // rows_per_pass
    blocks = [(dr, j) for dr in range(planes_per_pass) for j in range(rows_per_pass // B_BLOCK)]

    def one_pass(fixed_shift, t, carry):
        if planes_per_pass == 1:
            plane0 = t // passes_per_plane
            row0 = pl.multiple_of((t % passes_per_plane) * ATTN_CHUNK, ATTN_CHUNK)
        else:
            plane0, row0 = t * planes_per_pass, 0

        logits = {}
        for dr, j in blocks:
            for h, sl in enumerate(heads):
                q = q_ref[0, plane0 + dr, pl.ds(row0 + j * B_BLOCK, B_BLOCK), sl]
                k = kbuf[plane0 + dr, pl.ds(row0 + j * B_BLOCK, B_KEYS), sl]
                logits[dr, j, h] = lax.dot_general(q, k, nt, preferred_element_type=F32)

        probs = {}
        for dr, j in blocks:
            block = (i * step + row0) // B_BLOCK + j
            variant = (block == 0).astype(jnp.int32) + 2 * (block == n_blocks - 1).astype(jnp.int32)
            m_tile = den_tile = None
            for h in range(B_HEADS_PER_GROUP):
                s = logits[dr, j, h] + bias_ref[variant, h]
                if fixed_shift:
                    m = ctl_ref[1]
                    p = jnp.exp2(s)
                else:
                    m = jnp.max(s, axis=-1, keepdims=True)
                    p = jnp.exp2(s - m)
                den = jnp.sum(p, axis=-1, keepdims=True)
                probs[dr, j, h] = (p.astype(BF16), 1.0 / den)
                den_tile = den if h == 0 else jnp.where(lane >= h * lanes_per_head, den, den_tile)
                if not fixed_shift:
                    m_tile = m if h == 0 else jnp.where(lane >= h * lanes_per_head, m, m_tile)
            lse_ref[0, plane0 + dr, pl.ds(row0 + j * B_BLOCK, B_BLOCK), :] = (
                ((m if fixed_shift else m_tile) + jnp.log2(den_tile)) * LN2)

        for dr, j in blocks:
            for h, sl in enumerate(heads):
                p, rden = probs[dr, j, h]
                v = vbuf[plane0 + dr, pl.ds(row0 + j * B_BLOCK, B_KEYS), sl]
                o = jnp.dot(p, v, preferred_element_type=F32) * rden
                o_ref[0, plane0 + dr, pl.ds(row0 + j * B_BLOCK, B_BLOCK), sl] = o.astype(BF16)
        return carry

    _by_softmax_shift(ctl_ref, lambda fixed_shift: lax.fori_loop(
        0, n_planes * step // ATTN_CHUNK, functools.partial(one_pass, fixed_shift), 0))


def _attn_b_group(qkv, gi, logit_bound):
    n, dil, sub, _ = qkv.shape
    step = min(sub, B_ROWS_PER_STEP)
    n_planes = B_ROWS_PER_STEP // step
    assert sub % step == 0 and dil % n_planes == 0 and step % B_BLOCK == 0
    assert ATTN_CHUNK % step == 0 or step % ATTN_CHUNK == 0
    halo_per_step = step // B_HALF_WIN
    last_halo = sub // B_HALF_WIN - 1
    cur = lambda col: lambda b, r, i: (b, r, i, col)
    prev = lambda col: lambda b, r, i: (b, r, jnp.maximum(i * halo_per_step - 1, 0), col)
    nxt = lambda col: lambda b, r, i: (b, r, jnp.minimum((i + 1) * halo_per_step, last_halo), col)
    kv_specs = [[pl.BlockSpec((1, n_planes, B_HALF_WIN, B_GROUP_W), prev(col)),
                 pl.BlockSpec((1, n_planes, step, B_GROUP_W), cur(col)),
                 pl.BlockSpec((1, n_planes, B_HALF_WIN, B_GROUP_W), nxt(col))] for col in (1, 2)]
    all_slopes = _alibi_slopes(len(B_GROUPS) * B_HEADS_PER_GROUP)
    slopes = [all_slopes[gi * B_HEADS_PER_GROUP + h] * dil for h in range(B_HEADS_PER_GROUP)]
    return pl.pallas_call(
        functools.partial(_attn_b_kernel, sub, slopes),
        grid=(n, dil // n_planes, sub // step),
        in_specs=[pl.BlockSpec((1, n_planes, step, B_GROUP_W), cur(0))] + kv_specs[0] + kv_specs[1]
                 + [pl.BlockSpec(memory_space=pltpu.SMEM)],
        out_specs=[pl.BlockSpec((1, n_planes, step, B_GROUP_W), cur(0)),
                   pl.BlockSpec((1, n_planes, step, LANES), cur(0))],
        out_shape=[jax.ShapeDtypeStruct((n, dil, sub, B_GROUP_W), BF16),
                   jax.ShapeDtypeStruct((n, dil, sub, LANES), F32)],
        scratch_shapes=[pltpu.VMEM((n_planes, step + 2 * B_HALF_WIN, B_GROUP_W), BF16)] * 2
                       + [pltpu.VMEM((4, B_HEADS_PER_GROUP, B_BLOCK, B_KEYS), F32)],
        compiler_params=_params(3),
        name=f"attn_b{gi}",
    )(*[qkv] * 7, _softmax_ctl(logit_bound))


M_STEP = 4096


def _attn_m_kernel(q_ref, k_ref, v_ref, ctl_ref, o_ref):
    heads = [slice(h * M_HEAD_DIM, (h + 1) * M_HEAD_DIM) for h in range(M_HEADS)]
    nt = (((1,), (1,)), ((), ()))

    def chunk(fixed_shift, t, carry):
        rows = pl.ds(pl.multiple_of(t * ATTN_CHUNK, ATTN_CHUNK), ATTN_CHUNK)
        logits = [lax.dot_general(q_ref[0, rows, sl], k_ref[0, :, sl], nt, preferred_element_type=F32)
                  for sl in heads]
        probs = []
        for s in logits:
            m = ctl_ref[1] if fixed_shift else jnp.max(s, axis=-1, keepdims=True)
            p = jnp.exp2(s - m)
            probs.append((p.astype(BF16), 1.0 / jnp.sum(p, axis=-1, keepdims=True)))
        for sl, (p, rden) in zip(heads, probs):
            o = jnp.dot(p, v_ref[0, :, sl], preferred_element_type=F32) * rden
            o_ref[0, rows, sl] = o.astype(BF16)
        return carry

    _by_softmax_shift(ctl_ref, lambda fixed_shift: lax.fori_loop(
        0, q_ref.shape[1] // ATTN_CHUNK, functools.partial(chunk, fixed_shift), 0))


def _attn_m(am, mk, mv, logit_bound):
    n, s, _ = am.shape
    step = min(s, M_STEP)
    cur = lambda b, i: (b, i, 0)
    q_cols = lambda b, i: (b, i, AM_QM // M_W)
    mem = lambda b, i: (b, 0, 0)
    return pl.pallas_call(
        _attn_m_kernel,
        grid=(n, s // step),
        in_specs=[pl.BlockSpec((1, step, M_W), q_cols), pl.BlockSpec((1, N_MEM, M_W), mem),
                  pl.BlockSpec((1, N_MEM, M_W), mem), pl.BlockSpec(memory_space=pltpu.SMEM)],
        out_specs=pl.BlockSpec((1, step, M_W), cur),
        out_shape=jax.ShapeDtypeStruct((n, s, M_W), BF16),
        compiler_params=_params(2),
        name="attn_m",
    )(am, mk, mv, _softmax_ctl(logit_bound))


MERGE_ROWS = 512
assert all(dil <= DEINTERLEAVE_STRIDE ** 2 for _, dil in B_GROUPS)
N_MERGE_SLABS = sum((B_GROUP_W // LANES + 1) * (1 if dil <= DEINTERLEAVE_STRIDE else 2)
                    for _, dil in B_GROUPS if dil > 1)


def _merge_kernel(x_ref, g_ref, oa_ref, o0_ref, o1_ref, o2_ref, l0_ref, l1_ref, l2_ref, om_ref,
                  wg_ref, bg_ref, wbr_ref, wout_ref, y_ref, nat):
    tm = x_ref.shape[1]
    x = x_ref[0]
    h = _rms_rows(x, g_ref[...]).astype(BF16)
    tiles = B_GROUP_W // LANES

    o_nat, lse_nat = [], []
    slab = 0
    for (_, dil), o_ref, l_ref in zip(B_GROUPS, (o0_ref, o1_ref, o2_ref), (l0_ref, l1_ref, l2_ref)):
        if dil == 1:
            o_nat.append([o_ref[0, 0, :, c * LANES:(c + 1) * LANES].astype(F32) for c in range(tiles)])
            lse_nat.append(l_ref[0, 0])
            continue
        def plane_tile(r, c):
            if c == tiles:
                return l_ref[0, r]
            return o_ref[0, r, :, c * LANES:(c + 1) * LANES].astype(F32)

        for c in range(tiles + 1):
            if dil <= DEINTERLEAVE_STRIDE:
                for r in range(dil):
                    nat[slab + c, pl.ds(r, tm // dil, stride=dil), :] = plane_tile(r, c)
                continue
            groups = DEINTERLEAVE_STRIDE
            inner, rows = dil // groups, tm // groups
            spare = slab + tiles + 1 + c
            for g in range(groups):
                for r in range(inner):
                    nat[spare, pl.ds(g * rows + r, rows // inner, stride=inner), :] = plane_tile(g + groups * r, c)
            for g in range(groups):
                nat[slab + c, pl.ds(g, rows, stride=groups), :] = nat[spare, g * rows:(g + 1) * rows, :]
        o_nat.append([nat[slab + c] for c in range(tiles)])
        lse_nat.append(nat[slab + tiles])
        slab += (tiles + 1) * (1 if dil <= DEINTERLEAVE_STRIDE else 2)

    top = jnp.maximum(jnp.maximum(lse_nat[0], lse_nat[1]), lse_nat[2])
    es = [jnp.exp(l - top) for l in lse_nat]
    r_sum = 1.0 / (es[0] + es[1] + es[2])
    ws = [e * r_sum for e in es]
    lanes_per_head = LANES // B_HEADS_PER_GROUP
    ob_heads = []
    for hd in range(B_HEADS_PER_GROUP):
        acc = None
        for w, o in zip(ws, o_nat):
            term = w[:, hd * lanes_per_head:hd * lanes_per_head + 1] * o[hd]
            acc = term if acc is None else acc + term
        ob_heads.append(acc.astype(BF16))
    ob = jnp.concatenate(ob_heads, axis=1)

    z = None
    for b, o in enumerate((oa_ref[0], ob, om_ref[0])):
        logits = jnp.dot(h, wg_ref[:, b * D_MODEL:(b + 1) * D_MODEL], preferred_element_type=F32)
        gate = jax.nn.sigmoid(logits + bg_ref[b:b + 1, :])
        term = gate * jnp.dot(o, wbr_ref[b], preferred_element_type=F32)
        z = term if z is None else z + term
    y_ref[0] = x + jnp.dot(z.astype(BF16), wout_ref[...], preferred_element_type=F32)


def _merge(x, g_mix, oa, obs, lses, om, w_gate, b_gate, w_branch, w_out):
    n, s, _ = x.shape
    tm = MERGE_ROWS
    cur = lambda b, i: (b, i, 0)
    plane = lambda b, i: (b, 0, i, 0)
    rows = lambda w: pl.BlockSpec((1, tm, w), cur)
    planes = lambda w: [pl.BlockSpec((1, dil, tm // dil, w), plane) for _, dil in B_GROUPS]
    return pl.pallas_call(
        _merge_kernel,
        grid=(n, s // tm),
        in_specs=[rows(D_MODEL), _resident((1, D_MODEL)), rows(A_Q_W)]
                 + planes(B_GROUP_W) + planes(LANES) + [rows(M_W)]
                 + [_resident(w_gate.shape), _resident(b_gate.shape), _resident(w_branch.shape),
                    _resident(w_out.shape)],
        out_specs=rows(D_MODEL),
        out_shape=jax.ShapeDtypeStruct((n, s, D_MODEL), F32),
        scratch_shapes=[pltpu.VMEM((N_MERGE_SLABS, tm, LANES), F32)],
        compiler_params=_params(2),
        name="merge",
    )(x, g_mix, oa, *obs, *lses, om, w_gate, b_gate, w_branch, w_out)


MLP_ROWS = 1024
MLP_FF_CHUNK = 1024


def _mlp_kernel(x_ref, g_ref, wup_ref, wdn_ref, y_ref):
    x = x_ref[...]
    h = _rms_rows(x, g_ref[...]).astype(BF16)
    acc = x
    for lo in range(0, D_FF, MLP_FF_CHUNK):
        u = jnp.dot(h, wup_ref[:, lo:lo + MLP_FF_CHUNK], preferred_element_type=F32)
        a = jnp.square(jnp.maximum(u, 0.0)).astype(BF16)
        acc = acc + jnp.dot(a, wdn_ref[lo:lo + MLP_FF_CHUNK, :], preferred_element_type=F32)
    y_ref[...] = acc


def _mlp(x2d, g_mlp, w_up, w_down):
    t = x2d.shape[0]
    tm = MLP_ROWS
    row = lambda i: (i, 0)
    return pl.pallas_call(
        _mlp_kernel,
        grid=(t // tm,),
        in_specs=[pl.BlockSpec((tm, D_MODEL), row), _resident((1, D_MODEL)),
                  _resident(w_up.shape), _resident(w_down.shape)],
        out_specs=pl.BlockSpec((tm, D_MODEL), row),
        out_shape=jax.ShapeDtypeStruct((t, D_MODEL), F32),
        compiler_params=_params(1),
        name="mlp",
    )(x2d, g_mlp, w_up, w_down)


def _tile2(g):
    return jnp.concatenate([g, g]).reshape(1, LANES)


def _row(g):
    return g.reshape(1, -1)


def _layer(x, mem, g_mix, g_mem, w_in, b_gate, w_mem_kv, gq_a, gk_a, sink_a, gq_b, gk_b, gq_m, gk_m,
           w_branch, w_out, g_mlp, w_up, w_down):
    n, s, _ = x.shape
    bounds = np.cumsum((0, A_Q_W, A_KV_W, A_KV_W, B_W, B_W, B_W, M_W, GATE_W))
    seg = lambda a, b: w_in[:, int(bounds[a]):int(bounds[b])].astype(BF16)
    ws = (seg(0, 1), seg(1, 3), seg(3, 4), seg(4, 5), seg(5, 6), seg(6, 7))
    w_gate = seg(7, 8)
    gq_a, gq_b, gq_m = (gq_a * (A_HEAD_DIM ** -0.5 * LOG2E), gq_b * (B_HEAD_DIM ** -0.5 * LOG2E),
                        gq_m * (M_HEAD_DIM ** -0.5 * LOG2E))
    gains = (_tile2(gq_a), _tile2(gk_a), _row(gq_b), _row(gk_b), _row(gq_m))

    am, *qkv_b = _proj(x, _row(g_mix), ws, gains)
    mk, mv = _mem_kv(mem.reshape(n * N_MEM, D_MODEL), _row(g_mem), w_mem_kv.astype(BF16), _row(gk_m))

    oa = _attn_a(am, sink_a, _logit_bound(gq_a, gk_a, A_HEAD_DIM))
    bound_b = _logit_bound(gq_b, gk_b, B_HEAD_DIM)
    b_out = [_attn_b_group(qkv_b[gi], gi, bound_b) for gi in range(len(B_GROUPS))]
    om = _attn_m(am, mk.reshape(n, N_MEM, M_W), mv.reshape(n, N_MEM, M_W),
                 _logit_bound(gq_m, gk_m, M_HEAD_DIM))

    x1 = _merge(x, _row(g_mix), oa, [o for o, _ in b_out], [l for _, l in b_out], om,
                w_gate, b_gate, w_branch.astype(BF16), w_out.astype(BF16))
    y = _mlp(x1.reshape(n * s, D_MODEL), _row(g_mlp), w_up.astype(BF16), w_down.astype(BF16))
    return y.reshape(n, s, D_MODEL)


def kernel(x_prompt, x_sample, mem_prompt, mem_sample, g_mix, g_mem, w_in, b_gate, w_mem_kv, gq_a, gk_a,
           sink_a, gq_b, gk_b, gq_m, gk_m, w_branch, w_out, g_mlp, w_up, w_down):
    depth = w_in.shape[0]

    def run(x, mem):
        for l in range(depth):
            x = _layer(x, mem, g_mix[l], g_mem[l], w_in[l], b_gate[l], w_mem_kv[l], gq_a[l], gk_a[l],
                       sink_a[l], gq_b[l], gk_b[l], gq_m[l], gk_m[l], w_branch[l], w_out[l], g_mlp[l],
                       w_up[l], w_down[l])
        return x

    return (run(x_prompt, mem_prompt), run(x_sample, mem_sample))
```

```python
import functools

import numpy as np
import jax
import jax.numpy as jnp
from jax import lax
from jax.experimental import pallas as pl
from jax.experimental.pallas import tpu as pltpu

D_MODEL = 1024
N_MEM = 256
A_HEADS = 8
A_KV_HEADS = 2
A_HEAD_DIM = 64
A_HALF_WIN = 128
B_GROUPS = ((128, 1), (512, 4), (2048, 16))
B_HEADS_PER_GROUP = 4
B_HEAD_DIM = 128
M_HEADS = 4
M_HEAD_DIM = 128
N_BRANCH = 3
BRANCH_WIDTH = D_MODEL // 2
D_FF = 4 * D_MODEL
EPS = 1e-6
NEG_INF = -1e30

A_Q_W = A_HEADS * A_HEAD_DIM
A_KV_W = A_KV_HEADS * A_HEAD_DIM
B_GROUP_W = B_HEADS_PER_GROUP * B_HEAD_DIM
B_W = len(B_GROUPS) * B_GROUP_W
M_W = M_HEADS * M_HEAD_DIM
GATE_W = N_BRANCH * D_MODEL

LANES = 128
B_HALF_WIN = 64
VMEM_LIMIT_BYTES = 56 * 1024 * 1024

BF16 = jnp.bfloat16
F32 = jnp.float32


def _alibi_slopes(n):
    return [float(2.0 ** (-8.0 * (i + 1) / n)) for i in range(n)]


def _params(n_grid_axes):
    return pltpu.CompilerParams(
        dimension_semantics=("arbitrary",) * n_grid_axes,
        vmem_limit_bytes=VMEM_LIMIT_BYTES)


def _resident(shape):
    zeros = (0,) * len(shape)
    return pl.BlockSpec(shape, lambda *_: zeros, pipeline_mode=pl.Buffered(1))


def _rms_rows(x, gain):
    ms = jnp.mean(x * x, axis=-1, keepdims=True)
    return x * lax.rsqrt(ms + EPS) * gain


def _head_norm128(blk, gain):
    ms = jnp.sum(blk * blk, axis=-1, keepdims=True) * (1.0 / LANES)
    return blk * lax.rsqrt(ms + EPS) * gain


def _head_norm64(blk, gain2):
    low = lax.broadcasted_iota(jnp.int32, blk.shape, 1) < A_HEAD_DIM
    sq = blk * blk
    ss_lo = jnp.sum(jnp.where(low, sq, 0.0), axis=-1, keepdims=True)
    ss_hi = jnp.sum(jnp.where(low, 0.0, sq), axis=-1, keepdims=True)
    ms = jnp.where(low, ss_lo, ss_hi) * (1.0 / A_HEAD_DIM)
    return blk * lax.rsqrt(ms + EPS) * gain2


PROJ_ROWS = 1024
AM_QA, AM_QM, AM_KA, AM_VA = 0, A_Q_W, A_Q_W + M_W, A_Q_W + M_W + A_KV_W
AM_W = A_Q_W + M_W + 2 * A_KV_W
B_QKV_W = 3 * B_GROUP_W
DEINTERLEAVE_STRIDE = 4
N_DILATED_SLABS = 2 * (B_GROUP_W // LANES)


def _proj_kernel(x_ref, g_ref, wqa_ref, wkva_ref, wqb_ref, wkb_ref, wvb_ref, wqm_ref,
                 gqa_ref, gka_ref, gqb_ref, gkb_ref, gqm_ref,
                 am_ref, *rest):
    b_refs, ybuf = rest[:-1], rest[-1]
    tm = x_ref.shape[1]
    h = _rms_rows(x_ref[0], g_ref[...]).astype(BF16)

    def mm(w_ref, lo, hi):
        return lambda: jnp.dot(h, w_ref[:, lo:hi], preferred_element_type=F32)

    def finish_qa(y):
        for c in range(A_Q_W // LANES):
            sl = slice(c * LANES, (c + 1) * LANES)
            am_ref[0, :, AM_QA + c * LANES:AM_QA + (c + 1) * LANES] = (
                _head_norm64(y[:, sl], gqa_ref[...]).astype(BF16))

    def finish_kva(y):
        am_ref[0, :, AM_KA:AM_KA + A_KV_W] = _head_norm64(y[:, :A_KV_W], gka_ref[...]).astype(BF16)
        am_ref[0, :, AM_VA:AM_VA + A_KV_W] = y[:, A_KV_W:].astype(BF16)

    def finish_qm(y):
        for c in range(M_W // LANES):
            sl = slice(c * LANES, (c + 1) * LANES)
            am_ref[0, :, AM_QM + c * LANES:AM_QM + (c + 1) * LANES] = (
                _head_norm128(y[:, sl], gqm_ref[...]).astype(BF16))

    def finish_b(o_ref, col0, gain_ref, dil, slab):
        def finish(y):
            for c in range(B_GROUP_W // LANES):
                sl = slice(c * LANES, (c + 1) * LANES)
                out = slice(col0 + c * LANES, col0 + (c + 1) * LANES)
                blk = y[:, sl]
                if gain_ref is not None:
                    blk = _head_norm128(blk, gain_ref[...])
                if dil == 1:
                    o_ref[0, 0, :, out] = blk.astype(BF16)
                else:
                    ybuf[slab + c] = blk
                    groups, stride = 1, dil
                    while stride > DEINTERLEAVE_STRIDE:
                        rows = tm // groups
                        parts = [ybuf[slab + c, pl.ds(g * rows + r, rows // DEINTERLEAVE_STRIDE,
                                                      stride=DEINTERLEAVE_STRIDE), :]
                                 for g in range(groups) for r in range(DEINTERLEAVE_STRIDE)]
                        ybuf[slab + c] = jnp.concatenate(parts, axis=0)
                        groups, stride = groups * DEINTERLEAVE_STRIDE, stride // DEINTERLEAVE_STRIDE
                    rows = tm // groups
                    for g in range(groups):
                        for r in range(stride):
                            o_ref[0, g + groups * r, :, out] = (
                                ybuf[slab + c, pl.ds(g * rows + r, rows // stride, stride=stride), :].astype(BF16))
        return finish

    def branch_b(gi, slab):
        dil = B_GROUPS[gi][1]
        lo = gi * B_GROUP_W
        parts = []
        for part, (w_ref, gain_ref) in enumerate(((wqb_ref, gqb_ref), (wkb_ref, gkb_ref), (wvb_ref, None))):
            parts.append((mm(w_ref, lo, lo + B_GROUP_W),
                          finish_b(b_refs[gi], part * B_GROUP_W, gain_ref, dil, slab)))
            slab = (slab + B_GROUP_W // LANES) % N_DILATED_SLABS
        return parts, slab

    by_dilation = sorted(range(len(B_GROUPS)), key=lambda gi: -B_GROUPS[gi][1])
    stages, slab = [], 0
    for gi in by_dilation[:-1]:
        parts, slab = branch_b(gi, slab)
        stages += parts
    stages += [(mm(wqa_ref, 0, A_Q_W), finish_qa), (mm(wkva_ref, 0, 2 * A_KV_W), finish_kva),
               (mm(wqm_ref, 0, M_W), finish_qm)]
    stages += branch_b(by_dilation[-1], slab)[0]

    pending = None
    for matmul, finish in stages:
        y = matmul()
        if pending is not None:
            pending[0](pending[1])
        pending = (finish, y)
    pending[0](pending[1])


def _proj(x, g_mix, ws, gains):
    n, s, _ = x.shape
    tm = PROJ_ROWS
    cur = lambda b, i: (b, i, 0)
    plane = lambda b, i: (b, 0, i, 0)
    in_specs = ([pl.BlockSpec((1, tm, D_MODEL), cur), _resident((1, D_MODEL))]
                + [_resident(w.shape) for w in ws]
                + [_resident((1, LANES)) for _ in gains])
    out_specs = [pl.BlockSpec((1, tm, AM_W), cur)]
    out_shape = [jax.ShapeDtypeStruct((n, s, AM_W), BF16)]
    for _, dil in B_GROUPS:
        out_specs.append(pl.BlockSpec((1, dil, tm // dil, B_QKV_W), plane))
        out_shape.append(jax.ShapeDtypeStruct((n, dil, s // dil, B_QKV_W), BF16))
    return pl.pallas_call(
        _proj_kernel,
        grid=(n, s // tm),
        in_specs=in_specs,
        out_specs=out_specs,
        out_shape=out_shape,
        scratch_shapes=[pltpu.VMEM((N_DILATED_SLABS, tm, LANES), F32)],
        compiler_params=_params(2),
        name="proj",
    )(x, g_mix, *ws, *gains)


MEM_ROWS = 1024


def _mem_kv_kernel(m_ref, g_ref, w_ref, gk_ref, k_ref, v_ref):
    h = _rms_rows(m_ref[...], g_ref[...]).astype(BF16)
    y = jnp.dot(h, w_ref[...], preferred_element_type=F32)
    for c in range(M_HEADS):
        sl = slice(c * LANES, (c + 1) * LANES)
        k_ref[:, sl] = _head_norm128(y[:, sl], gk_ref[...]).astype(BF16)
    v_ref[...] = y[:, M_W:].astype(BF16)


def _mem_kv(mem2d, g_mem, w_mem_kv, gk_m):
    t = mem2d.shape[0]
    tm = min(t, MEM_ROWS)
    row = lambda i: (i, 0)
    return pl.pallas_call(
        _mem_kv_kernel,
        grid=(t // tm,),
        in_specs=[pl.BlockSpec((tm, D_MODEL), row), _resident((1, D_MODEL)),
                  _resident(w_mem_kv.shape), _resident((1, LANES))],
        out_specs=[pl.BlockSpec((tm, M_W), row)] * 2,
        out_shape=[jax.ShapeDtypeStruct((t, M_W), BF16)] * 2,
        compiler_params=_params(1),
        name="mem_kv",
    )(mem2d, g_mem, w_mem_kv, gk_m)


A_BLOCK = A_HALF_WIN
A_STEP = 2048
ATTN_CHUNK = 1024
INTERIOR, FIRST, LAST = 0, 1, 2
LOG2E = float(np.log2(np.e))
LN2 = float(np.log(2.0))

SOFTMAX_SHIFT_LIMIT = 40.0
BF16_SLACK = 1.0 + 2.0 ** -6


def _logit_bound(gain_q, gain_k, head_dim):
    return head_dim * jnp.max(jnp.abs(gain_q)) * jnp.max(jnp.abs(gain_k)) * BF16_SLACK


def _softmax_ctl(shifts):
    shifts = jnp.asarray(shifts, F32).reshape(-1)
    fixed = jnp.all(shifts <= SOFTMAX_SHIFT_LIMIT)
    return jnp.concatenate([fixed.astype(F32).reshape(1), shifts])


def _by_softmax_shift(ctl_ref, run):
    fixed = ctl_ref[0] > 0.5

    @pl.when(fixed)
    def _():
        run(True)

    @pl.when(jnp.logical_not(fixed))
    def _():
        run(False)


def _attn_a_init_bias(bias_ref, ctl_ref):
    n_keys = 3 * A_BLOCK
    q_row = lax.broadcasted_iota(jnp.int32, (A_BLOCK, n_keys), 0)
    k_col = lax.broadcasted_iota(jnp.int32, (A_BLOCK, n_keys), 1)
    dist = jnp.abs(q_row + A_BLOCK - k_col)
    dist_f = dist.astype(F32)
    for head, slope in enumerate(_alibi_slopes(A_HEADS)):
        shift = jnp.where(ctl_ref[0] > 0.5, ctl_ref[1 + head], 0.0)
        base = jnp.where(dist <= A_HALF_WIN, (-slope * LOG2E) * dist_f - shift, NEG_INF)
        bias_ref[INTERIOR, head] = base
        bias_ref[FIRST, head] = jnp.where(k_col >= A_BLOCK, base, NEG_INF)
        bias_ref[LAST, head] = jnp.where(k_col < 2 * A_BLOCK, base, NEG_INF)


def _attn_a_load_kv(kv_refs, kbuf, vbuf):
    kp_ref, kc_ref, kn_ref, vp_ref, vc_ref, vn_ref = kv_refs
    rows = kc_ref.shape[1]
    for buf, prev, cur, nxt in ((kbuf, kp_ref, kc_ref, kn_ref), (vbuf, vp_ref, vc_ref, vn_ref)):
        buf[0:A_BLOCK, :] = prev[0]
        buf[A_BLOCK:A_BLOCK + rows, :] = cur[0]
        buf[A_BLOCK + rows:, :] = nxt[0]


def _attn_a_pass(row0, first_block, n_blocks, fixed_shift, q_ref, kbuf, vbuf, bias_ref, sink_ref, ctl_ref,
                 o_ref):
    n_keys = 3 * A_BLOCK
    low = lax.broadcasted_iota(jnp.int32, (n_keys, LANES), 1) < A_HEAD_DIM
    low_q = lax.broadcasted_iota(jnp.int32, (A_BLOCK, LANES), 1) < A_HEAD_DIM
    zero = jnp.zeros((n_keys, LANES), BF16)
    nt = (((1,), (1,)), ((), ()))
    per_chunk = ATTN_CHUNK // A_BLOCK
    units = [(j, kvh) for j in range(per_chunk) for kvh in range(A_KV_HEADS)]
    logits, values, probs = {}, {}, {}

    def swap_halves(t):
        return jnp.concatenate([t[:, A_HEAD_DIM:], t[:, :A_HEAD_DIM]], axis=1)

    def padded(t, t_sw, kvh):
        lo, hi = (t, t_sw) if kvh == 0 else (t_sw, t)
        return jnp.concatenate([jnp.where(low, lo, zero), jnp.where(low, zero, hi)], axis=0)

    def logit_matmuls():
        for j in range(per_chunk):
            rows = pl.ds(row0 + j * A_BLOCK, A_BLOCK)
            kw = kbuf[pl.ds(row0 + j * A_BLOCK, n_keys), :]
            vw = vbuf[pl.ds(row0 + j * A_BLOCK, n_keys), :]
            kw_sw, vw_sw = swap_halves(kw), swap_halves(vw)
            for kvh in range(A_KV_HEADS):
                q2 = jnp.concatenate([q_ref[0, rows, (2 * kvh + c) * LANES:(2 * kvh + c + 1) * LANES]
                                      for c in range(2)], axis=0)
                logits[j, kvh] = lax.dot_general(q2, padded(kw, kw_sw, kvh), nt,
                                                 preferred_element_type=F32)
                values[j, kvh] = padded(vw, vw_sw, kvh)

    def softmaxes():
        for j, kvh in units:
            block = first_block + j
            variant = jnp.where(block == 0, FIRST, jnp.where(block == n_blocks - 1, LAST, INTERIOR))
            s2 = logits[j, kvh]
            p_rows, r_rows = [], []
            for c in range(2):
                ps, sums, sink_gaps = [], [], []
                for half in range(2):
                    head = 2 * (2 * kvh + c) + half
                    sh = s2[c * A_BLOCK:(c + 1) * A_BLOCK, half * n_keys:(half + 1) * n_keys]
                    sh = sh + bias_ref[variant, head]
                    sink = sink_ref[head] * LOG2E
                    if fixed_shift:
                        m = ctl_ref[1 + head]
                        p = jnp.exp2(sh)
                    else:
                        m = jnp.maximum(jnp.max(sh, axis=-1, keepdims=True), sink)
                        p = jnp.exp2(sh - m)
                    ps.append(p.astype(BF16))
                    sums.append(jnp.sum(p, axis=-1, keepdims=True))
                    sink_gaps.append(sink - m)
                sink_lanes = low_q[:1] if fixed_shift else low_q
                den = (jnp.where(low_q, sums[0], sums[1])
                       + jnp.exp2(jnp.where(sink_lanes, sink_gaps[0], sink_gaps[1])))
                p_rows.append(jnp.concatenate(ps, axis=1))
                r_rows.append(1.0 / den)
            probs[j, kvh] = (jnp.concatenate(p_rows, axis=0), jnp.concatenate(r_rows, axis=0))

    def value_matmuls():
        for j, kvh in units:
            p2, r2 = probs[j, kvh]
            o2 = jnp.dot(p2, values[j, kvh], preferred_element_type=F32) * r2
            for c in range(2):
                tile = 2 * kvh + c
                o_ref[0, pl.ds(row0 + j * A_BLOCK, A_BLOCK), tile * LANES:(tile + 1) * LANES] = (
                    o2[c * A_BLOCK:(c + 1) * A_BLOCK].astype(BF16))

    return logit_matmuls, softmaxes, value_matmuls


def _attn_a_kernel(seq_len, q_ref, kp_ref, kc_ref, kn_ref, vp_ref, vc_ref, vn_ref, sink_ref, ctl_ref,
                   o_ref, kbuf, vbuf, bias_ref):
    i = pl.program_id(1)
    step_rows = q_ref.shape[1]

    @pl.when((pl.program_id(0) == 0) & (i == 0))
    def _():
        _attn_a_init_bias(bias_ref, ctl_ref)

    _attn_a_load_kv((kp_ref, kc_ref, kn_ref, vp_ref, vc_ref, vn_ref), kbuf, vbuf)

    def run(fixed_shift):
        def chunk(t, carry):
            row0 = pl.multiple_of(t * ATTN_CHUNK, ATTN_CHUNK)
            for phase in _attn_a_pass(row0, (i * step_rows + row0) // A_BLOCK, seq_len // A_BLOCK,
                                      fixed_shift, q_ref, kbuf, vbuf, bias_ref, sink_ref, ctl_ref, o_ref):
                phase()
            return carry

        lax.fori_loop(0, step_rows // ATTN_CHUNK, chunk, 0)

    _by_softmax_shift(ctl_ref, run)


def _attn_a_specs(s, step_rows, position):
    per_step = step_rows // A_BLOCK
    n_blocks = s // A_BLOCK
    assert n_blocks >= 2 and s % step_rows == 0

    def cur(col):
        def index(*ids):
            b, i = position(*ids)
            return (b, i, col)
        return index

    def prev(col):
        def index(*ids):
            b, i = position(*ids)
            return (b, jnp.maximum(i * per_step - 1, 0), col)
        return index

    def nxt(col):
        def index(*ids):
            b, i = position(*ids)
            return (b, jnp.minimum((i + 1) * per_step, n_blocks - 1), col)
        return index

    q_spec = pl.BlockSpec((1, step_rows, A_Q_W), cur(AM_QA // A_Q_W))
    o_spec = pl.BlockSpec((1, step_rows, A_Q_W), cur(0))
    kv_specs = [[pl.BlockSpec((1, A_BLOCK, A_KV_W), prev(col // A_KV_W)),
                 pl.BlockSpec((1, step_rows, A_KV_W), cur(col // A_KV_W)),
                 pl.BlockSpec((1, A_BLOCK, A_KV_W), nxt(col // A_KV_W))] for col in (AM_KA, AM_VA)]
    scratch = [pltpu.VMEM((step_rows + 2 * A_BLOCK, A_KV_W), BF16)] * 2 + [
        pltpu.VMEM((3, A_HEADS, A_BLOCK, 3 * A_BLOCK), F32)]
    return q_spec, o_spec, kv_specs, scratch


def _attn_a(am, sink, logit_bound):
    n, s, _ = am.shape
    step = min(s, A_STEP)
    q_spec, o_spec, kv_specs, scratch = _attn_a_specs(s, step, lambda b, i: (b, i))
    smem = pl.BlockSpec(memory_space=pltpu.SMEM)
    ctl = _softmax_ctl(jnp.maximum(logit_bound, sink * LOG2E))
    return pl.pallas_call(
        functools.partial(_attn_a_kernel, s),
        grid=(n, s // step),
        in_specs=[q_spec] + kv_specs[0] + kv_specs[1] + [smem, smem],
        out_specs=o_spec,
        out_shape=jax.ShapeDtypeStruct((n, s, A_Q_W), BF16),
        scratch_shapes=scratch,
        compiler_params=_params(2),
        name="attn_a",
    )(*[am] * 7, sink, ctl)


B_BLOCK = 2 * B_HALF_WIN
B_KEYS = B_BLOCK + 2 * B_HALF_WIN
B_ROWS_PER_STEP = 2048


def _attn_b_kernel(sub_len, slopes, q_ref, kp_ref, kc_ref, kn_ref, vp_ref, vc_ref, vn_ref, ctl_ref,
                   o_ref, lse_ref, kbuf, vbuf, bias_ref):
    i = pl.program_id(2)
    n_planes, step = q_ref.shape[1], q_ref.shape[2]
    n_blocks = sub_len // B_BLOCK

    @pl.when((pl.program_id(0) == 0) & (pl.program_id(1) == 0) & (i == 0))
    def _():
        shift = jnp.where(ctl_ref[0] > 0.5, ctl_ref[1], 0.0)
        q_row = lax.broadcasted_iota(jnp.int32, (B_BLOCK, B_KEYS), 0)
        k_col = lax.broadcasted_iota(jnp.int32, (B_BLOCK, B_KEYS), 1)
        dist = jnp.abs(q_row + B_HALF_WIN - k_col)
        dist_f = dist.astype(F32)
        for head, slope in enumerate(slopes):
            base = jnp.where(dist <= B_HALF_WIN, (-slope * LOG2E) * dist_f - shift, NEG_INF)
            first = jnp.where(k_col >= B_HALF_WIN, base, NEG_INF)
            bias_ref[0, head] = base
            bias_ref[1, head] = first
            bias_ref[2, head] = jnp.where(k_col < B_HALF_WIN + B_BLOCK, base, NEG_INF)
            bias_ref[3, head] = jnp.where(k_col < B_HALF_WIN + B_BLOCK, first, NEG_INF)

    for buf, prev, cur, nxt in ((kbuf, kp_ref, kc_ref, kn_ref), (vbuf, vp_ref, vc_ref, vn_ref)):
        for r in range(n_planes):
            buf[r, 0:B_HALF_WIN, :] = prev[0, r]
            buf[r, B_HALF_WIN:B_HALF_WIN + step, :] = cur[0, r]
            buf[r, B_HALF_WIN + step:, :] = nxt[0, r]

    lane = lax.broadcasted_iota(jnp.int32, (B_BLOCK, LANES), 1)
    lanes_per_head = LANES // B_HEADS_PER_GROUP
    heads = [slice(h * B_HEAD_DIM, (h + 1) * B_HEAD_DIM) for h in range(B_HEADS_PER_GROUP)]
    nt = (((1,), (1,)), ((), ()))
    rows_per_pass = min(step, ATTN_CHUNK)
    planes_per_pass = ATTN_CHUNK // rows_per_pass
    passes_per_plane = step // rows_per_pass
    blocks = [(dr, j) for dr in range(planes_per_pass) for j in range(rows_per_pass // B_BLOCK)]

    def one_pass(fixed_shift, t, carry):
        if planes_per_pass == 1:
            plane0 = t // passes_per_plane
            row0 = pl.multiple_of((t % passes_per_plane) * ATTN_CHUNK, ATTN_CHUNK)
        else:
            plane0, row0 = t * planes_per_pass, 0

        logits = {}
        for dr, j in blocks:
            for h, sl in enumerate(heads):
                q = q_ref[0, plane0 + dr, pl.ds(row0 + j * B_BLOCK, B_BLOCK), sl]
                k = kbuf[plane0 + dr, pl.ds(row0 + j * B_BLOCK, B_KEYS), sl]
                logits[dr, j, h] = lax.dot_general(q, k, nt, preferred_element_type=F32)

        probs = {}
        for dr, j in blocks:
            block = (i * step + row0) // B_BLOCK + j
            variant = (block == 0).astype(jnp.int32) + 2 * (block == n_blocks - 1).astype(jnp.int32)
            m_tile = den_tile = None
            for h in range(B_HEADS_PER_GROUP):
                s = logits[dr, j, h] + bias_ref[variant, h]
                if fixed_shift:
                    m = ctl_ref[1]
                    p = jnp.exp2(s)
                else:
                    m = jnp.max(s, axis=-1, keepdims=True)
                    p = jnp.exp2(s - m)
                den = jnp.sum(p, axis=-1, keepdims=True)
                probs[dr, j, h] = (p.astype(BF16), 1.0 / den)
                den_tile = den if h == 0 else jnp.where(lane >= h * lanes_per_head, den, den_tile)
                if not fixed_shift:
                    m_tile = m if h == 0 else jnp.where(lane >= h * lanes_per_head, m, m_tile)
            lse_ref[0, plane0 + dr, pl.ds(row0 + j * B_BLOCK, B_BLOCK), :] = (
                ((m if fixed_shift else m_tile) + jnp.log2(den_tile)) * LN2)

        for dr, j in blocks:
            for h, sl in enumerate(heads):
                p, rden = probs[dr, j, h]
                v = vbuf[plane0 + dr, pl.ds(row0 + j * B_BLOCK, B_KEYS), sl]
                o = jnp.dot(p, v, preferred_element_type=F32) * rden
                o_ref[0, plane0 + dr, pl.ds(row0 + j * B_BLOCK, B_BLOCK), sl] = o.astype(BF16)
        return carry

    _by_softmax_shift(ctl_ref, lambda fixed_shift: lax.fori_loop(
        0, n_planes * step // ATTN_CHUNK, functools.partial(one_pass, fixed_shift), 0))


def _attn_b_group(qkv, gi, logit_bound):
    n, dil, sub, _ = qkv.shape
    step = min(sub, B_ROWS_PER_STEP)
    n_planes = B_ROWS_PER_STEP // step
    assert sub % step == 0 and dil % n_planes == 0 and step % B_BLOCK == 0
    assert ATTN_CHUNK % step == 0 or step % ATTN_CHUNK == 0
    halo_per_step = step // B_HALF_WIN
    last_halo = sub // B_HALF_WIN - 1
    cur = lambda col: lambda b, r, i: (b, r, i, col)
    prev = lambda col: lambda b, r, i: (b, r, jnp.maximum(i * halo_per_step - 1, 0), col)
    nxt = lambda col: lambda b, r, i: (b, r, jnp.minimum((i + 1) * halo_per_step, last_halo), col)
    kv_specs = [[pl.BlockSpec((1, n_planes, B_HALF_WIN, B_GROUP_W), prev(col)),
                 pl.BlockSpec((1, n_planes, step, B_GROUP_W), cur(col)),
                 pl.BlockSpec((1, n_planes, B_HALF_WIN, B_GROUP_W), nxt(col))] for col in (1, 2)]
    all_slopes = _alibi_slopes(len(B_GROUPS) * B_HEADS_PER_GROUP)
    slopes = [all_slopes[gi * B_HEADS_PER_GROUP + h] * dil for h in range(B_HEADS_PER_GROUP)]
    return pl.pallas_call(
        functools.partial(_attn_b_kernel, sub, slopes),
        grid=(n, dil // n_planes, sub // step),
        in_specs=[pl.BlockSpec((1, n_planes, step, B_GROUP_W), cur(0))] + kv_specs[0] + kv_specs[1]
                 + [pl.BlockSpec(memory_space=pltpu.SMEM)],
        out_specs=[pl.BlockSpec((1, n_planes, step, B_GROUP_W), cur(0)),
                   pl.BlockSpec((1, n_planes, step, LANES), cur(0))],
        out_shape=[jax.ShapeDtypeStruct((n, dil, sub, B_GROUP_W), BF16),
                   jax.ShapeDtypeStruct((n, dil, sub, LANES), F32)],
        scratch_shapes=[pltpu.VMEM((n_planes, step + 2 * B_HALF_WIN, B_GROUP_W), BF16)] * 2
                       + [pltpu.VMEM((4, B_HEADS_PER_GROUP, B_BLOCK, B_KEYS), F32)],
        compiler_params=_params(3),
        name=f"attn_b{gi}",
    )(*[qkv] * 7, _softmax_ctl(logit_bound))


M_STEP = 2048


def _attn_m_kernel(q_ref, k_ref, v_ref, ctl_ref, o_ref):
    heads = [slice(h * M_HEAD_DIM, (h + 1) * M_HEAD_DIM) for h in range(M_HEADS)]
    nt = (((1,), (1,)), ((), ()))

    def chunk(fixed_shift, t, carry):
        rows = pl.ds(pl.multiple_of(t * ATTN_CHUNK, ATTN_CHUNK), ATTN_CHUNK)
        logits = [lax.dot_general(q_ref[0, rows, sl], k_ref[0, :, sl], nt, preferred_element_type=F32)
                  for sl in heads]
        probs = []
        for s in logits:
            m = ctl_ref[1] if fixed_shift else jnp.max(s, axis=-1, keepdims=True)
            p = jnp.exp2(s - m)
            probs.append((p.astype(BF16), 1.0 / jnp.sum(p, axis=-1, keepdims=True)))
        for sl, (p, rden) in zip(heads, probs):
            o = jnp.dot(p, v_ref[0, :, sl], preferred_element_type=F32) * rden
            o_ref[0, rows, sl] = o.astype(BF16)
        return carry

    _by_softmax_shift(ctl_ref, lambda fixed_shift: lax.fori_loop(
        0, q_ref.shape[1] // ATTN_CHUNK, functools.partial(chunk, fixed_shift), 0))


def _attn_m(am, mk, mv, logit_bound):
    n, s, _ = am.shape
    step = min(s, M_STEP)
    cur = lambda b, i: (b, i, 0)
    q_cols = lambda b, i: (b, i, AM_QM // M_W)
    mem = lambda b, i: (b, 0, 0)
    return pl.pallas_call(
        _attn_m_kernel,
        grid=(n, s // step),
        in_specs=[pl.BlockSpec((1, step, M_W), q_cols), pl.BlockSpec((1, N_MEM, M_W), mem),
                  pl.BlockSpec((1, N_MEM, M_W), mem), pl.BlockSpec(memory_space=pltpu.SMEM)],
        out_specs=pl.BlockSpec((1, step, M_W), cur),
        out_shape=jax.ShapeDtypeStruct((n, s, M_W), BF16),
        compiler_params=_params(2),
        name="attn_m",
    )(am, mk, mv, _softmax_ctl(logit_bound))


MERGE_ROWS = 512
assert all(dil <= DEINTERLEAVE_STRIDE ** 2 for _, dil in B_GROUPS)
N_MERGE_SLABS = sum((B_GROUP_W // LANES + 1) * (1 if dil <= DEINTERLEAVE_STRIDE else 2)
                    for _, dil in B_GROUPS if dil > 1)


def _merge_kernel(x_ref, g_ref, oa_ref, o0_ref, o1_ref, o2_ref, l0_ref, l1_ref, l2_ref, om_ref,
                  wg_ref, bg_ref, wbr_ref, wout_ref, y_ref, nat):
    tm = x_ref.shape[1]
    x = x_ref[0]
    h = _rms_rows(x, g_ref[...]).astype(BF16)
    tiles = B_GROUP_W // LANES

    o_nat, lse_nat = [], []
    slab = 0
    for (_, dil), o_ref, l_ref in zip(B_GROUPS, (o0_ref, o1_ref, o2_ref), (l0_ref, l1_ref, l2_ref)):
        if dil == 1:
            o_nat.append([o_ref[0, 0, :, c * LANES:(c + 1) * LANES].astype(F32) for c in range(tiles)])
            lse_nat.append(l_ref[0, 0])
            continue
        def plane_tile(r, c):
            if c == tiles:
                return l_ref[0, r]
            return o_ref[0, r, :, c * LANES:(c + 1) * LANES].astype(F32)

        for c in range(tiles + 1):
            if dil <= DEINTERLEAVE_STRIDE:
                for r in range(dil):
                    nat[slab + c, pl.ds(r, tm // dil, stride=dil), :] = plane_tile(r, c)
                continue
            groups = DEINTERLEAVE_STRIDE
            inner, rows = dil // groups, tm // groups
            spare = slab + tiles + 1 + c
            for g in range(groups):
                for r in range(inner):
                    nat[spare, pl.ds(g * rows + r, rows // inner, stride=inner), :] = plane_tile(g + groups * r, c)
            for g in range(groups):
                nat[slab + c, pl.ds(g, rows, stride=groups), :] = nat[spare, g * rows:(g + 1) * rows, :]
        o_nat.append([nat[slab + c] for c in range(tiles)])
        lse_nat.append(nat[slab + tiles])
        slab += (tiles + 1) * (1 if dil <= DEINTERLEAVE_STRIDE else 2)

    top = jnp.maximum(jnp.maximum(lse_nat[0], lse_nat[1]), lse_nat[2])
    es = [jnp.exp(l - top) for l in lse_nat]
    r_sum = 1.0 / (es[0] + es[1] + es[2])
    ws = [e * r_sum for e in es]
    lanes_per_head = LANES // B_HEADS_PER_GROUP
    ob_heads = []
    for hd in range(B_HEADS_PER_GROUP):
        acc = None
        for w, o in zip(ws, o_nat):
            term = w[:, hd * lanes_per_head:hd * lanes_per_head + 1] * o[hd]
            acc = term if acc is None else acc + term
        ob_heads.append(acc.astype(BF16))
    ob = jnp.concatenate(ob_heads, axis=1)

    z = None
    for b, o in enumerate((oa_ref[0], ob, om_ref[0])):
        logits = jnp.dot(h, wg_ref[:, b * D_MODEL:(b + 1) * D_MODEL], preferred_element_type=F32)
        gate = jax.nn.sigmoid(logits + bg_ref[b:b + 1, :])
        term = gate * jnp.dot(o, wbr_ref[b], preferred_element_type=F32)
        z = term if z is None else z + term
    y_ref[0] = x + jnp.dot(z.astype(BF16), wout_ref[...], preferred_element_type=F32)


def _merge(x, g_mix, oa, obs, lses, om, w_gate, b_gate, w_branch, w_out):
    n, s, _ = x.shape
    tm = MERGE_ROWS
    cur = lambda b, i: (b, i, 0)
    plane = lambda b, i: (b, 0, i, 0)
    rows = lambda w: pl.BlockSpec((1, tm, w), cur)
    planes = lambda w: [pl.BlockSpec((1, dil, tm // dil, w), plane) for _, dil in B_GROUPS]
    return pl.pallas_call(
        _merge_kernel,
        grid=(n, s // tm),
        in_specs=[rows(D_MODEL), _resident((1, D_MODEL)), rows(A_Q_W)]
                 + planes(B_GROUP_W) + planes(LANES) + [rows(M_W)]
                 + [_resident(w_gate.shape), _resident(b_gate.shape), _resident(w_branch.shape),
                    _resident(w_out.shape)],
        out_specs=rows(D_MODEL),
        out_shape=jax.ShapeDtypeStruct((n, s, D_MODEL), F32),
        scratch_shapes=[pltpu.VMEM((N_MERGE_SLABS, tm, LANES), F32)],
        compiler_params=_params(2),
        name="merge",
    )(x, g_mix, oa, *obs, *lses, om, w_gate, b_gate, w_branch, w_out)


MLP_ROWS = 1024
MLP_FF_CHUNK = 1024


def _mlp_kernel(x_ref, g_ref, wup_ref, wdn_ref, y_ref):
    x = x_ref[...]
    h = _rms_rows(x, g_ref[...]).astype(BF16)
    acc = x
    for lo in range(0, D_FF, MLP_FF_CHUNK):
        u = jnp.dot(h, wup_ref[:, lo:lo + MLP_FF_CHUNK], preferred_element_type=F32)
        a = jnp.square(jnp.maximum(u, 0.0)).astype(BF16)
        acc = acc + jnp.dot(a, wdn_ref[lo:lo + MLP_FF_CHUNK, :], preferred_element_type=F32)
    y_ref[...] = acc


def _mlp(x2d, g_mlp, w_up, w_down):
    t = x2d.shape[0]
    tm = MLP_ROWS
    row = lambda i: (i, 0)
    return pl.pallas_call(
        _mlp_kernel,
        grid=(t // tm,),
        in_specs=[pl.BlockSpec((tm, D_MODEL), row), _resident((1, D_MODEL)),
                  _resident(w_up.shape), _resident(w_down.shape)],
        out_specs=pl.BlockSpec((tm, D_MODEL), row),
        out_shape=jax.ShapeDtypeStruct((t, D_MODEL), F32),
        compiler_params=_params(1),
        name="mlp",
    )(x2d, g_mlp, w_up, w_down)


def _tile2(g):
    return jnp.concatenate([g, g]).reshape(1, LANES)


def _row(g):
    return g.reshape(1, -1)


def _layer(x, mem, g_mix, g_mem, w_in, b_gate, w_mem_kv, gq_a, gk_a, sink_a, gq_b, gk_b, gq_m, gk_m,
           w_branch, w_out, g_mlp, w_up, w_down):
    n, s, _ = x.shape
    bounds = np.cumsum((0, A_Q_W, A_KV_W, A_KV_W, B_W, B_W, B_W, M_W, GATE_W))
    seg = lambda a, b: w_in[:, int(bounds[a]):int(bounds[b])].astype(BF16)
    ws = (seg(0, 1), seg(1, 3), seg(3, 4), seg(4, 5), seg(5, 6), seg(6, 7))
    w_gate = seg(7, 8)
    gq_a, gq_b, gq_m = (gq_a * (A_HEAD_DIM ** -0.5 * LOG2E), gq_b * (B_HEAD_DIM ** -0.5 * LOG2E),
                        gq_m * (M_HEAD_DIM ** -0.5 * LOG2E))
    gains = (_tile2(gq_a), _tile2(gk_a), _row(gq_b), _row(gk_b), _row(gq_m))

    am, *qkv_b = _proj(x, _row(g_mix), ws, gains)
    mk, mv = _mem_kv(mem.reshape(n * N_MEM, D_MODEL), _row(g_mem), w_mem_kv.astype(BF16), _row(gk_m))

    oa = _attn_a(am, sink_a, _logit_bound(gq_a, gk_a, A_HEAD_DIM))
    bound_b = _logit_bound(gq_b, gk_b, B_HEAD_DIM)
    b_out = [_attn_b_group(qkv_b[gi], gi, bound_b) for gi in range(len(B_GROUPS))]
    om = _attn_m(am, mk.reshape(n, N_MEM, M_W), mv.reshape(n, N_MEM, M_W),
                 _logit_bound(gq_m, gk_m, M_HEAD_DIM))

    x1 = _merge(x, _row(g_mix), oa, [o for o, _ in b_out], [l for _, l in b_out], om,
                w_gate, b_gate, w_branch.astype(BF16), w_out.astype(BF16))
    y = _mlp(x1.reshape(n * s, D_MODEL), _row(g_mlp), w_up.astype(BF16), w_down.astype(BF16))
    return y.reshape(n, s, D_MODEL)


def kernel(x_prompt, x_sample, mem_prompt, mem_sample, g_mix, g_mem, w_in, b_gate, w_mem_kv, gq_a, gk_a,
           sink_a, gq_b, gk_b, gq_m, gk_m, w_branch, w_out, g_mlp, w_up, w_down):
    depth = w_in.shape[0]

    def run(x, mem):
        for l in range(depth):
            x = _layer(x, mem, g_mix[l], g_mem[l], w_in[l], b_gate[l], w_mem_kv[l], gq_a[l], gk_a[l],
                       sink_a[l], gq_b[l], gk_b[l], gq_m[l], gk_m[l], w_branch[l], w_out[l], g_mlp[l],
                       w_up[l], w_down[l])
        return x

    return (run(x_prompt, mem_prompt), run(x_sample, mem_sample))
```

```python
import functools

import numpy as np
import jax
import jax.numpy as jnp
from jax import lax
from jax.experimental import pallas as pl
from jax.experimental.pallas import tpu as pltpu

D_MODEL = 1024
N_MEM = 256
A_HEADS = 8
A_KV_HEADS = 2
A_HEAD_DIM = 64
A_HALF_WIN = 128
B_GROUPS = ((128, 1), (512, 4), (2048, 16))
B_HEADS_PER_GROUP = 4
B_HEAD_DIM = 128
M_HEADS = 4
M_HEAD_DIM = 128
N_BRANCH = 3
BRANCH_WIDTH = D_MODEL // 2
D_FF = 4 * D_MODEL
EPS = 1e-6
NEG_INF = -1e30

A_Q_W = A_HEADS * A_HEAD_DIM
A_KV_W = A_KV_HEADS * A_HEAD_DIM
B_GROUP_W = B_HEADS_PER_GROUP * B_HEAD_DIM
B_W = len(B_GROUPS) * B_GROUP_W
M_W = M_HEADS * M_HEAD_DIM
GATE_W = N_BRANCH * D_MODEL

LANES = 128
B_HALF_WIN = 64
VMEM_LIMIT_BYTES = 60000 * 1024

BF16 = jnp.bfloat16
F32 = jnp.float32


def _alibi_slopes(n):
    return [float(2.0 ** (-8.0 * (i + 1) / n)) for i in range(n)]


def _params(n_grid_axes):
    return pltpu.CompilerParams(
        dimension_semantics=("arbitrary",) * n_grid_axes,
        vmem_limit_bytes=VMEM_LIMIT_BYTES)


def _resident(shape):
    zeros = (0,) * len(shape)
    return pl.BlockSpec(shape, lambda *_: zeros, pipeline_mode=pl.Buffered(1))


def _rms_rows(x, gain):
    ms = jnp.mean(x * x, axis=-1, keepdims=True)
    return x * lax.rsqrt(ms + EPS) * gain


def _head_norm128(blk, gain):
    ms = jnp.sum(blk * blk, axis=-1, keepdims=True) * (1.0 / LANES)
    return blk * lax.rsqrt(ms + EPS) * gain


def _head_norm64(blk, gain2):
    low = lax.broadcasted_iota(jnp.int32, blk.shape, 1) < A_HEAD_DIM
    sq = blk * blk
    ss_lo = jnp.sum(jnp.where(low, sq, 0.0), axis=-1, keepdims=True)
    ss_hi = jnp.sum(jnp.where(low, 0.0, sq), axis=-1, keepdims=True)
    ms = jnp.where(low, ss_lo, ss_hi) * (1.0 / A_HEAD_DIM)
    return blk * lax.rsqrt(ms + EPS) * gain2


PROJ_ROWS = 1024
AM_QA, AM_QM, AM_KA, AM_VA = 0, A_Q_W, A_Q_W + M_W, A_Q_W + M_W + A_KV_W
AM_W = A_Q_W + M_W + 2 * A_KV_W
B_QKV_W = 3 * B_GROUP_W
DEINTERLEAVE_STRIDE = 4
N_DILATED_SLABS = 2 * (B_GROUP_W // LANES)


def _proj_kernel(x_ref, g_ref, wqa_ref, wkva_ref, wqb_ref, wkb_ref, wvb_ref, wqm_ref,
                 gqa_ref, gka_ref, gqb_ref, gkb_ref, gqm_ref,
                 am_ref, *rest):
    b_refs, ybuf = rest[:-1], rest[-1]
    tm = x_ref.shape[1]
    h = _rms_rows(x_ref[0], g_ref[...]).astype(BF16)

    def mm(w_ref, lo, hi):
        return lambda: jnp.dot(h, w_ref[:, lo:hi], preferred_element_type=F32)

    def finish_qa(y):
        for c in range(A_Q_W // LANES):
            sl = slice(c * LANES, (c + 1) * LANES)
            am_ref[0, :, AM_QA + c * LANES:AM_QA + (c + 1) * LANES] = (
                _head_norm64(y[:, sl], gqa_ref[...]).astype(BF16))

    def finish_kva(y):
        am_ref[0, :, AM_KA:AM_KA + A_KV_W] = _head_norm64(y[:, :A_KV_W], gka_ref[...]).astype(BF16)
        am_ref[0, :, AM_VA:AM_VA + A_KV_W] = y[:, A_KV_W:].astype(BF16)

    def finish_qm(y):
        for c in range(M_W // LANES):
            sl = slice(c * LANES, (c + 1) * LANES)
            am_ref[0, :, AM_QM + c * LANES:AM_QM + (c + 1) * LANES] = (
                _head_norm128(y[:, sl], gqm_ref[...]).astype(BF16))

    def finish_b(o_ref, col0, gain_ref, dil, slab):
        def finish(y):
            for c in range(B_GROUP_W // LANES):
                sl = slice(c * LANES, (c + 1) * LANES)
                out = slice(col0 + c * LANES, col0 + (c + 1) * LANES)
                blk = y[:, sl]
                if gain_ref is not None:
                    blk = _head_norm128(blk, gain_ref[...])
                if dil == 1:
                    o_ref[0, 0, :, out] = blk.astype(BF16)
                else:
                    ybuf[slab + c] = blk
                    groups, stride = 1, dil
                    while stride > DEINTERLEAVE_STRIDE:
                        rows = tm // groups
                        parts = [ybuf[slab + c, pl.ds(g * rows + r, rows // DEINTERLEAVE_STRIDE,
                                                      stride=DEINTERLEAVE_STRIDE), :]
                                 for g in range(groups) for r in range(DEINTERLEAVE_STRIDE)]
                        ybuf[slab + c] = jnp.concatenate(parts, axis=0)
                        groups, stride = groups * DEINTERLEAVE_STRIDE, stride // DEINTERLEAVE_STRIDE
                    rows = tm // groups
                    for g in range(groups):
                        for r in range(stride):
                            o_ref[0, g + groups * r, :, out] = (
                                ybuf[slab + c, pl.ds(g * rows + r, rows // stride, stride=stride), :].astype(BF16))
        return finish

    def branch_b(gi, slab):
        dil = B_GROUPS[gi][1]
        lo = gi * B_GROUP_W
        parts = []
        for part, (w_ref, gain_ref) in enumerate(((wqb_ref, gqb_ref), (wkb_ref, gkb_ref), (wvb_ref, None))):
            parts.append((mm(w_ref, lo, lo + B_GROUP_W),
                          finish_b(b_refs[gi], part * B_GROUP_W, gain_ref, dil, slab)))
            slab = (slab + B_GROUP_W // LANES) % N_DILATED_SLABS
        return parts, slab

    by_dilation = sorted(range(len(B_GROUPS)), key=lambda gi: -B_GROUPS[gi][1])
    stages, slab = [], 0
    for gi in by_dilation[:-1]:
        parts, slab = branch_b(gi, slab)
        stages += parts
    stages += [(mm(wqa_ref, 0, A_Q_W), finish_qa), (mm(wkva_ref, 0, 2 * A_KV_W), finish_kva),
               (mm(wqm_ref, 0, M_W), finish_qm)]
    stages += branch_b(by_dilation[-1], slab)[0]

    pending = None
    for matmul, finish in stages:
        y = matmul()
        if pending is not None:
            pending[0](pending[1])
        pending = (finish, y)
    pending[0](pending[1])


def _proj(x, g_mix, ws, gains):
    n, s, _ = x.shape
    tm = PROJ_ROWS
    cur = lambda b, i: (b, i, 0)
    plane = lambda b, i: (b, 0, i, 0)
    in_specs = ([pl.BlockSpec((1, tm, D_MODEL), cur), _resident((1, D_MODEL))]
                + [_resident(w.shape) for w in ws]
                + [_resident((1, LANES)) for _ in gains])
    out_specs = [pl.BlockSpec((1, tm, AM_W), cur)]
    out_shape = [jax.ShapeDtypeStruct((n, s, AM_W), BF16)]
    for _, dil in B_GROUPS:
        out_specs.append(pl.BlockSpec((1, dil, tm // dil, B_QKV_W), plane))
        out_shape.append(jax.ShapeDtypeStruct((n, dil, s // dil, B_QKV_W), BF16))
    return pl.pallas_call(
        _proj_kernel,
        grid=(n, s // tm),
        in_specs=in_specs,
        out_specs=out_specs,
        out_shape=out_shape,
        scratch_shapes=[pltpu.VMEM((N_DILATED_SLABS, tm, LANES), F32)],
        compiler_params=_params(2),
        name="proj",
    )(x, g_mix, *ws, *gains)


MEM_ROWS = 1024


def _mem_kv_kernel(m_ref, g_ref, w_ref, gk_ref, k_ref, v_ref):
    h = _rms_rows(m_ref[...], g_ref[...]).astype(BF16)
    y = jnp.dot(h, w_ref[...], preferred_element_type=F32)
    for c in range(M_HEADS):
        sl = slice(c * LANES, (c + 1) * LANES)
        k_ref[:, sl] = _head_norm128(y[:, sl], gk_ref[...]).astype(BF16)
    v_ref[...] = y[:, M_W:].astype(BF16)


def _mem_kv(mem2d, g_mem, w_mem_kv, gk_m):
    t = mem2d.shape[0]
    tm = min(t, MEM_ROWS)
    row = lambda i: (i, 0)
    return pl.pallas_call(
        _mem_kv_kernel,
        grid=(t // tm,),
        in_specs=[pl.BlockSpec((tm, D_MODEL), row), _resident((1, D_MODEL)),
                  _resident(w_mem_kv.shape), _resident((1, LANES))],
        out_specs=[pl.BlockSpec((tm, M_W), row)] * 2,
        out_shape=[jax.ShapeDtypeStruct((t, M_W), BF16)] * 2,
        compiler_params=_params(1),
        name="mem_kv",
    )(mem2d, g_mem, w_mem_kv, gk_m)


A_BLOCK = A_HALF_WIN
A_STEP = 2048
ATTN_CHUNK = 1024
INTERIOR, FIRST, LAST = 0, 1, 2
LOG2E = float(np.log2(np.e))
LN2 = float(np.log(2.0))

SOFTMAX_SHIFT_LIMIT = 40.0
BF16_SLACK = 1.0 + 2.0 ** -6


def _logit_bound(gain_q, gain_k, head_dim):
    return head_dim * jnp.max(jnp.abs(gain_q)) * jnp.max(jnp.abs(gain_k)) * BF16_SLACK


def _softmax_ctl(shifts):
    shifts = jnp.asarray(shifts, F32).reshape(-1)
    fixed = jnp.all(shifts <= SOFTMAX_SHIFT_LIMIT)
    return jnp.concatenate([fixed.astype(F32).reshape(1), shifts])


def _by_softmax_shift(ctl_ref, run):
    fixed = ctl_ref[0] > 0.5

    @pl.when(fixed)
    def _():
        run(True)

    @pl.when(jnp.logical_not(fixed))
    def _():
        run(False)


def _attn_a_init_bias(bias_ref, ctl_ref):
    n_keys = 3 * A_BLOCK
    q_row = lax.broadcasted_iota(jnp.int32, (A_BLOCK, n_keys), 0)
    k_col = lax.broadcasted_iota(jnp.int32, (A_BLOCK, n_keys), 1)
    dist = jnp.abs(q_row + A_BLOCK - k_col)
    dist_f = dist.astype(F32)
    for head, slope in enumerate(_alibi_slopes(A_HEADS)):
        shift = jnp.where(ctl_ref[0] > 0.5, ctl_ref[1 + head], 0.0)
        base = jnp.where(dist <= A_HALF_WIN, (-slope * LOG2E) * dist_f - shift, NEG_INF)
        bias_ref[INTERIOR, head] = base
        bias_ref[FIRST, head] = jnp.where(k_col >= A_BLOCK, base, NEG_INF)
        bias_ref[LAST, head] = jnp.where(k_col < 2 * A_BLOCK, base, NEG_INF)


def _attn_a_load_kv(kv_refs, kbuf, vbuf):
    kp_ref, kc_ref, kn_ref, vp_ref, vc_ref, vn_ref = kv_refs
    rows = kc_ref.shape[1]
    for buf, prev, cur, nxt in ((kbuf, kp_ref, kc_ref, kn_ref), (vbuf, vp_ref, vc_ref, vn_ref)):
        buf[0:A_BLOCK, :] = prev[0]
        buf[A_BLOCK:A_BLOCK + rows, :] = cur[0]
        buf[A_BLOCK + rows:, :] = nxt[0]


def _attn_a_pass(row0, first_block, n_blocks, fixed_shift, q_ref, kbuf, vbuf, bias_ref, sink_ref, ctl_ref,
                 o_ref):
    n_keys = 3 * A_BLOCK
    low = lax.broadcasted_iota(jnp.int32, (n_keys, LANES), 1) < A_HEAD_DIM
    low_q = lax.broadcasted_iota(jnp.int32, (A_BLOCK, LANES), 1) < A_HEAD_DIM
    zero = jnp.zeros((n_keys, LANES), BF16)
    nt = (((1,), (1,)), ((), ()))
    per_chunk = ATTN_CHUNK // A_BLOCK
    units = [(j, kvh) for j in range(per_chunk) for kvh in range(A_KV_HEADS)]
    logits, values, probs = {}, {}, {}

    def swap_halves(t):
        return jnp.concatenate([t[:, A_HEAD_DIM:], t[:, :A_HEAD_DIM]], axis=1)

    def padded(t, t_sw, kvh):
        lo, hi = (t, t_sw) if kvh == 0 else (t_sw, t)
        return jnp.concatenate([jnp.where(low, lo, zero), jnp.where(low, zero, hi)], axis=0)

    def logit_matmuls():
        for j in range(per_chunk):
            rows = pl.ds(row0 + j * A_BLOCK, A_BLOCK)
            kw = kbuf[pl.ds(row0 + j * A_BLOCK, n_keys), :]
            vw = vbuf[pl.ds(row0 + j * A_BLOCK, n_keys), :]
            kw_sw, vw_sw = swap_halves(kw), swap_halves(vw)
            for kvh in range(A_KV_HEADS):
                q2 = jnp.concatenate([q_ref[0, rows, (2 * kvh + c) * LANES:(2 * kvh + c + 1) * LANES]
                                      for c in range(2)], axis=0)
                logits[j, kvh] = lax.dot_general(q2, padded(kw, kw_sw, kvh), nt,
                                                 preferred_element_type=F32)
                values[j, kvh] = padded(vw, vw_sw, kvh)

    def softmaxes():
        for j, kvh in units:
            block = first_block + j
            variant = jnp.where(block == 0, FIRST, jnp.where(block == n_blocks - 1, LAST, INTERIOR))
            s2 = logits[j, kvh]
            p_rows, r_rows = [], []
            for c in range(2):
                ps, sums, sink_gaps = [], [], []
                for half in range(2):
                    head = 2 * (2 * kvh + c) + half
                    sh = s2[c * A_BLOCK:(c + 1) * A_BLOCK, half * n_keys:(half + 1) * n_keys]
                    sh = sh + bias_ref[variant, head]
                    sink = sink_ref[head] * LOG2E
                    if fixed_shift:
                        m = ctl_ref[1 + head]
                        p = jnp.exp2(sh)
                    else:
                        m = jnp.maximum(jnp.max(sh, axis=-1, keepdims=True), sink)
                        p = jnp.exp2(sh - m)
                    ps.append(p.astype(BF16))
                    sums.append(jnp.sum(p, axis=-1, keepdims=True))
                    sink_gaps.append(sink - m)
                sink_lanes = low_q[:1] if fixed_shift else low_q
                den = (jnp.where(low_q, sums[0], sums[1])
                       + jnp.exp2(jnp.where(sink_lanes, sink_gaps[0], sink_gaps[1])))
                p_rows.append(jnp.concatenate(ps, axis=1))
                r_rows.append(1.0 / den)
            probs[j, kvh] = (jnp.concatenate(p_rows, axis=0), jnp.concatenate(r_rows, axis=0))

    def value_matmuls():
        for j, kvh in units:
            p2, r2 = probs[j, kvh]
            o2 = jnp.dot(p2, values[j, kvh], preferred_element_type=F32) * r2
            for c in range(2):
                tile = 2 * kvh + c
                o_ref[0, pl.ds(row0 + j * A_BLOCK, A_BLOCK), tile * LANES:(tile + 1) * LANES] = (
                    o2[c * A_BLOCK:(c + 1) * A_BLOCK].astype(BF16))

    return logit_matmuls, softmaxes, value_matmuls


def _attn_a_kernel(seq_len, q_ref, kp_ref, kc_ref, kn_ref, vp_ref, vc_ref, vn_ref, sink_ref, ctl_ref,
                   o_ref, kbuf, vbuf, bias_ref):
    i = pl.program_id(1)
    step_rows = q_ref.shape[1]

    @pl.when((pl.program_id(0) == 0) & (i == 0))
    def _():
        _attn_a_init_bias(bias_ref, ctl_ref)

    _attn_a_load_kv((kp_ref, kc_ref, kn_ref, vp_ref, vc_ref, vn_ref), kbuf, vbuf)

    def run(fixed_shift):
        def chunk(t, carry):
            row0 = pl.multiple_of(t * ATTN_CHUNK, ATTN_CHUNK)
            for phase in _attn_a_pass(row0, (i * step_rows + row0) // A_BLOCK, seq_len // A_BLOCK,
                                      fixed_shift, q_ref, kbuf, vbuf, bias_ref, sink_ref, ctl_ref, o_ref):
                phase()
            return carry

        lax.fori_loop(0, step_rows // ATTN_CHUNK, chunk, 0)

    _by_softmax_shift(ctl_ref, run)


def _attn_a_specs(s, step_rows, position):
    per_step = step_rows // A_BLOCK
    n_blocks = s // A_BLOCK
    assert n_blocks >= 2 and s % step_rows == 0

    def cur(col):
        def index(*ids):
            b, i = position(*ids)
            return (b, i, col)
        return index

    def prev(col):
        def index(*ids):
            b, i = position(*ids)
            return (b, jnp.maximum(i * per_step - 1, 0), col)
        return index

    def nxt(col):
        def index(*ids):
            b, i = position(*ids)
            return (b, jnp.minimum((i + 1) * per_step, n_blocks - 1), col)
        return index

    q_spec = pl.BlockSpec((1, step_rows, A_Q_W), cur(AM_QA // A_Q_W))
    o_spec = pl.BlockSpec((1, step_rows, A_Q_W), cur(0))
    kv_specs = [[pl.BlockSpec((1, A_BLOCK, A_KV_W), prev(col // A_KV_W)),
                 pl.BlockSpec((1, step_rows, A_KV_W), cur(col // A_KV_W)),
                 pl.BlockSpec((1, A_BLOCK, A_KV_W), nxt(col // A_KV_W))] for col in (AM_KA, AM_VA)]
    scratch = [pltpu.VMEM((step_rows + 2 * A_BLOCK, A_KV_W), BF16)] * 2 + [
        pltpu.VMEM((3, A_HEADS, A_BLOCK, 3 * A_BLOCK), F32)]
    return q_spec, o_spec, kv_specs, scratch


def _attn_a(am, sink, logit_bound):
    n, s, _ = am.shape
    step = min(s, A_STEP)
    q_spec, o_spec, kv_specs, scratch = _attn_a_specs(s, step, lambda b, i: (b, i))
    smem = pl.BlockSpec(memory_space=pltpu.SMEM)
    ctl = _softmax_ctl(jnp.maximum(logit_bound, sink * LOG2E))
    return pl.pallas_call(
        functools.partial(_attn_a_kernel, s),
        grid=(n, s // step),
        in_specs=[q_spec] + kv_specs[0] + kv_specs[1] + [smem, smem],
        out_specs=o_spec,
        out_shape=jax.ShapeDtypeStruct((n, s, A_Q_W), BF16),
        scratch_shapes=scratch,
        compiler_params=_params(2),
        name="attn_a",
    )(*[am] * 7, sink, ctl)


B_BLOCK = 2 * B_HALF_WIN
B_KEYS = B_BLOCK + 2 * B_HALF_WIN
B_ROWS_PER_STEP = 2048


def _attn_b_kernel(sub_len, slopes, q_ref, kp_ref, kc_ref, kn_ref, vp_ref, vc_ref, vn_ref, ctl_ref,
                   o_ref, lse_ref, kbuf, vbuf, bias_ref):
    i = pl.program_id(2)
    n_planes, step = q_ref.shape[1], q_ref.shape[2]
    n_blocks = sub_len // B_BLOCK

    @pl.when((pl.program_id(0) == 0) & (pl.program_id(1) == 0) & (i == 0))
    def _():
        shift = jnp.where(ctl_ref[0] > 0.5, ctl_ref[1], 0.0)
        q_row = lax.broadcasted_iota(jnp.int32, (B_BLOCK, B_KEYS), 0)
        k_col = lax.broadcasted_iota(jnp.int32, (B_BLOCK, B_KEYS), 1)
        dist = jnp.abs(q_row + B_HALF_WIN - k_col)
        dist_f = dist.astype(F32)
        for head, slope in enumerate(slopes):
            base = jnp.where(dist <= B_HALF_WIN, (-slope * LOG2E) * dist_f - shift, NEG_INF)
            first = jnp.where(k_col >= B_HALF_WIN, base, NEG_INF)
            bias_ref[0, head] = base
            bias_ref[1, head] = first
            bias_ref[2, head] = jnp.where(k_col < B_HALF_WIN + B_BLOCK, base, NEG_INF)
            bias_ref[3, head] = jnp.where(k_col < B_HALF_WIN + B_BLOCK, first, NEG_INF)

    for buf, prev, cur, nxt in ((kbuf, kp_ref, kc_ref, kn_ref), (vbuf, vp_ref, vc_ref, vn_ref)):
        for r in range(n_planes):
            buf[r, 0:B_HALF_WIN, :] = prev[0, r]
            buf[r, B_HALF_WIN:B_HALF_WIN + step, :] = cur[0, r]
            buf[r, B_HALF_WIN + step:, :] = nxt[0, r]

    lane = lax.broadcasted_iota(jnp.int32, (B_BLOCK, LANES), 1)
    lanes_per_head = LANES // B_HEADS_PER_GROUP
    heads = [slice(h * B_HEAD_DIM, (h + 1) * B_HEAD_DIM) for h in range(B_HEADS_PER_GROUP)]
    nt = (((1,), (1,)), ((), ()))
    rows_per_pass = min(step, ATTN_CHUNK)
    planes_per_pass = ATTN_CHUNK // rows_per_pass
    passes_per_plane = step // rows_per_pass
    blocks = [(dr, j) for dr in range(planes_per_pass) for j in range(rows_per_pass // B_BLOCK)]

    def one_pass(fixed_shift, t, carry):
        if planes_per_pass == 1:
            plane0 = t // passes_per_plane
            row0 = pl.multiple_of((t % passes_per_plane) * ATTN_CHUNK, ATTN_CHUNK)
        else:
            plane0, row0 = t * planes_per_pass, 0

        logits = {}
        for dr, j in blocks:
            for h, sl in enumerate(heads):
                q = q_ref[0, plane0 + dr, pl.ds(row0 + j * B_BLOCK, B_BLOCK), sl]
                k = kbuf[plane0 + dr, pl.ds(row0 + j * B_BLOCK, B_KEYS), sl]
                logits[dr, j, h] = lax.dot_general(q, k, nt, preferred_element_type=F32)

        probs = {}
        for dr, j in blocks:
            block = (i * step + row0) // B_BLOCK + j
            variant = (block == 0).astype(jnp.int32) + 2 * (block == n_blocks - 1).astype(jnp.int32)
            m_tile = den_tile = None
            for h in range(B_HEADS_PER_GROUP):
                s = logits[dr, j, h] + bias_ref[variant, h]
                if fixed_shift:
                    m = ctl_ref[1]
                    p = jnp.exp2(s)
                else:
                    m = jnp.max(s, axis=-1, keepdims=True)
                    p = jnp.exp2(s - m)
                den = jnp.sum(p, axis=-1, keepdims=True)
                probs[dr, j, h] = (p.astype(BF16), 1.0 / den)
                den_tile = den if h == 0 else jnp.where(lane >= h * lanes_per_head, den, den_tile)
                if not fixed_shift:
                    m_tile = m if h == 0 else jnp.where(lane >= h * lanes_per_head, m, m_tile)
            lse_ref[0, plane0 + dr, pl.ds(row0 + j * B_BLOCK, B_BLOCK), :] = (
                ((m if fixed_shift else m_tile) + jnp.log2(den_tile)) * LN2)

        for dr, j in blocks:
            for h, sl in enumerate(heads):
                p, rden = probs[dr, j, h]
                v = vbuf[plane0 + dr, pl.ds(row0 + j * B_BLOCK, B_KEYS), sl]
                o = jnp.dot(p, v, preferred_element_type=F32) * rden
                o_ref[0, plane0 + dr, pl.ds(row0 + j * B_BLOCK, B_BLOCK), sl] = o.astype(BF16)
        return carry

    _by_softmax_shift(ctl_ref, lambda fixed_shift: lax.fori_loop(
        0, n_planes * step // ATTN_CHUNK, functools.partial(one_pass, fixed_shift), 0))


def _attn_b_group(qkv, gi, logit_bound):
    n, dil, sub, _ = qkv.shape
    step = min(sub, B_ROWS_PER_STEP)
    n_planes = B_ROWS_PER_STEP // step
    assert sub % step == 0 and dil % n_planes == 0 and step % B_BLOCK == 0
    assert ATTN_CHUNK % step == 0 or step % ATTN_CHUNK == 0
    halo_per_step = step // B_HALF_WIN
    last_halo = sub // B_HALF_WIN - 1
    cur = lambda col: lambda b, r, i: (b, r, i, col)
    prev = lambda col: lambda b, r, i: (b, r, jnp.maximum(i * halo_per_step - 1, 0), col)
    nxt = lambda col: lambda b, r, i: (b, r, jnp.minimum((i + 1) * halo_per_step, last_halo), col)
    kv_specs = [[pl.BlockSpec((1, n_planes, B_HALF_WIN, B_GROUP_W), prev(col)),
                 pl.BlockSpec((1, n_planes, step, B_GROUP_W), cur(col)),
                 pl.BlockSpec((1, n_planes, B_HALF_WIN, B_GROUP_W), nxt(col))] for col in (1, 2)]
    all_slopes = _alibi_slopes(len(B_GROUPS) * B_HEADS_PER_GROUP)
    slopes = [all_slopes[gi * B_HEADS_PER_GROUP + h] * dil for h in range(B_HEADS_PER_GROUP)]
    return pl.pallas_call(
        functools.partial(_attn_b_kernel, sub, slopes),
        grid=(n, dil // n_planes, sub // step),
        in_specs=[pl.BlockSpec((1, n_planes, step, B_GROUP_W), cur(0))] + kv_specs[0] + kv_specs[1]
                 + [pl.BlockSpec(memory_space=pltpu.SMEM)],
        out_specs=[pl.BlockSpec((1, n_planes, step, B_GROUP_W), cur(0)),
                   pl.BlockSpec((1, n_planes, step, LANES), cur(0))],
        out_shape=[jax.ShapeDtypeStruct((n, dil, sub, B_GROUP_W), BF16),
                   jax.ShapeDtypeStruct((n, dil, sub, LANES), F32)],
        scratch_shapes=[pltpu.VMEM((n_planes, step + 2 * B_HALF_WIN, B_GROUP_W), BF16)] * 2
                       + [pltpu.VMEM((4, B_HEADS_PER_GROUP, B_BLOCK, B_KEYS), F32)],
        compiler_params=_params(3),
        name=f"attn_b{gi}",
    )(*[qkv] * 7, _softmax_ctl(logit_bound))


M_STEP = 2048


def _attn_m_kernel(q_ref, k_ref, v_ref, ctl_ref, o_ref):
    heads = [slice(h * M_HEAD_DIM, (h + 1) * M_HEAD_DIM) for h in range(M_HEADS)]
    nt = (((1,), (1,)), ((), ()))

    def chunk(fixed_shift, t, carry):
        rows = pl.ds(pl.multiple_of(t * ATTN_CHUNK, ATTN_CHUNK), ATTN_CHUNK)
        logits = [lax.dot_general(q_ref[0, rows, sl], k_ref[0, :, sl], nt, preferred_element_type=F32)
                  for sl in heads]
        probs = []
        for s in logits:
            m = ctl_ref[1] if fixed_shift else jnp.max(s, axis=-1, keepdims=True)
            p = jnp.exp2(s - m)
            probs.append((p.astype(BF16), 1.0 / jnp.sum(p, axis=-1, keepdims=True)))
        for sl, (p, rden) in zip(heads, probs):
            o = jnp.dot(p, v_ref[0, :, sl], preferred_element_type=F32) * rden
            o_ref[0, rows, sl] = o.astype(BF16)
        return carry

    _by_softmax_shift(ctl_ref, lambda fixed_shift: lax.fori_loop(
        0, q_ref.shape[1] // ATTN_CHUNK, functools.partial(chunk, fixed_shift), 0))


def _attn_m(am, mk, mv, logit_bound):
    n, s, _ = am.shape
    step = min(s, M_STEP)
    cur = lambda b, i: (b, i, 0)
    q_cols = lambda b, i: (b, i, AM_QM // M_W)
    mem = lambda b, i: (b, 0, 0)
    return pl.pallas_call(
        _attn_m_kernel,
        grid=(n, s // step),
        in_specs=[pl.BlockSpec((1, step, M_W), q_cols), pl.BlockSpec((1, N_MEM, M_W), mem),
                  pl.BlockSpec((1, N_MEM, M_W), mem), pl.BlockSpec(memory_space=pltpu.SMEM)],
        out_specs=pl.BlockSpec((1, step, M_W), cur),
        out_shape=jax.ShapeDtypeStruct((n, s, M_W), BF16),
        compiler_params=_params(2),
        name="attn_m",
    )(am, mk, mv, _softmax_ctl(logit_bound))


MERGE_ROWS = 1024
assert all(dil <= DEINTERLEAVE_STRIDE ** 2 for _, dil in B_GROUPS)
MERGE_SPARE_SLABS = 2
N_MERGE_SLABS = sum(B_GROUP_W // LANES + 1 + (0 if dil <= DEINTERLEAVE_STRIDE else MERGE_SPARE_SLABS)
                    for _, dil in B_GROUPS if dil > 1)


def _merge_kernel(x_ref, g_ref, oa_ref, o0_ref, o1_ref, o2_ref, l0_ref, l1_ref, l2_ref, om_ref,
                  wg_ref, bg_ref, wbr_ref, wout_ref, y_ref, nat):
    tm = x_ref.shape[1]
    x = x_ref[0]
    h = _rms_rows(x, g_ref[...]).astype(BF16)
    tiles = B_GROUP_W // LANES

    o_nat, lse_nat = [], []
    slab = 0
    for (_, dil), o_ref, l_ref in zip(B_GROUPS, (o0_ref, o1_ref, o2_ref), (l0_ref, l1_ref, l2_ref)):
        if dil == 1:
            o_nat.append([o_ref[0, 0, :, c * LANES:(c + 1) * LANES].astype(F32) for c in range(tiles)])
            lse_nat.append(l_ref[0, 0])
            continue
        def plane_tile(r, c):
            if c == tiles:
                return l_ref[0, r]
            return o_ref[0, r, :, c * LANES:(c + 1) * LANES].astype(F32)

        for c in range(tiles + 1):
            if dil <= DEINTERLEAVE_STRIDE:
                for r in range(dil):
                    nat[slab + c, pl.ds(r, tm // dil, stride=dil), :] = plane_tile(r, c)
                continue
            groups = DEINTERLEAVE_STRIDE
            inner, rows = dil // groups, tm // groups
            spare = slab + tiles + 1 + c % MERGE_SPARE_SLABS
            for g in range(groups):
                for r in range(inner):
                    nat[spare, pl.ds(g * rows + r, rows // inner, stride=inner), :] = plane_tile(g + groups * r, c)
            for g in range(groups):
                nat[slab + c, pl.ds(g, rows, stride=groups), :] = nat[spare, g * rows:(g + 1) * rows, :]
        o_nat.append([nat[slab + c] for c in range(tiles)])
        lse_nat.append(nat[slab + tiles])
        slab += tiles + 1 + (0 if dil <= DEINTERLEAVE_STRIDE else MERGE_SPARE_SLABS)

    top = jnp.maximum(jnp.maximum(lse_nat[0], lse_nat[1]), lse_nat[2])
    es = [jnp.exp(l - top) for l in lse_nat]
    r_sum = 1.0 / (es[0] + es[1] + es[2])
    ws = [e * r_sum for e in es]
    lanes_per_head = LANES // B_HEADS_PER_GROUP
    ob_heads = []
    for hd in range(B_HEADS_PER_GROUP):
        acc = None
        for w, o in zip(ws, o_nat):
            term = w[:, hd * lanes_per_head:hd * lanes_per_head + 1] * o[hd]
            acc = term if acc is None else acc + term
        ob_heads.append(acc.astype(BF16))
    ob = jnp.concatenate(ob_heads, axis=1)

    z = None
    for b, o in enumerate((oa_ref[0], ob, om_ref[0])):
        logits = jnp.dot(h, wg_ref[:, b * D_MODEL:(b + 1) * D_MODEL], preferred_element_type=F32)
        gate = jax.nn.sigmoid(logits + bg_ref[b:b + 1, :])
        term = gate * jnp.dot(o, wbr_ref[b], preferred_element_type=F32)
        z = term if z is None else z + term
    y_ref[0] = x + jnp.dot(z.astype(BF16), wout_ref[...], preferred_element_type=F32)


def _merge(x, g_mix, oa, obs, lses, om, w_gate, b_gate, w_branch, w_out):
    n, s, _ = x.shape
    tm = MERGE_ROWS
    cur = lambda b, i: (b, i, 0)
    plane = lambda b, i: (b, 0, i, 0)
    rows = lambda w: pl.BlockSpec((1, tm, w), cur)
    planes = lambda w: [pl.BlockSpec((1, dil, tm // dil, w), plane) for _, dil in B_GROUPS]
    return pl.pallas_call(
        _merge_kernel,
        grid=(n, s // tm),
        in_specs=[rows(D_MODEL), _resident((1, D_MODEL)), rows(A_Q_W)]
                 + planes(B_GROUP_W) + planes(LANES) + [rows(M_W)]
                 + [_resident(w_gate.shape), _resident(b_gate.shape), _resident(w_branch.shape),
                    _resident(w_out.shape)],
        out_specs=rows(D_MODEL),
        out_shape=jax.ShapeDtypeStruct((n, s, D_MODEL), F32),
        scratch_shapes=[pltpu.VMEM((N_MERGE_SLABS, tm, LANES), F32)],
        compiler_params=_params(2),
        name="merge",
    )(x, g_mix, oa, *obs, *lses, om, w_gate, b_gate, w_branch, w_out)


MLP_ROWS = 1024
MLP_FF_CHUNK = 1024


def _mlp_kernel(x_ref, g_ref, wup_ref, wdn_ref, y_ref):
    x = x_ref[...]
    h = _rms_rows(x, g_ref[...]).astype(BF16)
    acc = x
    for lo in range(0, D_FF, MLP_FF_CHUNK):
        u = jnp.dot(h, wup_ref[:, lo:lo + MLP_FF_CHUNK], preferred_element_type=F32)
        a = jnp.square(jnp.maximum(u, 0.0)).astype(BF16)
        acc = acc + jnp.dot(a, wdn_ref[lo:lo + MLP_FF_CHUNK, :], preferred_element_type=F32)
    y_ref[...] = acc


def _mlp(x2d, g_mlp, w_up, w_down):
    t = x2d.shape[0]
    tm = MLP_ROWS
    row = lambda i: (i, 0)
    return pl.pallas_call(
        _mlp_kernel,
        grid=(t // tm,),
        in_specs=[pl.BlockSpec((tm, D_MODEL), row), _resident((1, D_MODEL)),
                  _resident(w_up.shape), _resident(w_down.shape)],
        out_specs=pl.BlockSpec((tm, D_MODEL), row),
        out_shape=jax.ShapeDtypeStruct((t, D_MODEL), F32),
        compiler_params=_params(1),
        name="mlp",
    )(x2d, g_mlp, w_up, w_down)


def _tile2(g):
    return jnp.concatenate([g, g]).reshape(1, LANES)


def _row(g):
    return g.reshape(1, -1)


def _layer(x, mem, g_mix, g_mem, w_in, b_gate, w_mem_kv, gq_a, gk_a, sink_a, gq_b, gk_b, gq_m, gk_m,
           w_branch, w_out, g_mlp, w_up, w_down):
    n, s, _ = x.shape
    bounds = np.cumsum((0, A_Q_W, A_KV_W, A_KV_W, B_W, B_W, B_W, M_W, GATE_W))
    seg = lambda a, b: w_in[:, int(bounds[a]):int(bounds[b])].astype(BF16)
    ws = (seg(0, 1), seg(1, 3), seg(3, 4), seg(4, 5), seg(5, 6), seg(6, 7))
    w_gate = seg(7, 8)
    gq_a, gq_b, gq_m = (gq_a * (A_HEAD_DIM ** -0.5 * LOG2E), gq_b * (B_HEAD_DIM ** -0.5 * LOG2E),
                        gq_m * (M_HEAD_DIM ** -0.5 * LOG2E))
    gains = (_tile2(gq_a), _tile2(gk_a), _row(gq_b), _row(gk_b), _row(gq_m))

    am, *qkv_b = _proj(x, _row(g_mix), ws, gains)
    mk, mv = _mem_kv(mem.reshape(n * N_MEM, D_MODEL), _row(g_mem), w_mem_kv.astype(BF16), _row(gk_m))

    oa = _attn_a(am, sink_a, _logit_bound(gq_a, gk_a, A_HEAD_DIM))
    bound_b = _logit_bound(gq_b, gk_b, B_HEAD_DIM)
    b_out = [_attn_b_group(qkv_b[gi], gi, bound_b) for gi in range(len(B_GROUPS))]
    om = _attn_m(am, mk.reshape(n, N_MEM, M_W), mv.reshape(n, N_MEM, M_W),
                 _logit_bound(gq_m, gk_m, M_HEAD_DIM))

    x1 = _merge(x, _row(g_mix), oa, [o for o, _ in b_out], [l for _, l in b_out], om,
                w_gate, b_gate, w_branch.astype(BF16), w_out.astype(BF16))
    y = _mlp(x1.reshape(n * s, D_MODEL), _row(g_mlp), w_up.astype(BF16), w_down.astype(BF16))
    return y.reshape(n, s, D_MODEL)


def kernel(x_prompt, x_sample, mem_prompt, mem_sample, g_mix, g_mem, w_in, b_gate, w_mem_kv, gq_a, gk_a,
           sink_a, gq_b, gk_b, gq_m, gk_m, w_branch, w_out, g_mlp, w_up, w_down):
    depth = w_in.shape[0]

    def run(x, mem):
        for l in range(depth):
            x = _layer(x, mem, g_mix[l], g_mem[l], w_in[l], b_gate[l], w_mem_kv[l], gq_a[l], gk_a[l],
                       sink_a[l], gq_b[l], gk_b[l], gq_m[l], gk_m[l], w_branch[l], w_out[l], g_mlp[l],
                       w_up[l], w_down[l])
        return x

    return (run(x_prompt, mem_prompt), run(x_sample, mem_sample))
```

```python
import functools

import numpy as np
import jax
import jax.numpy as jnp
from jax import lax
from jax.experimental import pallas as pl
from jax.experimental.pallas import tpu as pltpu

D_MODEL = 1024
N_MEM = 256
A_HEADS = 8
A_KV_HEADS = 2
A_HEAD_DIM = 64
A_HALF_WIN = 128
B_GROUPS = ((128, 1), (512, 4), (2048, 16))
B_HEADS_PER_GROUP = 4
B_HEAD_DIM = 128
M_HEADS = 4
M_HEAD_DIM = 128
N_BRANCH = 3
BRANCH_WIDTH = D_MODEL // 2
D_FF = 4 * D_MODEL
EPS = 1e-6
NEG_INF = -1e30

A_Q_W = A_HEADS * A_HEAD_DIM
A_KV_W = A_KV_HEADS * A_HEAD_DIM
B_GROUP_W = B_HEADS_PER_GROUP * B_HEAD_DIM
B_W = len(B_GROUPS) * B_GROUP_W
M_W = M_HEADS * M_HEAD_DIM
GATE_W = N_BRANCH * D_MODEL

LANES = 128
B_HALF_WIN = 64
VMEM_LIMIT_BYTES = 56 * 1024 * 1024

BF16 = jnp.bfloat16
F32 = jnp.float32


def _alibi_slopes(n):
    return [float(2.0 ** (-8.0 * (i + 1) / n)) for i in range(n)]


def _params(n_grid_axes):
    return pltpu.CompilerParams(
        dimension_semantics=("arbitrary",) * n_grid_axes,
        vmem_limit_bytes=VMEM_LIMIT_BYTES)


def _resident(shape):
    zeros = (0,) * len(shape)
    return pl.BlockSpec(shape, lambda *_: zeros, pipeline_mode=pl.Buffered(1))


def _rms_rows(x, gain):
    ms = jnp.mean(x * x, axis=-1, keepdims=True)
    return x * lax.rsqrt(ms + EPS) * gain


def _head_norm128(blk, gain):
    ms = jnp.sum(blk * blk, axis=-1, keepdims=True) * (1.0 / LANES)
    return blk * lax.rsqrt(ms + EPS) * gain


def _head_norm64(blk, gain2):
    low = lax.broadcasted_iota(jnp.int32, blk.shape, 1) < A_HEAD_DIM
    sq = blk * blk
    ss_lo = jnp.sum(jnp.where(low, sq, 0.0), axis=-1, keepdims=True)
    ss_hi = jnp.sum(jnp.where(low, 0.0, sq), axis=-1, keepdims=True)
    ms = jnp.where(low, ss_lo, ss_hi) * (1.0 / A_HEAD_DIM)
    return blk * lax.rsqrt(ms + EPS) * gain2


PROJ_ROWS = 1024
AM_QA, AM_QM, AM_KA, AM_VA = 0, A_Q_W, A_Q_W + M_W, A_Q_W + M_W + A_KV_W
AM_W = A_Q_W + M_W + 2 * A_KV_W
B_QKV_W = 3 * B_GROUP_W
DEINTERLEAVE_STRIDE = 4
N_DILATED_SLABS = 2 * (B_GROUP_W // LANES)


def _proj_kernel(x_ref, g_ref, wqa_ref, wkva_ref, wqb_ref, wkb_ref, wvb_ref, wqm_ref,
                 gqa_ref, gka_ref, gqb_ref, gkb_ref, gqm_ref,
                 am_ref, *rest):
    b_refs, ybuf = rest[:-1], rest[-1]
    tm = x_ref.shape[1]
    h = _rms_rows(x_ref[0], g_ref[...]).astype(BF16)

    def mm(w_ref, lo, hi):
        return lambda: jnp.dot(h, w_ref[:, lo:hi], preferred_element_type=F32)

    def finish_qa(y):
        for c in range(A_Q_W // LANES):
            sl = slice(c * LANES, (c + 1) * LANES)
            am_ref[0, :, AM_QA + c * LANES:AM_QA + (c + 1) * LANES] = (
                _head_norm64(y[:, sl], gqa_ref[...]).astype(BF16))

    def finish_kva(y):
        am_ref[0, :, AM_KA:AM_KA + A_KV_W] = _head_norm64(y[:, :A_KV_W], gka_ref[...]).astype(BF16)
        am_ref[0, :, AM_VA:AM_VA + A_KV_W] = y[:, A_KV_W:].astype(BF16)

    def finish_qm(y):
        for c in range(M_W // LANES):
            sl = slice(c * LANES, (c + 1) * LANES)
            am_ref[0, :, AM_QM + c * LANES:AM_QM + (c + 1) * LANES] = (
                _head_norm128(y[:, sl], gqm_ref[...]).astype(BF16))

    def finish_b(o_ref, col0, gain_ref, dil, slab):
        def finish(y):
            for c in range(B_GROUP_W // LANES):
                sl = slice(c * LANES, (c + 1) * LANES)
                out = slice(col0 + c * LANES, col0 + (c + 1) * LANES)
                blk = y[:, sl]
                if gain_ref is not None:
                    blk = _head_norm128(blk, gain_ref[...])
                if dil == 1:
                    o_ref[0, 0, :, out] = blk.astype(BF16)
                else:
                    ybuf[slab + c] = blk
                    groups, stride = 1, dil
                    while stride > DEINTERLEAVE_STRIDE:
                        rows = tm // groups
                        parts = [ybuf[slab + c, pl.ds(g * rows + r, rows // DEINTERLEAVE_STRIDE,
                                                      stride=DEINTERLEAVE_STRIDE), :]
                                 for g in range(groups) for r in range(DEINTERLEAVE_STRIDE)]
                        ybuf[slab + c] = jnp.concatenate(parts, axis=0)
                        groups, stride = groups * DEINTERLEAVE_STRIDE, stride // DEINTERLEAVE_STRIDE
                    rows = tm // groups
                    for g in range(groups):
                        for r in range(stride):
                            o_ref[0, g + groups * r, :, out] = (
                                ybuf[slab + c, pl.ds(g * rows + r, rows // stride, stride=stride), :].astype(BF16))
        return finish

    def branch_b(gi, slab):
        dil = B_GROUPS[gi][1]
        lo = gi * B_GROUP_W
        parts = []
        for part, (w_ref, gain_ref) in enumerate(((wqb_ref, gqb_ref), (wkb_ref, gkb_ref), (wvb_ref, None))):
            parts.append((mm(w_ref, lo, lo + B_GROUP_W),
                          finish_b(b_refs[gi], part * B_GROUP_W, gain_ref, dil, slab)))
            slab = (slab + B_GROUP_W // LANES) % N_DILATED_SLABS
        return parts, slab

    by_dilation = sorted(range(len(B_GROUPS)), key=lambda gi: -B_GROUPS[gi][1])
    stages, slab = [], 0
    for gi in by_dilation[:-1]:
        parts, slab = branch_b(gi, slab)
        stages += parts
    stages += [(mm(wqa_ref, 0, A_Q_W), finish_qa), (mm(wkva_ref, 0, 2 * A_KV_W), finish_kva),
               (mm(wqm_ref, 0, M_W), finish_qm)]
    stages += branch_b(by_dilation[-1], slab)[0]

    pending = None
    for matmul, finish in stages:
        y = matmul()
        if pending is not None:
            pending[0](pending[1])
        pending = (finish, y)
    pending[0](pending[1])


def _proj(x, g_mix, ws, gains):
    n, s, _ = x.shape
    tm = PROJ_ROWS
    cur = lambda b, i: (b, i, 0)
    plane = lambda b, i: (b, 0, i, 0)
    in_specs = ([pl.BlockSpec((1, tm, D_MODEL), cur), _resident((1, D_MODEL))]
                + [_resident(w.shape) for w in ws]
                + [_resident((1, LANES)) for _ in gains])
    out_specs = [pl.BlockSpec((1, tm, AM_W), cur)]
    out_shape = [jax.ShapeDtypeStruct((n, s, AM_W), BF16)]
    for _, dil in B_GROUPS:
        out_specs.append(pl.BlockSpec((1, dil, tm // dil, B_QKV_W), plane))
        out_shape.append(jax.ShapeDtypeStruct((n, dil, s // dil, B_QKV_W), BF16))
    return pl.pallas_call(
        _proj_kernel,
        grid=(n, s // tm),
        in_specs=in_specs,
        out_specs=out_specs,
        out_shape=out_shape,
        scratch_shapes=[pltpu.VMEM((N_DILATED_SLABS, tm, LANES), F32)],
        compiler_params=_params(2),
        name="proj",
    )(x, g_mix, *ws, *gains)


MEM_ROWS = 1024


def _mem_kv_kernel(m_ref, g_ref, w_ref, gk_ref, k_ref, v_ref):
    h = _rms_rows(m_ref[...], g_ref[...]).astype(BF16)
    y = jnp.dot(h, w_ref[...], preferred_element_type=F32)
    for c in range(M_HEADS):
        sl = slice(c * LANES, (c + 1) * LANES)
        k_ref[:, sl] = _head_norm128(y[:, sl], gk_ref[...]).astype(BF16)
    v_ref[...] = y[:, M_W:].astype(BF16)


def _mem_kv(mem2d, g_mem, w_mem_kv, gk_m):
    t = mem2d.shape[0]
    tm = min(t, MEM_ROWS)
    row = lambda i: (i, 0)
    return pl.pallas_call(
        _mem_kv_kernel,
        grid=(t // tm,),
        in_specs=[pl.BlockSpec((tm, D_MODEL), row), _resident((1, D_MODEL)),
                  _resident(w_mem_kv.shape), _resident((1, LANES))],
        out_specs=[pl.BlockSpec((tm, M_W), row)] * 2,
        out_shape=[jax.ShapeDtypeStruct((t, M_W), BF16)] * 2,
        compiler_params=_params(1),
        name="mem_kv",
    )(mem2d, g_mem, w_mem_kv, gk_m)


A_BLOCK = A_HALF_WIN
A_STEP = 2048
ATTN_CHUNK = 1024
INTERIOR, FIRST, LAST = 0, 1, 2
LOG2E = float(np.log2(np.e))
LN2 = float(np.log(2.0))

SOFTMAX_SHIFT_LIMIT = 40.0
BF16_SLACK = 1.0 + 2.0 ** -6


def _logit_bound(gain_q, gain_k, head_dim):
    return head_dim * jnp.max(jnp.abs(gain_q)) * jnp.max(jnp.abs(gain_k)) * BF16_SLACK


def _softmax_ctl(shifts):
    shifts = jnp.asarray(shifts, F32).reshape(-1)
    fixed = jnp.all(shifts <= SOFTMAX_SHIFT_LIMIT)
    return jnp.concatenate([fixed.astype(F32).reshape(1), shifts])


def _by_softmax_shift(ctl_ref, run):
    fixed = ctl_ref[0] > 0.5

    @pl.when(fixed)
    def _():
        run(True)

    @pl.when(jnp.logical_not(fixed))
    def _():
        run(False)


def _attn_a_init_bias(bias_ref, ctl_ref):
    n_keys = 3 * A_BLOCK
    q_row = lax.broadcasted_iota(jnp.int32, (A_BLOCK, n_keys), 0)
    k_col = lax.broadcasted_iota(jnp.int32, (A_BLOCK, n_keys), 1)
    dist = jnp.abs(q_row + A_BLOCK - k_col)
    dist_f = dist.astype(F32)
    for head, slope in enumerate(_alibi_slopes(A_HEADS)):
        shift = jnp.where(ctl_ref[0] > 0.5, ctl_ref[1 + head], 0.0)
        base = jnp.where(dist <= A_HALF_WIN, (-slope * LOG2E) * dist_f - shift, NEG_INF)
        bias_ref[INTERIOR, head] = base
        bias_ref[FIRST, head] = jnp.where(k_col >= A_BLOCK, base, NEG_INF)
        bias_ref[LAST, head] = jnp.where(k_col < 2 * A_BLOCK, base, NEG_INF)


def _attn_a_load_kv(kv_refs, kbuf, vbuf):
    kp_ref, kc_ref, kn_ref, vp_ref, vc_ref, vn_ref = kv_refs
    rows = kc_ref.shape[1]
    for buf, prev, cur, nxt in ((kbuf, kp_ref, kc_ref, kn_ref), (vbuf, vp_ref, vc_ref, vn_ref)):
        buf[0:A_BLOCK, :] = prev[0]
        buf[A_BLOCK:A_BLOCK + rows, :] = cur[0]
        buf[A_BLOCK + rows:, :] = nxt[0]


def _attn_a_pass(row0, first_block, n_blocks, fixed_shift, q_ref, kbuf, vbuf, bias_ref, sink_ref, ctl_ref,
                 o_ref):
    n_keys = 3 * A_BLOCK
    low = lax.broadcasted_iota(jnp.int32, (n_keys, LANES), 1) < A_HEAD_DIM
    low_q = lax.broadcasted_iota(jnp.int32, (A_BLOCK, LANES), 1) < A_HEAD_DIM
    zero = jnp.zeros((n_keys, LANES), BF16)
    nt = (((1,), (1,)), ((), ()))
    per_chunk = ATTN_CHUNK // A_BLOCK
    units = [(j, kvh) for j in range(per_chunk) for kvh in range(A_KV_HEADS)]
    logits, values, probs = {}, {}, {}

    def swap_halves(t):
        return jnp.concatenate([t[:, A_HEAD_DIM:], t[:, :A_HEAD_DIM]], axis=1)

    def padded(t, t_sw, kvh):
        lo, hi = (t, t_sw) if kvh == 0 else (t_sw, t)
        return jnp.concatenate([jnp.where(low, lo, zero), jnp.where(low, zero, hi)], axis=0)

    def logit_matmuls():
        for j in range(per_chunk):
            rows = pl.ds(row0 + j * A_BLOCK, A_BLOCK)
            kw = kbuf[pl.ds(row0 + j * A_BLOCK, n_keys), :]
            vw = vbuf[pl.ds(row0 + j * A_BLOCK, n_keys), :]
            kw_sw, vw_sw = swap_halves(kw), swap_halves(vw)
            for kvh in range(A_KV_HEADS):
                q2 = jnp.concatenate([q_ref[0, rows, (2 * kvh + c) * LANES:(2 * kvh + c + 1) * LANES]
                                      for c in range(2)], axis=0)
                logits[j, kvh] = lax.dot_general(q2, padded(kw, kw_sw, kvh), nt,
                                                 preferred_element_type=F32)
                values[j, kvh] = padded(vw, vw_sw, kvh)

    def softmaxes():
        for j, kvh in units:
            block = first_block + j
            variant = jnp.where(block == 0, FIRST, jnp.where(block == n_blocks - 1, LAST, INTERIOR))
            s2 = logits[j, kvh]
            p_rows, r_rows = [], []
            for c in range(2):
                ps, sums, sink_gaps = [], [], []
                for half in range(2):
                    head = 2 * (2 * kvh + c) + half
                    sh = s2[c * A_BLOCK:(c + 1) * A_BLOCK, half * n_keys:(half + 1) * n_keys]
                    sh = sh + bias_ref[variant, head]
                    sink = sink_ref[head] * LOG2E
                    if fixed_shift:
                        m = ctl_ref[1 + head]
                        p = jnp.exp2(sh)
                    else:
                        m = jnp.maximum(jnp.max(sh, axis=-1, keepdims=True), sink)
                        p = jnp.exp2(sh - m)
                    ps.append(p.astype(BF16))
                    sums.append(jnp.sum(p, axis=-1, keepdims=True))
                    sink_gaps.append(sink - m)
                sink_lanes = low_q[:1] if fixed_shift else low_q
                den = (jnp.where(low_q, sums[0], sums[1])
                       + jnp.exp2(jnp.where(sink_lanes, sink_gaps[0], sink_gaps[1])))
                p_rows.append(jnp.concatenate(ps, axis=1))
                r_rows.append(1.0 / den)
            probs[j, kvh] = (jnp.concatenate(p_rows, axis=0), jnp.concatenate(r_rows, axis=0))

    def value_matmuls():
        for j, kvh in units:
            p2, r2 = probs[j, kvh]
            o2 = jnp.dot(p2, values[j, kvh], preferred_element_type=F32) * r2
            for c in range(2):
                tile = 2 * kvh + c
                o_ref[0, pl.ds(row0 + j * A_BLOCK, A_BLOCK), tile * LANES:(tile + 1) * LANES] = (
                    o2[c * A_BLOCK:(c + 1) * A_BLOCK].astype(BF16))

    return logit_matmuls, softmaxes, value_matmuls


def _attn_a_kernel(seq_len, q_ref, kp_ref, kc_ref, kn_ref, vp_ref, vc_ref, vn_ref, sink_ref, ctl_ref,
                   o_ref, kbuf, vbuf, bias_ref):
    i = pl.program_id(1)
    step_rows = q_ref.shape[1]

    @pl.when((pl.program_id(0) == 0) & (i == 0))
    def _():
        _attn_a_init_bias(bias_ref, ctl_ref)

    _attn_a_load_kv((kp_ref, kc_ref, kn_ref, vp_ref, vc_ref, vn_ref), kbuf, vbuf)

    def run(fixed_shift):
        def chunk(t, carry):
            row0 = pl.multiple_of(t * ATTN_CHUNK, ATTN_CHUNK)
            for phase in _attn_a_pass(row0, (i * step_rows + row0) // A_BLOCK, seq_len // A_BLOCK,
                                      fixed_shift, q_ref, kbuf, vbuf, bias_ref, sink_ref, ctl_ref, o_ref):
                phase()
            return carry

        lax.fori_loop(0, step_rows // ATTN_CHUNK, chunk, 0)

    _by_softmax_shift(ctl_ref, run)


def _attn_a_specs(s, step_rows, position):
    per_step = step_rows // A_BLOCK
    n_blocks = s // A_BLOCK
    assert n_blocks >= 2 and s % step_rows == 0

    def cur(col):
        def index(*ids):
            b, i = position(*ids)
            return (b, i, col)
        return index

    def prev(col):
        def index(*ids):
            b, i = position(*ids)
            return (b, jnp.maximum(i * per_step - 1, 0), col)
        return index

    def nxt(col):
        def index(*ids):
            b, i = position(*ids)
            return (b, jnp.minimum((i + 1) * per_step, n_blocks - 1), col)
        return index

    q_spec = pl.BlockSpec((1, step_rows, A_Q_W), cur(AM_QA // A_Q_W))
    o_spec = pl.BlockSpec((1, step_rows, A_Q_W), cur(0))
    kv_specs = [[pl.BlockSpec((1, A_BLOCK, A_KV_W), prev(col // A_KV_W)),
                 pl.BlockSpec((1, step_rows, A_KV_W), cur(col // A_KV_W)),
                 pl.BlockSpec((1, A_BLOCK, A_KV_W), nxt(col // A_KV_W))] for col in (AM_KA, AM_VA)]
    scratch = [pltpu.VMEM((step_rows + 2 * A_BLOCK, A_KV_W), BF16)] * 2 + [
        pltpu.VMEM((3, A_HEADS, A_BLOCK, 3 * A_BLOCK), F32)]
    return q_spec, o_spec, kv_specs, scratch


def _attn_a(am, sink, logit_bound):
    n, s, _ = am.shape
    step = min(s, A_STEP)
    q_spec, o_spec, kv_specs, scratch = _attn_a_specs(s, step, lambda b, i: (b, i))
    smem = pl.BlockSpec(memory_space=pltpu.SMEM)
    ctl = _softmax_ctl(jnp.maximum(logit_bound, sink * LOG2E))
    return pl.pallas_call(
        functools.partial(_attn_a_kernel, s),
        grid=(n, s // step),
        in_specs=[q_spec] + kv_specs[0] + kv_specs[1] + [smem, smem],
        out_specs=o_spec,
        out_shape=jax.ShapeDtypeStruct((n, s, A_Q_W), BF16),
        scratch_shapes=scratch,
        compiler_params=_params(2),
        name="attn_a",
    )(*[am] * 7, sink, ctl)


B_BLOCK = 2 * B_HALF_WIN
B_KEYS = B_BLOCK + 2 * B_HALF_WIN
B_ROWS_PER_STEP = 2048


def _attn_b_kernel(sub_len, slopes, whole_planes, q_ref, *refs):
    if whole_planes:
        kc_ref, vc_ref, ctl_ref, o_ref, lse_ref, kbuf, vbuf, bias_ref = refs
        kp_ref = kn_ref = vp_ref = vn_ref = None
    else:
        kp_ref, kc_ref, kn_ref, vp_ref, vc_ref, vn_ref, ctl_ref, o_ref, lse_ref, kbuf, vbuf, bias_ref = refs
    i = pl.program_id(2)
    n_planes, step = q_ref.shape[1], q_ref.shape[2]
    n_blocks = sub_len // B_BLOCK

    @pl.when((pl.program_id(0) == 0) & (pl.program_id(1) == 0) & (i == 0))
    def _():
        shift = jnp.where(ctl_ref[0] > 0.5, ctl_ref[1], 0.0)
        q_row = lax.broadcasted_iota(jnp.int32, (B_BLOCK, B_KEYS), 0)
        k_col = lax.broadcasted_iota(jnp.int32, (B_BLOCK, B_KEYS), 1)
        dist = jnp.abs(q_row + B_HALF_WIN - k_col)
        dist_f = dist.astype(F32)
        for head, slope in enumerate(slopes):
            base = jnp.where(dist <= B_HALF_WIN, (-slope * LOG2E) * dist_f - shift, NEG_INF)
            first = jnp.where(k_col >= B_HALF_WIN, base, NEG_INF)
            bias_ref[0, head] = base
            bias_ref[1, head] = first
            bias_ref[2, head] = jnp.where(k_col < B_HALF_WIN + B_BLOCK, base, NEG_INF)
            bias_ref[3, head] = jnp.where(k_col < B_HALF_WIN + B_BLOCK, first, NEG_INF)

    no_rows = jnp.zeros((B_HALF_WIN, B_GROUP_W), BF16)
    for buf, prev, cur, nxt in ((kbuf, kp_ref, kc_ref, kn_ref), (vbuf, vp_ref, vc_ref, vn_ref)):
        for r in range(n_planes):
            buf[r, 0:B_HALF_WIN, :] = no_rows if prev is None else prev[0, r]
            buf[r, B_HALF_WIN:B_HALF_WIN + step, :] = cur[0, r]
            buf[r, B_HALF_WIN + step:, :] = no_rows if nxt is None else nxt[0, r]

    lane = lax.broadcasted_iota(jnp.int32, (B_BLOCK, LANES), 1)
    lanes_per_head = LANES // B_HEADS_PER_GROUP
    heads = [slice(h * B_HEAD_DIM, (h + 1) * B_HEAD_DIM) for h in range(B_HEADS_PER_GROUP)]
    nt = (((1,), (1,)), ((), ()))
    rows_per_pass = min(step, ATTN_CHUNK)
    planes_per_pass = ATTN_CHUNK // rows_per_pass
    passes_per_plane = step // rows_per_pass
    blocks = [(dr, j) for dr in range(planes_per_pass) for j in range(rows_per_pass // B_BLOCK)]

    def one_pass(fixed_shift, t, carry):
        if planes_per_pass == 1:
            plane0 = t // passes_per_plane
            row0 = pl.multiple_of((t % passes_per_plane) * ATTN_CHUNK, ATTN_CHUNK)
        else:
            plane0, row0 = t * planes_per_pass, 0

        logits = {}
        for dr, j in blocks:
            for h, sl in enumerate(heads):
                q = q_ref[0, plane0 + dr, pl.ds(row0 + j * B_BLOCK, B_BLOCK), sl]
                k = kbuf[plane0 + dr, pl.ds(row0 + j * B_BLOCK, B_KEYS), sl]
                logits[dr, j, h] = lax.dot_general(q, k, nt, preferred_element_type=F32)

        probs = {}
        for dr, j in blocks:
            block = (i * step + row0) // B_BLOCK + j
            variant = (block == 0).astype(jnp.int32) + 2 * (block == n_blocks - 1).astype(jnp.int32)
            m_tile = den_tile = None
            for h in range(B_HEADS_PER_GROUP):
                s = logits[dr, j, h] + bias_ref[variant, h]
                if fixed_shift:
                    m = ctl_ref[1]
                    p = jnp.exp2(s)
                else:
                    m = jnp.max(s, axis=-1, keepdims=True)
                    p = jnp.exp2(s - m)
                den = jnp.sum(p, axis=-1, keepdims=True)
                probs[dr, j, h] = (p.astype(BF16), 1.0 / den)
                den_tile = den if h == 0 else jnp.where(lane >= h * lanes_per_head, den, den_tile)
                if not fixed_shift:
                    m_tile = m if h == 0 else jnp.where(lane >= h * lanes_per_head, m, m_tile)
            lse_ref[0, plane0 + dr, pl.ds(row0 + j * B_BLOCK, B_BLOCK), :] = (
                ((m if fixed_shift else m_tile) + jnp.log2(den_tile)) * LN2)

        for dr, j in blocks:
            for h, sl in enumerate(heads):
                p, rden = probs[dr, j, h]
                v = vbuf[plane0 + dr, pl.ds(row0 + j * B_BLOCK, B_KEYS), sl]
                o = jnp.dot(p, v, preferred_element_type=F32) * rden
                o_ref[0, plane0 + dr, pl.ds(row0 + j * B_BLOCK, B_BLOCK), sl] = o.astype(BF16)
        return carry

    _by_softmax_shift(ctl_ref, lambda fixed_shift: lax.fori_loop(
        0, n_planes * step // ATTN_CHUNK, functools.partial(one_pass, fixed_shift), 0))


def _attn_b_group(qkv, gi, logit_bound):
    n, dil, sub, _ = qkv.shape
    step = min(sub, B_ROWS_PER_STEP)
    n_planes = B_ROWS_PER_STEP // step
    assert sub % step == 0 and dil % n_planes == 0 and step % B_BLOCK == 0
    assert ATTN_CHUNK % step == 0 or step % ATTN_CHUNK == 0
    halo_per_step = step // B_HALF_WIN
    last_halo = sub // B_HALF_WIN - 1
    cur = lambda col: lambda b, r, i: (b, r, i, col)
    prev = lambda col: lambda b, r, i: (b, r, jnp.maximum(i * halo_per_step - 1, 0), col)
    nxt = lambda col: lambda b, r, i: (b, r, jnp.minimum((i + 1) * halo_per_step, last_halo), col)
    whole_planes = step == sub
    kv_specs = []
    for col in (1, 2):
        block = pl.BlockSpec((1, n_planes, step, B_GROUP_W), cur(col))
        kv_specs += [block] if whole_planes else [
            pl.BlockSpec((1, n_planes, B_HALF_WIN, B_GROUP_W), prev(col)), block,
            pl.BlockSpec((1, n_planes, B_HALF_WIN, B_GROUP_W), nxt(col))]
    all_slopes = _alibi_slopes(len(B_GROUPS) * B_HEADS_PER_GROUP)
    slopes = [all_slopes[gi * B_HEADS_PER_GROUP + h] * dil for h in range(B_HEADS_PER_GROUP)]
    return pl.pallas_call(
        functools.partial(_attn_b_kernel, sub, slopes, whole_planes),
        grid=(n, dil // n_planes, sub // step),
        in_specs=[pl.BlockSpec((1, n_planes, step, B_GROUP_W), cur(0))] + kv_specs
                 + [pl.BlockSpec(memory_space=pltpu.SMEM)],
        out_specs=[pl.BlockSpec((1, n_planes, step, B_GROUP_W), cur(0)),
                   pl.BlockSpec((1, n_planes, step, LANES), cur(0))],
        out_shape=[jax.ShapeDtypeStruct((n, dil, sub, B_GROUP_W), BF16),
                   jax.ShapeDtypeStruct((n, dil, sub, LANES), F32)],
        scratch_shapes=[pltpu.VMEM((n_planes, step + 2 * B_HALF_WIN, B_GROUP_W), BF16)] * 2
                       + [pltpu.VMEM((4, B_HEADS_PER_GROUP, B_BLOCK, B_KEYS), F32)],
        compiler_params=_params(3),
        name=f"attn_b{gi}",
    )(*[qkv] * (1 + len(kv_specs)), _softmax_ctl(logit_bound))


M_STEP = 2048


def _attn_m_kernel(q_ref, k_ref, v_ref, ctl_ref, o_ref):
    heads = [slice(h * M_HEAD_DIM, (h + 1) * M_HEAD_DIM) for h in range(M_HEADS)]
    nt = (((1,), (1,)), ((), ()))

    def chunk(fixed_shift, t, carry):
        rows = pl.ds(pl.multiple_of(t * ATTN_CHUNK, ATTN_CHUNK), ATTN_CHUNK)
        logits = [lax.dot_general(q_ref[0, rows, sl], k_ref[0, :, sl], nt, preferred_element_type=F32)
                  for sl in heads]
        probs = []
        for s in logits:
            m = ctl_ref[1] if fixed_shift else jnp.max(s, axis=-1, keepdims=True)
            p = jnp.exp2(s - m)
            probs.append((p.astype(BF16), 1.0 / jnp.sum(p, axis=-1, keepdims=True)))
        for sl, (p, rden) in zip(heads, probs):
            o = jnp.dot(p, v_ref[0, :, sl], preferred_element_type=F32) * rden
            o_ref[0, rows, sl] = o.astype(BF16)
        return carry

    _by_softmax_shift(ctl_ref, lambda fixed_shift: lax.fori_loop(
        0, q_ref.shape[1] // ATTN_CHUNK, functools.partial(chunk, fixed_shift), 0))


def _attn_m(am, mk, mv, logit_bound):
    n, s, _ = am.shape
    step = min(s, M_STEP)
    cur = lambda b, i: (b, i, 0)
    q_cols = lambda b, i: (b, i, AM_QM // M_W)
    mem = lambda b, i: (b, 0, 0)
    return pl.pallas_call(
        _attn_m_kernel,
        grid=(n, s // step),
        in_specs=[pl.BlockSpec((1, step, M_W), q_cols), pl.BlockSpec((1, N_MEM, M_W), mem),
                  pl.BlockSpec((1, N_MEM, M_W), mem), pl.BlockSpec(memory_space=pltpu.SMEM)],
        out_specs=pl.BlockSpec((1, step, M_W), cur),
        out_shape=jax.ShapeDtypeStruct((n, s, M_W), BF16),
        compiler_params=_params(2),
        name="attn_m",
    )(am, mk, mv, _softmax_ctl(logit_bound))


MERGE_ROWS = 512
assert all(dil <= DEINTERLEAVE_STRIDE ** 2 for _, dil in B_GROUPS)
N_MERGE_SLABS = sum((B_GROUP_W // LANES + 1) * (1 if dil <= DEINTERLEAVE_STRIDE else 2)
                    for _, dil in B_GROUPS if dil > 1)


def _merge_kernel(x_ref, g_ref, oa_ref, o0_ref, o1_ref, o2_ref, l0_ref, l1_ref, l2_ref, om_ref,
                  wg_ref, bg_ref, wbr_ref, wout_ref, y_ref, nat):
    tm = x_ref.shape[1]
    x = x_ref[0]
    h = _rms_rows(x, g_ref[...]).astype(BF16)
    tiles = B_GROUP_W // LANES

    o_nat, lse_nat = [], []
    slab = 0
    for (_, dil), o_ref, l_ref in zip(B_GROUPS, (o0_ref, o1_ref, o2_ref), (l0_ref, l1_ref, l2_ref)):
        if dil == 1:
            o_nat.append([o_ref[0, 0, :, c * LANES:(c + 1) * LANES].astype(F32) for c in range(tiles)])
            lse_nat.append(l_ref[0, 0])
            continue
        def plane_tile(r, c):
            if c == tiles:
                return l_ref[0, r]
            return o_ref[0, r, :, c * LANES:(c + 1) * LANES].astype(F32)

        for c in range(tiles + 1):
            if dil <= DEINTERLEAVE_STRIDE:
                for r in range(dil):
                    nat[slab + c, pl.ds(r, tm // dil, stride=dil), :] = plane_tile(r, c)
                continue
            groups = DEINTERLEAVE_STRIDE
            inner, rows = dil // groups, tm // groups
            spare = slab + tiles + 1 + c
            for g in range(groups):
                for r in range(inner):
                    nat[spare, pl.ds(g * rows + r, rows // inner, stride=inner), :] = plane_tile(g + groups * r, c)
            for g in range(groups):
                nat[slab + c, pl.ds(g, rows, stride=groups), :] = nat[spare, g * rows:(g + 1) * rows, :]
        o_nat.append([nat[slab + c] for c in range(tiles)])
        lse_nat.append(nat[slab + tiles])
        slab += (tiles + 1) * (1 if dil <= DEINTERLEAVE_STRIDE else 2)

    top = jnp.maximum(jnp.maximum(lse_nat[0], lse_nat[1]), lse_nat[2])
    es = [jnp.exp(l - top) for l in lse_nat]
    r_sum = 1.0 / (es[0] + es[1] + es[2])
    ws = [e * r_sum for e in es]
    lanes_per_head = LANES // B_HEADS_PER_GROUP
    ob_heads = []
    for hd in range(B_HEADS_PER_GROUP):
        acc = None
        for w, o in zip(ws, o_nat):
            term = w[:, hd * lanes_per_head:hd * lanes_per_head + 1] * o[hd]
            acc = term if acc is None else acc + term
        ob_heads.append(acc.astype(BF16))
    ob = jnp.concatenate(ob_heads, axis=1)

    z = None
    for b, o in enumerate((oa_ref[0], ob, om_ref[0])):
        logits = jnp.dot(h, wg_ref[:, b * D_MODEL:(b + 1) * D_MODEL], preferred_element_type=F32)
        gate = jax.nn.sigmoid(logits + bg_ref[b:b + 1, :])
        term = gate * jnp.dot(o, wbr_ref[b], preferred_element_type=F32)
        z = term if z is None else z + term
    y_ref[0] = x + jnp.dot(z.astype(BF16), wout_ref[...], preferred_element_type=F32)


def _merge(x, g_mix, oa, obs, lses, om, w_gate, b_gate, w_branch, w_out):
    n, s, _ = x.shape
    tm = MERGE_ROWS
    cur = lambda b, i: (b, i, 0)
    plane = lambda b, i: (b, 0, i, 0)
    rows = lambda w: pl.BlockSpec((1, tm, w), cur)
    planes = lambda w: [pl.BlockSpec((1, dil, tm // dil, w), plane) for _, dil in B_GROUPS]
    return pl.pallas_call(
        _merge_kernel,
        grid=(n, s // tm),
        in_specs=[rows(D_MODEL), _resident((1, D_MODEL)), rows(A_Q_W)]
                 + planes(B_GROUP_W) + planes(LANES) + [rows(M_W)]
                 + [_resident(w_gate.shape), _resident(b_gate.shape), _resident(w_branch.shape),
                    _resident(w_out.shape)],
        out_specs=rows(D_MODEL),
        out_shape=jax.ShapeDtypeStruct((n, s, D_MODEL), F32),
        scratch_shapes=[pltpu.VMEM((N_MERGE_SLABS, tm, LANES), F32)],
        compiler_params=_params(2),
        name="merge",
    )(x, g_mix, oa, *obs, *lses, om, w_gate, b_gate, w_branch, w_out)


MLP_ROWS = 1024
MLP_FF_CHUNK = 1024


def _mlp_kernel(x_ref, g_ref, wup_ref, wdn_ref, y_ref):
    x = x_ref[...]
    h = _rms_rows(x, g_ref[...]).astype(BF16)
    acc = x
    for lo in range(0, D_FF, MLP_FF_CHUNK):
        u = jnp.dot(h, wup_ref[:, lo:lo + MLP_FF_CHUNK], preferred_element_type=F32)
        a = jnp.square(jnp.maximum(u, 0.0)).astype(BF16)
        acc = acc + jnp.dot(a, wdn_ref[lo:lo + MLP_FF_CHUNK, :], preferred_element_type=F32)
    y_ref[...] = acc


def _mlp(x2d, g_mlp, w_up, w_down):
    t = x2d.shape[0]
    tm = MLP_ROWS
    row = lambda i: (i, 0)
    return pl.pallas_call(
        _mlp_kernel,
        grid=(t // tm,),
        in_specs=[pl.BlockSpec((tm, D_MODEL), row), _resident((1, D_MODEL)),
                  _resident(w_up.shape), _resident(w_down.shape)],
        out_specs=pl.BlockSpec((tm, D_MODEL), row),
        out_shape=jax.ShapeDtypeStruct((t, D_MODEL), F32),
        compiler_params=_params(1),
        name="mlp",
    )(x2d, g_mlp, w_up, w_down)


def _tile2(g):
    return jnp.concatenate([g, g]).reshape(1, LANES)


def _row(g):
    return g.reshape(1, -1)


def _layer(x, mem, g_mix, g_mem, w_in, b_gate, w_mem_kv, gq_a, gk_a, sink_a, gq_b, gk_b, gq_m, gk_m,
           w_branch, w_out, g_mlp, w_up, w_down):
    n, s, _ = x.shape
    bounds = np.cumsum((0, A_Q_W, A_KV_W, A_KV_W, B_W, B_W, B_W, M_W, GATE_W))
    seg = lambda a, b: w_in[:, int(bounds[a]):int(bounds[b])].astype(BF16)
    ws = (seg(0, 1), seg(1, 3), seg(3, 4), seg(4, 5), seg(5, 6), seg(6, 7))
    w_gate = seg(7, 8)
    gq_a, gq_b, gq_m = (gq_a * (A_HEAD_DIM ** -0.5 * LOG2E), gq_b * (B_HEAD_DIM ** -0.5 * LOG2E),
                        gq_m * (M_HEAD_DIM ** -0.5 * LOG2E))
    gains = (_tile2(gq_a), _tile2(gk_a), _row(gq_b), _row(gk_b), _row(gq_m))

    am, *qkv_b = _proj(x, _row(g_mix), ws, gains)
    mk, mv = _mem_kv(mem.reshape(n * N_MEM, D_MODEL), _row(g_mem), w_mem_kv.astype(BF16), _row(gk_m))

    oa = _attn_a(am, sink_a, _logit_bound(gq_a, gk_a, A_HEAD_DIM))
    bound_b = _logit_bound(gq_b, gk_b, B_HEAD_DIM)
    b_out = [_attn_b_group(qkv_b[gi], gi, bound_b) for gi in range(len(B_GROUPS))]
    om = _attn_m(am, mk.reshape(n, N_MEM, M_W), mv.reshape(n, N_MEM, M_W),
                 _logit_bound(gq_m, gk_m, M_HEAD_DIM))

    x1 = _merge(x, _row(g_mix), oa, [o for o, _ in b_out], [l for _, l in b_out], om,
                w_gate, b_gate, w_branch.astype(BF16), w_out.astype(BF16))
    y = _mlp(x1.reshape(n * s, D_MODEL), _row(g_mlp), w_up.astype(BF16), w_down.astype(BF16))
    return y.reshape(n, s, D_MODEL)


def kernel(x_prompt, x_sample, mem_prompt, mem_sample, g_mix, g_mem, w_in, b_gate, w_mem_kv, gq_a, gk_a,
           sink_a, gq_b, gk_b, gq_m, gk_m, w_branch, w_out, g_mlp, w_up, w_down):
    depth = w_in.shape[0]

    def run(x, mem):
        for l in range(depth):
            x = _layer(x, mem, g_mix[l], g_mem[l], w_in[l], b_gate[l], w_mem_kv[l], gq_a[l], gk_a[l],
                       sink_a[l], gq_b[l], gk_b[l], gq_m[l], gk_m[l], w_branch[l], w_out[l], g_mlp[l],
                       w_up[l], w_down[l])
        return x

    return (run(x_prompt, mem_prompt), run(x_sample, mem_sample))
```

```python
import functools

import numpy as np
import jax
import jax.numpy as jnp
from jax import lax
from jax.experimental import pallas as pl
from jax.experimental.pallas import tpu as pltpu

D_MODEL = 1024
N_MEM = 256
A_HEADS = 8
A_KV_HEADS = 2
A_HEAD_DIM = 64
A_HALF_WIN = 128
B_GROUPS = ((128, 1), (512, 4), (2048, 16))
B_HEADS_PER_GROUP = 4
B_HEAD_DIM = 128
M_HEADS = 4
M_HEAD_DIM = 128
N_BRANCH = 3
BRANCH_WIDTH = D_MODEL // 2
D_FF = 4 * D_MODEL
EPS = 1e-6
NEG_INF = -1e30

A_Q_W = A_HEADS * A_HEAD_DIM
A_KV_W = A_KV_HEADS * A_HEAD_DIM
B_GROUP_W = B_HEADS_PER_GROUP * B_HEAD_DIM
B_W = len(B_GROUPS) * B_GROUP_W
M_W = M_HEADS * M_HEAD_DIM
GATE_W = N_BRANCH * D_MODEL

LANES = 128
B_HALF_WIN = 64
VMEM_LIMIT_BYTES = 56 * 1024 * 1024

BF16 = jnp.bfloat16
F32 = jnp.float32


def _alibi_slopes(n):
    return [float(2.0 ** (-8.0 * (i + 1) / n)) for i in range(n)]


def _params(n_grid_axes):
    return pltpu.CompilerParams(
        dimension_semantics=("arbitrary",) * n_grid_axes,
        vmem_limit_bytes=VMEM_LIMIT_BYTES)


def _resident(shape):
    zeros = (0,) * len(shape)
    return pl.BlockSpec(shape, lambda *_: zeros, pipeline_mode=pl.Buffered(1))


def _rms_rows(x, gain):
    ms = jnp.mean(x * x, axis=-1, keepdims=True)
    return x * lax.rsqrt(ms + EPS) * gain


def _head_norm128(blk, gain):
    ms = jnp.sum(blk * blk, axis=-1, keepdims=True) * (1.0 / LANES)
    return blk * lax.rsqrt(ms + EPS) * gain


def _head_norm64(blk, gain2):
    low = lax.broadcasted_iota(jnp.int32, blk.shape, 1) < A_HEAD_DIM
    sq = blk * blk
    ss_lo = jnp.sum(jnp.where(low, sq, 0.0), axis=-1, keepdims=True)
    ss_hi = jnp.sum(jnp.where(low, 0.0, sq), axis=-1, keepdims=True)
    ms = jnp.where(low, ss_lo, ss_hi) * (1.0 / A_HEAD_DIM)
    return blk * lax.rsqrt(ms + EPS) * gain2


PROJ_ROWS = 1024
AM_QA, AM_QM, AM_KA, AM_VA = 0, A_Q_W, A_Q_W + M_W, A_Q_W + M_W + A_KV_W
AM_W = A_Q_W + M_W + 2 * A_KV_W
B_QKV_W = 3 * B_GROUP_W
DEINTERLEAVE_STRIDE = 4
N_DILATED_SLABS = 2 * (B_GROUP_W // LANES)


def _proj_kernel(x_ref, g_ref, wqa_ref, wkva_ref, wqb_ref, wkb_ref, wvb_ref, wqm_ref,
                 gqa_ref, gka_ref, gqb_ref, gkb_ref, gqm_ref,
                 am_ref, *rest):
    b_refs, ybuf = rest[:-1], rest[-1]
    tm = x_ref.shape[1]
    h = _rms_rows(x_ref[0], g_ref[...]).astype(BF16)

    def mm(w_ref, lo, hi):
        return lambda: jnp.dot(h, w_ref[:, lo:hi], preferred_element_type=F32)

    def finish_qa(y):
        for c in range(A_Q_W // LANES):
            sl = slice(c * LANES, (c + 1) * LANES)
            am_ref[0, :, AM_QA + c * LANES:AM_QA + (c + 1) * LANES] = (
                _head_norm64(y[:, sl], gqa_ref[...]).astype(BF16))

    def finish_kva(y):
        am_ref[0, :, AM_KA:AM_KA + A_KV_W] = _head_norm64(y[:, :A_KV_W], gka_ref[...]).astype(BF16)
        am_ref[0, :, AM_VA:AM_VA + A_KV_W] = y[:, A_KV_W:].astype(BF16)

    def finish_qm(y):
        for c in range(M_W // LANES):
            sl = slice(c * LANES, (c + 1) * LANES)
            am_ref[0, :, AM_QM + c * LANES:AM_QM + (c + 1) * LANES] = (
                _head_norm128(y[:, sl], gqm_ref[...]).astype(BF16))

    def finish_b(o_ref, col0, gain_ref, dil, slab):
        def finish(y):
            for c in range(B_GROUP_W // LANES):
                sl = slice(c * LANES, (c + 1) * LANES)
                out = slice(col0 + c * LANES, col0 + (c + 1) * LANES)
                blk = y[:, sl]
                if gain_ref is not None:
                    blk = _head_norm128(blk, gain_ref[...])
                if dil == 1:
                    o_ref[0, 0, :, out] = blk.astype(BF16)
                else:
                    ybuf[slab + c] = blk
                    groups, stride = 1, dil
                    while stride > DEINTERLEAVE_STRIDE:
                        rows = tm // groups
                        parts = [ybuf[slab + c, pl.ds(g * rows + r, rows // DEINTERLEAVE_STRIDE,
                                                      stride=DEINTERLEAVE_STRIDE), :]
                                 for g in range(groups) for r in range(DEINTERLEAVE_STRIDE)]
                        ybuf[slab + c] = jnp.concatenate(parts, axis=0)
                        groups, stride = groups * DEINTERLEAVE_STRIDE, stride // DEINTERLEAVE_STRIDE
                    rows = tm // groups
                    for g in range(groups):
                        for r in range(stride):
                            o_ref[0, g + groups * r, :, out] = (
                                ybuf[slab + c, pl.ds(g * rows + r, rows // stride, stride=stride), :].astype(BF16))
        return finish

    def branch_b(gi, slab):
        dil = B_GROUPS[gi][1]
        lo = gi * B_GROUP_W
        parts = []
        for part, (w_ref, gain_ref) in enumerate(((wqb_ref, gqb_ref), (wkb_ref, gkb_ref), (wvb_ref, None))):
            parts.append((mm(w_ref, lo, lo + B_GROUP_W),
                          finish_b(b_refs[gi], part * B_GROUP_W, gain_ref, dil, slab)))
            slab = (slab + B_GROUP_W // LANES) % N_DILATED_SLABS
        return parts, slab

    by_dilation = sorted(range(len(B_GROUPS)), key=lambda gi: -B_GROUPS[gi][1])
    stages, slab = [], 0
    for gi in by_dilation[:-1]:
        parts, slab = branch_b(gi, slab)
        stages += parts
    stages += [(mm(wqa_ref, 0, A_Q_W), finish_qa), (mm(wkva_ref, 0, 2 * A_KV_W), finish_kva),
               (mm(wqm_ref, 0, M_W), finish_qm)]
    stages += branch_b(by_dilation[-1], slab)[0]

    pending = None
    for matmul, finish in stages:
        y = matmul()
        if pending is not None:
            pending[0](pending[1])
        pending = (finish, y)
    pending[0](pending[1])


def _proj(x, g_mix, ws, gains):
    n, s, _ = x.shape
    tm = PROJ_ROWS
    cur = lambda b, i: (b, i, 0)
    plane = lambda b, i: (b, 0, i, 0)
    in_specs = ([pl.BlockSpec((1, tm, D_MODEL), cur), _resident((1, D_MODEL))]
                + [_resident(w.shape) for w in ws]
                + [_resident((1, LANES)) for _ in gains])
    out_specs = [pl.BlockSpec((1, tm, AM_W), cur)]
    out_shape = [jax.ShapeDtypeStruct((n, s, AM_W), BF16)]
    for _, dil in B_GROUPS:
        out_specs.append(pl.BlockSpec((1, dil, tm // dil, B_QKV_W), plane))
        out_shape.append(jax.ShapeDtypeStruct((n, dil, s // dil, B_QKV_W), BF16))
    return pl.pallas_call(
        _proj_kernel,
        grid=(n, s // tm),
        in_specs=in_specs,
        out_specs=out_specs,
        out_shape=out_shape,
        scratch_shapes=[pltpu.VMEM((N_DILATED_SLABS, tm, LANES), F32)],
        compiler_params=_params(2),
        name="proj",
    )(x, g_mix, *ws, *gains)


MEM_ROWS = 1024


def _mem_kv_kernel(m_ref, g_ref, w_ref, gk_ref, k_ref, v_ref):
    h = _rms_rows(m_ref[...], g_ref[...]).astype(BF16)
    y = jnp.dot(h, w_ref[...], preferred_element_type=F32)
    for c in range(M_HEADS):
        sl = slice(c * LANES, (c + 1) * LANES)
        k_ref[:, sl] = _head_norm128(y[:, sl], gk_ref[...]).astype(BF16)
    v_ref[...] = y[:, M_W:].astype(BF16)


def _mem_kv(mem2d, g_mem, w_mem_kv, gk_m):
    t = mem2d.shape[0]
    tm = min(t, MEM_ROWS)
    row = lambda i: (i, 0)
    return pl.pallas_call(
        _mem_kv_kernel,
        grid=(t // tm,),
        in_specs=[pl.BlockSpec((tm, D_MODEL), row), _resident((1, D_MODEL)),
                  _resident(w_mem_kv.shape), _resident((1, LANES))],
        out_specs=[pl.BlockSpec((tm, M_W), row)] * 2,
        out_shape=[jax.ShapeDtypeStruct((t, M_W), BF16)] * 2,
        compiler_params=_params(1),
        name="mem_kv",
    )(mem2d, g_mem, w_mem_kv, gk_m)


A_BLOCK = A_HALF_WIN
A_STEP = 2048
ATTN_CHUNK = 1024
INTERIOR, FIRST, LAST = 0, 1, 2
LOG2E = float(np.log2(np.e))
LN2 = float(np.log(2.0))

SOFTMAX_SHIFT_LIMIT = 40.0
BF16_SLACK = 1.0 + 2.0 ** -6


def _logit_bound(gain_q, gain_k, head_dim):
    return head_dim * jnp.max(jnp.abs(gain_q)) * jnp.max(jnp.abs(gain_k)) * BF16_SLACK


def _softmax_ctl(shifts):
    shifts = jnp.asarray(shifts, F32).reshape(-1)
    fixed = jnp.all(shifts <= SOFTMAX_SHIFT_LIMIT)
    return jnp.concatenate([fixed.astype(F32).reshape(1), shifts])


def _by_softmax_shift(ctl_ref, run):
    fixed = ctl_ref[0] > 0.5

    @pl.when(fixed)
    def _():
        run(True)

    @pl.when(jnp.logical_not(fixed))
    def _():
        run(False)


def _attn_a_init_bias(bias_ref, ctl_ref):
    n_keys = 3 * A_BLOCK
    q_row = lax.broadcasted_iota(jnp.int32, (A_BLOCK, n_keys), 0)
    k_col = lax.broadcasted_iota(jnp.int32, (A_BLOCK, n_keys), 1)
    dist = jnp.abs(q_row + A_BLOCK - k_col)
    dist_f = dist.astype(F32)
    for head, slope in enumerate(_alibi_slopes(A_HEADS)):
        shift = jnp.where(ctl_ref[0] > 0.5, ctl_ref[1 + head], 0.0)
        base = jnp.where(dist <= A_HALF_WIN, (-slope * LOG2E) * dist_f - shift, NEG_INF)
        bias_ref[INTERIOR, head] = base
        bias_ref[FIRST, head] = jnp.where(k_col >= A_BLOCK, base, NEG_INF)
        bias_ref[LAST, head] = jnp.where(k_col < 2 * A_BLOCK, base, NEG_INF)


def _attn_a_load_kv(kv_refs, kbuf, vbuf):
    kp_ref, kc_ref, kn_ref, vp_ref, vc_ref, vn_ref = kv_refs
    rows = kc_ref.shape[1]
    for buf, prev, cur, nxt in ((kbuf, kp_ref, kc_ref, kn_ref), (vbuf, vp_ref, vc_ref, vn_ref)):
        buf[0:A_BLOCK, :] = prev[0]
        buf[A_BLOCK:A_BLOCK + rows, :] = cur[0]
        buf[A_BLOCK + rows:, :] = nxt[0]


def _attn_a_pass(row0, first_block, n_blocks, fixed_shift, q_ref, kbuf, vbuf, bias_ref, sink_ref, ctl_ref,
                 o_ref):
    n_keys = 3 * A_BLOCK
    low = lax.broadcasted_iota(jnp.int32, (n_keys, LANES), 1) < A_HEAD_DIM
    low_q = lax.broadcasted_iota(jnp.int32, (A_BLOCK, LANES), 1) < A_HEAD_DIM
    zero = jnp.zeros((n_keys, LANES), BF16)
    nt = (((1,), (1,)), ((), ()))
    per_chunk = ATTN_CHUNK // A_BLOCK
    units = [(j, kvh) for j in range(per_chunk) for kvh in range(A_KV_HEADS)]
    logits, values, probs = {}, {}, {}

    def swap_halves(t):
        return jnp.concatenate([t[:, A_HEAD_DIM:], t[:, :A_HEAD_DIM]], axis=1)

    def padded(t, t_sw, kvh):
        lo, hi = (t, t_sw) if kvh == 0 else (t_sw, t)
        return jnp.concatenate([jnp.where(low, lo, zero), jnp.where(low, zero, hi)], axis=0)

    def logit_matmuls():
        for j in range(per_chunk):
            rows = pl.ds(row0 + j * A_BLOCK, A_BLOCK)
            kw = kbuf[pl.ds(row0 + j * A_BLOCK, n_keys), :]
            vw = vbuf[pl.ds(row0 + j * A_BLOCK, n_keys), :]
            kw_sw, vw_sw = swap_halves(kw), swap_halves(vw)
            for kvh in range(A_KV_HEADS):
                q2 = jnp.concatenate([q_ref[0, rows, (2 * kvh + c) * LANES:(2 * kvh + c + 1) * LANES]
                                      for c in range(2)], axis=0)
                logits[j, kvh] = lax.dot_general(q2, padded(kw, kw_sw, kvh), nt,
                                                 preferred_element_type=F32)
                values[j, kvh] = padded(vw, vw_sw, kvh)

    def softmaxes():
        for j, kvh in units:
            block = first_block + j
            variant = jnp.where(block == 0, FIRST, jnp.where(block == n_blocks - 1, LAST, INTERIOR))
            s2 = logits[j, kvh]
            p_rows, r_rows = [], []
            for c in range(2):
                ps, sums, sink_gaps = [], [], []
                for half in range(2):
                    head = 2 * (2 * kvh + c) + half
                    sh = s2[c * A_BLOCK:(c + 1) * A_BLOCK, half * n_keys:(half + 1) * n_keys]
                    sh = sh + bias_ref[variant, head]
                    sink = sink_ref[head] * LOG2E
                    if fixed_shift:
                        m = ctl_ref[1 + head]
                        p = jnp.exp2(sh)
                    else:
                        m = jnp.maximum(jnp.max(sh, axis=-1, keepdims=True), sink)
                        p = jnp.exp2(sh - m)
                    ps.append(p.astype(BF16))
                    sums.append(jnp.sum(p, axis=-1, keepdims=True))
                    sink_gaps.append(sink - m)
                sink_lanes = low_q[:1] if fixed_shift else low_q
                den = (jnp.where(low_q, sums[0], sums[1])
                       + jnp.exp2(jnp.where(sink_lanes, sink_gaps[0], sink_gaps[1])))
                p_rows.append(jnp.concatenate(ps, axis=1))
                r_rows.append(1.0 / den)
            probs[j, kvh] = (jnp.concatenate(p_rows, axis=0), jnp.concatenate(r_rows, axis=0))

    def value_matmuls():
        for j, kvh in units:
            p2, r2 = probs[j, kvh]
            o2 = jnp.dot(p2, values[j, kvh], preferred_element_type=F32) * r2
            for c in range(2):
                tile = 2 * kvh + c
                o_ref[0, pl.ds(row0 + j * A_BLOCK, A_BLOCK), tile * LANES:(tile + 1) * LANES] = (
                    o2[c * A_BLOCK:(c + 1) * A_BLOCK].astype(BF16))

    return logit_matmuls, softmaxes, value_matmuls


def _attn_a_kernel(seq_len, q_ref, kp_ref, kc_ref, kn_ref, vp_ref, vc_ref, vn_ref, sink_ref, ctl_ref,
                   o_ref, kbuf, vbuf, bias_ref):
    i = pl.program_id(1)
    step_rows = q_ref.shape[1]

    @pl.when((pl.program_id(0) == 0) & (i == 0))
    def _():
        _attn_a_init_bias(bias_ref, ctl_ref)

    _attn_a_load_kv((kp_ref, kc_ref, kn_ref, vp_ref, vc_ref, vn_ref), kbuf, vbuf)

    def run(fixed_shift):
        def chunk(t, carry):
            row0 = pl.multiple_of(t * ATTN_CHUNK, ATTN_CHUNK)
            for phase in _attn_a_pass(row0, (i * step_rows + row0) // A_BLOCK, seq_len // A_BLOCK,
                                      fixed_shift, q_ref, kbuf, vbuf, bias_ref, sink_ref, ctl_ref, o_ref):
                phase()
            return carry

        lax.fori_loop(0, step_rows // ATTN_CHUNK, chunk, 0)

    _by_softmax_shift(ctl_ref, run)


def _attn_a_specs(s, step_rows, position):
    per_step = step_rows // A_BLOCK
    n_blocks = s // A_BLOCK
    assert n_blocks >= 2 and s % step_rows == 0

    def cur(col):
        def index(*ids):
            b, i = position(*ids)
            return (b, i, col)
        return index

    def prev(col):
        def index(*ids):
            b, i = position(*ids)
            return (b, jnp.maximum(i * per_step - 1, 0), col)
        return index

    def nxt(col):
        def index(*ids):
            b, i = position(*ids)
            return (b, jnp.minimum((i + 1) * per_step, n_blocks - 1), col)
        return index

    q_spec = pl.BlockSpec((1, step_rows, A_Q_W), cur(AM_QA // A_Q_W))
    o_spec = pl.BlockSpec((1, step_rows, A_Q_W), cur(0))
    kv_specs = [[pl.BlockSpec((1, A_BLOCK, A_KV_W), prev(col // A_KV_W)),
                 pl.BlockSpec((1, step_rows, A_KV_W), cur(col // A_KV_W)),
                 pl.BlockSpec((1, A_BLOCK, A_KV_W), nxt(col // A_KV_W))] for col in (AM_KA, AM_VA)]
    scratch = [pltpu.VMEM((step_rows + 2 * A_BLOCK, A_KV_W), BF16)] * 2 + [
        pltpu.VMEM((3, A_HEADS, A_BLOCK, 3 * A_BLOCK), F32)]
    return q_spec, o_spec, kv_specs, scratch


def _attn_a(am, sink, logit_bound):
    n, s, _ = am.shape
    step = min(s, A_STEP)
    q_spec, o_spec, kv_specs, scratch = _attn_a_specs(s, step, lambda b, i: (b, i))
    smem = pl.BlockSpec(memory_space=pltpu.SMEM)
    ctl = _softmax_ctl(jnp.maximum(logit_bound, sink * LOG2E))
    return pl.pallas_call(
        functools.partial(_attn_a_kernel, s),
        grid=(n, s // step),
        in_specs=[q_spec] + kv_specs[0] + kv_specs[1] + [smem, smem],
        out_specs=o_spec,
        out_shape=jax.ShapeDtypeStruct((n, s, A_Q_W), BF16),
        scratch_shapes=scratch,
        compiler_params=_params(2),
        name="attn_a",
    )(*[am] * 7, sink, ctl)


B_BLOCK = 2 * B_HALF_WIN
B_ROWS_PER_STEP = 2048


def _attn_b_kernel(sub_len, slopes, whole_planes, q_ref, *refs):
    if whole_planes:
        kc_ref, vc_ref, ctl_ref, o_ref, lse_ref, kbuf, vbuf, bias_ref = refs
        kp_ref = kn_ref = vp_ref = vn_ref = None
    else:
        kp_ref, kc_ref, kn_ref, vp_ref, vc_ref, vn_ref, ctl_ref, o_ref, lse_ref, kbuf, vbuf, bias_ref = refs
    i = pl.program_id(2)
    n_planes, step = q_ref.shape[1], q_ref.shape[2]
    n_blocks = sub_len // B_BLOCK
    halo = (kbuf.shape[1] - step) // 2
    n_keys = B_BLOCK + 2 * halo

    @pl.when((pl.program_id(0) == 0) & (pl.program_id(1) == 0) & (i == 0))
    def _():
        shift = jnp.where(ctl_ref[0] > 0.5, ctl_ref[1], 0.0)
        q_row = lax.broadcasted_iota(jnp.int32, (B_BLOCK, n_keys), 0)
        k_col = lax.broadcasted_iota(jnp.int32, (B_BLOCK, n_keys), 1)
        dist = jnp.abs(q_row + halo - k_col)
        dist_f = dist.astype(F32)
        for head, slope in enumerate(slopes):
            base = jnp.where(dist <= B_HALF_WIN, (-slope * LOG2E) * dist_f - shift, NEG_INF)
            first = jnp.where(k_col >= halo, base, NEG_INF)
            bias_ref[0, head] = base
            bias_ref[1, head] = first
            bias_ref[2, head] = jnp.where(k_col < halo + B_BLOCK, base, NEG_INF)
            bias_ref[3, head] = jnp.where(k_col < halo + B_BLOCK, first, NEG_INF)

    for buf, prev, cur, nxt in ((kbuf, kp_ref, kc_ref, kn_ref), (vbuf, vp_ref, vc_ref, vn_ref)):
        for r in range(n_planes):
            buf[r, halo:halo + step, :] = cur[0, r]
            if halo:
                no_rows = jnp.zeros((halo, B_GROUP_W), BF16)
                buf[r, 0:halo, :] = no_rows if prev is None else prev[0, r]
                buf[r, halo + step:, :] = no_rows if nxt is None else nxt[0, r]

    lane = lax.broadcasted_iota(jnp.int32, (B_BLOCK, LANES), 1)
    lanes_per_head = LANES // B_HEADS_PER_GROUP
    heads = [slice(h * B_HEAD_DIM, (h + 1) * B_HEAD_DIM) for h in range(B_HEADS_PER_GROUP)]
    nt = (((1,), (1,)), ((), ()))
    rows_per_pass = min(step, ATTN_CHUNK)
    planes_per_pass = ATTN_CHUNK // rows_per_pass
    passes_per_plane = step // rows_per_pass
    blocks = [(dr, j) for dr in range(planes_per_pass) for j in range(rows_per_pass // B_BLOCK)]

    def one_pass(fixed_shift, t, carry):
        if planes_per_pass == 1:
            plane0 = t // passes_per_plane
            row0 = pl.multiple_of((t % passes_per_plane) * ATTN_CHUNK, ATTN_CHUNK)
        else:
            plane0, row0 = t * planes_per_pass, 0

        logits = {}
        for dr, j in blocks:
            for h, sl in enumerate(heads):
                q = q_ref[0, plane0 + dr, pl.ds(row0 + j * B_BLOCK, B_BLOCK), sl]
                k = kbuf[plane0 + dr, pl.ds(row0 + j * B_BLOCK, n_keys), sl]
                logits[dr, j, h] = lax.dot_general(q, k, nt, preferred_element_type=F32)

        probs = {}
        for dr, j in blocks:
            block = (i * step + row0) // B_BLOCK + j
            variant = (block == 0).astype(jnp.int32) + 2 * (block == n_blocks - 1).astype(jnp.int32)
            m_tile = den_tile = None
            for h in range(B_HEADS_PER_GROUP):
                s = logits[dr, j, h] + bias_ref[variant, h]
                if fixed_shift:
                    m = ctl_ref[1]
                    p = jnp.exp2(s)
                else:
                    m = jnp.max(s, axis=-1, keepdims=True)
                    p = jnp.exp2(s - m)
                den = jnp.sum(p, axis=-1, keepdims=True)
                probs[dr, j, h] = (p.astype(BF16), 1.0 / den)
                den_tile = den if h == 0 else jnp.where(lane >= h * lanes_per_head, den, den_tile)
                if not fixed_shift:
                    m_tile = m if h == 0 else jnp.where(lane >= h * lanes_per_head, m, m_tile)
            lse_ref[0, plane0 + dr, pl.ds(row0 + j * B_BLOCK, B_BLOCK), :] = (
                ((m if fixed_shift else m_tile) + jnp.log2(den_tile)) * LN2)

        for dr, j in blocks:
            for h, sl in enumerate(heads):
                p, rden = probs[dr, j, h]
                v = vbuf[plane0 + dr, pl.ds(row0 + j * B_BLOCK, n_keys), sl]
                o = jnp.dot(p, v, preferred_element_type=F32) * rden
                o_ref[0, plane0 + dr, pl.ds(row0 + j * B_BLOCK, B_BLOCK), sl] = o.astype(BF16)
        return carry

    _by_softmax_shift(ctl_ref, lambda fixed_shift: lax.fori_loop(
        0, n_planes * step // ATTN_CHUNK, functools.partial(one_pass, fixed_shift), 0))


def _attn_b_group(qkv, gi, logit_bound):
    n, dil, sub, _ = qkv.shape
    step = min(sub, B_ROWS_PER_STEP)
    n_planes = B_ROWS_PER_STEP // step
    assert sub % step == 0 and dil % n_planes == 0 and step % B_BLOCK == 0
    assert ATTN_CHUNK % step == 0 or step % ATTN_CHUNK == 0
    halo_per_step = step // B_HALF_WIN
    last_halo = sub // B_HALF_WIN - 1
    cur = lambda col: lambda b, r, i: (b, r, i, col)
    prev = lambda col: lambda b, r, i: (b, r, jnp.maximum(i * halo_per_step - 1, 0), col)
    nxt = lambda col: lambda b, r, i: (b, r, jnp.minimum((i + 1) * halo_per_step, last_halo), col)
    whole_planes = step == sub
    halo = 0 if sub == B_BLOCK else B_HALF_WIN
    kv_specs = []
    for col in (1, 2):
        block = pl.BlockSpec((1, n_planes, step, B_GROUP_W), cur(col))
        kv_specs += [block] if whole_planes else [
            pl.BlockSpec((1, n_planes, B_HALF_WIN, B_GROUP_W), prev(col)), block,
            pl.BlockSpec((1, n_planes, B_HALF_WIN, B_GROUP_W), nxt(col))]
    all_slopes = _alibi_slopes(len(B_GROUPS) * B_HEADS_PER_GROUP)
    slopes = [all_slopes[gi * B_HEADS_PER_GROUP + h] * dil for h in range(B_HEADS_PER_GROUP)]
    return pl.pallas_call(
        functools.partial(_attn_b_kernel, sub, slopes, whole_planes),
        grid=(n, dil // n_planes, sub // step),
        in_specs=[pl.BlockSpec((1, n_planes, step, B_GROUP_W), cur(0))] + kv_specs
                 + [pl.BlockSpec(memory_space=pltpu.SMEM)],
        out_specs=[pl.BlockSpec((1, n_planes, step, B_GROUP_W), cur(0)),
                   pl.BlockSpec((1, n_planes, step, LANES), cur(0))],
        out_shape=[jax.ShapeDtypeStruct((n, dil, sub, B_GROUP_W), BF16),
                   jax.ShapeDtypeStruct((n, dil, sub, LANES), F32)],
        scratch_shapes=[pltpu.VMEM((n_planes, step + 2 * halo, B_GROUP_W), BF16)] * 2
                       + [pltpu.VMEM((4, B_HEADS_PER_GROUP, B_BLOCK, B_BLOCK + 2 * halo), F32)],
        compiler_params=_params(3),
        name=f"attn_b{gi}",
    )(*[qkv] * (1 + len(kv_specs)), _softmax_ctl(logit_bound))


M_STEP = 2048


def _attn_m_kernel(q_ref, k_ref, v_ref, ctl_ref, o_ref):
    heads = [slice(h * M_HEAD_DIM, (h + 1) * M_HEAD_DIM) for h in range(M_HEADS)]
    nt = (((1,), (1,)), ((), ()))

    def chunk(fixed_shift, t, carry):
        rows = pl.ds(pl.multiple_of(t * ATTN_CHUNK, ATTN_CHUNK), ATTN_CHUNK)
        logits = [lax.dot_general(q_ref[0, rows, sl], k_ref[0, :, sl], nt, preferred_element_type=F32)
                  for sl in heads]
        probs = []
        for s in logits:
            m = ctl_ref[1] if fixed_shift else jnp.max(s, axis=-1, keepdims=True)
            p = jnp.exp2(s - m)
            probs.append((p.astype(BF16), 1.0 / jnp.sum(p, axis=-1, keepdims=True)))
        for sl, (p, rden) in zip(heads, probs):
            o = jnp.dot(p, v_ref[0, :, sl], preferred_element_type=F32) * rden
            o_ref[0, rows, sl] = o.astype(BF16)
        return carry

    _by_softmax_shift(ctl_ref, lambda fixed_shift: lax.fori_loop(
        0, q_ref.shape[1] // ATTN_CHUNK, functools.partial(chunk, fixed_shift), 0))


def _attn_m(am, mk, mv, logit_bound):
    n, s, _ = am.shape
    step = min(s, M_STEP)
    cur = lambda b, i: (b, i, 0)
    q_cols = lambda b, i: (b, i, AM_QM // M_W)
    mem = lambda b, i: (b, 0, 0)
    return pl.pallas_call(
        _attn_m_kernel,
        grid=(n, s // step),
        in_specs=[pl.BlockSpec((1, step, M_W), q_cols), pl.BlockSpec((1, N_MEM, M_W), mem),
                  pl.BlockSpec((1, N_MEM, M_W), mem), pl.BlockSpec(memory_space=pltpu.SMEM)],
        out_specs=pl.BlockSpec((1, step, M_W), cur),
        out_shape=jax.ShapeDtypeStruct((n, s, M_W), BF16),
        compiler_params=_params(2),
        name="attn_m",
    )(am, mk, mv, _softmax_ctl(logit_bound))


MERGE_ROWS = 512
assert all(dil <= DEINTERLEAVE_STRIDE ** 2 for _, dil in B_GROUPS)
N_MERGE_SLABS = sum((B_GROUP_W // LANES + 1) * (1 if dil <= DEINTERLEAVE_STRIDE else 2)
                    for _, dil in B_GROUPS if dil > 1)


def _merge_kernel(x_ref, g_ref, oa_ref, o0_ref, o1_ref, o2_ref, l0_ref, l1_ref, l2_ref, om_ref,
                  wg_ref, bg_ref, wbr_ref, wout_ref, y_ref, nat):
    tm = x_ref.shape[1]
    x = x_ref[0]
    h = _rms_rows(x, g_ref[...]).astype(BF16)
    tiles = B_GROUP_W // LANES

    o_nat, lse_nat = [], []
    slab = 0
    for (_, dil), o_ref, l_ref in zip(B_GROUPS, (o0_ref, o1_ref, o2_ref), (l0_ref, l1_ref, l2_ref)):
        if dil == 1:
            o_nat.append([o_ref[0, 0, :, c * LANES:(c + 1) * LANES].astype(F32) for c in range(tiles)])
            lse_nat.append(l_ref[0, 0])
            continue
        def plane_tile(r, c):
            if c == tiles:
                return l_ref[0, r]
            return o_ref[0, r, :, c * LANES:(c + 1) * LANES].astype(F32)

        for c in range(tiles + 1):
            if dil <= DEINTERLEAVE_STRIDE:
                for r in range(dil):
                    nat[slab + c, pl.ds(r, tm // dil, stride=dil), :] = plane_tile(r, c)
                continue
            groups = DEINTERLEAVE_STRIDE
            inner, rows = dil // groups, tm // groups
            spare = slab + tiles + 1 + c
            for g in range(groups):
                for r in range(inner):
                    nat[spare, pl.ds(g * rows + r, rows // inner, stride=inner), :] = plane_tile(g + groups * r, c)
            for g in range(groups):
                nat[slab + c, pl.ds(g, rows, stride=groups), :] = nat[spare, g * rows:(g + 1) * rows, :]
        o_nat.append([nat[slab + c] for c in range(tiles)])
        lse_nat.append(nat[slab + tiles])
        slab += (tiles + 1) * (1 if dil <= DEINTERLEAVE_STRIDE else 2)

    top = jnp.maximum(jnp.maximum(lse_nat[0], lse_nat[1]), lse_nat[2])
    es = [jnp.exp(l - top) for l in lse_nat]
    r_sum = 1.0 / (es[0] + es[1] + es[2])
    ws = [e * r_sum for e in es]
    lanes_per_head = LANES // B_HEADS_PER_GROUP
    ob_heads = []
    for hd in range(B_HEADS_PER_GROUP):
        acc = None
        for w, o in zip(ws, o_nat):
            term = w[:, hd * lanes_per_head:hd * lanes_per_head + 1] * o[hd]
            acc = term if acc is None else acc + term
        ob_heads.append(acc.astype(BF16))
    ob = jnp.concatenate(ob_heads, axis=1)

    z = None
    for b, o in enumerate((oa_ref[0], ob, om_ref[0])):
        logits = jnp.dot(h, wg_ref[:, b * D_MODEL:(b + 1) * D_MODEL], preferred_element_type=F32)
        gate = jax.nn.sigmoid(logits + bg_ref[b:b + 1, :])
        term = gate * jnp.dot(o, wbr_ref[b], preferred_element_type=F32)
        z = term if z is None else z + term
    y_ref[0] = x + jnp.dot(z.astype(BF16), wout_ref[...], preferred_element_type=F32)


def _merge(x, g_mix, oa, obs, lses, om, w_gate, b_gate, w_branch, w_out):
    n, s, _ = x.shape
    tm = MERGE_ROWS
    cur = lambda b, i: (b, i, 0)
    plane = lambda b, i: (b, 0, i, 0)
    rows = lambda w: pl.BlockSpec((1, tm, w), cur)
    planes = lambda w: [pl.BlockSpec((1, dil, tm // dil, w), plane) for _, dil in B_GROUPS]
    return pl.pallas_call(
        _merge_kernel,
        grid=(n, s // tm),
        in_specs=[rows(D_MODEL), _resident((1, D_MODEL)), rows(A_Q_W)]
                 + planes(B_GROUP_W) + planes(LANES) + [rows(M_W)]
                 + [_resident(w_gate.shape), _resident(b_gate.shape), _resident(w_branch.shape),
                    _resident(w_out.shape)],
        out_specs=rows(D_MODEL),
        out_shape=jax.ShapeDtypeStruct((n, s, D_MODEL), F32),
        scratch_shapes=[pltpu.VMEM((N_MERGE_SLABS, tm, LANES), F32)],
        compiler_params=_params(2),
        name="merge",
    )(x, g_mix, oa, *obs, *lses, om, w_gate, b_gate, w_branch, w_out)


MLP_ROWS = 1024
MLP_FF_CHUNK = 1024


def _mlp_kernel(x_ref, g_ref, wup_ref, wdn_ref, y_ref):
    x = x_ref[...]
    h = _rms_rows(x, g_ref[...]).astype(BF16)
    acc = x
    for lo in range(0, D_FF, MLP_FF_CHUNK):
        u = jnp.dot(h, wup_ref[:, lo:lo + MLP_FF_CHUNK], preferred_element_type=F32)
        a = jnp.square(jnp.maximum(u, 0.0)).astype(BF16)
        acc = acc + jnp.dot(a, wdn_ref[lo:lo + MLP_FF_CHUNK, :], preferred_element_type=F32)
    y_ref[...] = acc


def _mlp(x2d, g_mlp, w_up, w_down):
    t = x2d.shape[0]
    tm = MLP_ROWS
    row = lambda i: (i, 0)
    return pl.pallas_call(
        _mlp_kernel,
        grid=(t // tm,),
        in_specs=[pl.BlockSpec((tm, D_MODEL), row), _resident((1, D_MODEL)),
                  _resident(w_up.shape), _resident(w_down.shape)],
        out_specs=pl.BlockSpec((tm, D_MODEL), row),
        out_shape=jax.ShapeDtypeStruct((t, D_MODEL), F32),
        compiler_params=_params(1),
        name="mlp",
    )(x2d, g_mlp, w_up, w_down)


def _tile2(g):
    return jnp.concatenate([g, g]).reshape(1, LANES)


def _row(g):
    return g.reshape(1, -1)


def _layer(x, mem, g_mix, g_mem, w_in, b_gate, w_mem_kv, gq_a, gk_a, sink_a, gq_b, gk_b, gq_m, gk_m,
           w_branch, w_out, g_mlp, w_up, w_down):
    n, s, _ = x.shape
    bounds = np.cumsum((0, A_Q_W, A_KV_W, A_KV_W, B_W, B_W, B_W, M_W, GATE_W))
    seg = lambda a, b: w_in[:, int(bounds[a]):int(bounds[b])].astype(BF16)
    ws = (seg(0, 1), seg(1, 3), seg(3, 4), seg(4, 5), seg(5, 6), seg(6, 7))
    w_gate = seg(7, 8)
    gq_a, gq_b, gq_m = (gq_a * (A_HEAD_DIM ** -0.5 * LOG2E), gq_b * (B_HEAD_DIM ** -0.5 * LOG2E),
                        gq_m * (M_HEAD_DIM ** -0.5 * LOG2E))
    gains = (_tile2(gq_a), _tile2(gk_a), _row(gq_b), _row(gk_b), _row(gq_m))

    am, *qkv_b = _proj(x, _row(g_mix), ws, gains)
    mk, mv = _mem_kv(mem.reshape(n * N_MEM, D_MODEL), _row(g_mem), w_mem_kv.astype(BF16), _row(gk_m))

    oa = _attn_a(am, sink_a, _logit_bound(gq_a, gk_a, A_HEAD_DIM))
    bound_b = _logit_bound(gq_b, gk_b, B_HEAD_DIM)
    b_out = [_attn_b_group(qkv_b[gi], gi, bound_b) for gi in range(len(B_GROUPS))]
    om = _attn_m(am, mk.reshape(n, N_MEM, M_W), mv.reshape(n, N_MEM, M_W),
                 _logit_bound(gq_m, gk_m, M_HEAD_DIM))

    x1 = _merge(x, _row(g_mix), oa, [o for o, _ in b_out], [l for _, l in b_out], om,
                w_gate, b_gate, w_branch.astype(BF16), w_out.astype(BF16))
    y = _mlp(x1.reshape(n * s, D_MODEL), _row(g_mlp), w_up.astype(BF16), w_down.astype(BF16))
    return y.reshape(n, s, D_MODEL)


def kernel(x_prompt, x_sample, mem_prompt, mem_sample, g_mix, g_mem, w_in, b_gate, w_mem_kv, gq_a, gk_a,
           sink_a, gq_b, gk_b, gq_m, gk_m, w_branch, w_out, g_mlp, w_up, w_down):
    depth = w_in.shape[0]

    def run(x, mem):
        for l in range(depth):
            x = _layer(x, mem, g_mix[l], g_mem[l], w_in[l], b_gate[l], w_mem_kv[l], gq_a[l], gk_a[l],
                       sink_a[l], gq_b[l], gk_b[l], gq_m[l], gk_m[l], w_branch[l], w_out[l], g_mlp[l],
                       w_up[l], w_down[l])
        return x

    return (run(x_prompt, mem_prompt), run(x_sample, mem_sample))
```

```python
import functools

import numpy as np
import jax
import jax.numpy as jnp
from jax import lax
from jax.experimental import pallas as pl
from jax.experimental.pallas import tpu as pltpu

D_MODEL = 1024
N_MEM = 256
A_HEADS = 8
A_KV_HEADS = 2
A_HEAD_DIM = 64
A_HALF_WIN = 128
B_GROUPS = ((128, 1), (512, 4), (2048, 16))
B_HEADS_PER_GROUP = 4
B_HEAD_DIM = 128
M_HEADS = 4
M_HEAD_DIM = 128
N_BRANCH = 3
BRANCH_WIDTH = D_MODEL // 2
D_FF = 4 * D_MODEL
EPS = 1e-6
NEG_INF = -1e30

A_Q_W = A_HEADS * A_HEAD_DIM
A_KV_W = A_KV_HEADS * A_HEAD_DIM
B_GROUP_W = B_HEADS_PER_GROUP * B_HEAD_DIM
B_W = len(B_GROUPS) * B_GROUP_W
M_W = M_HEADS * M_HEAD_DIM
GATE_W = N_BRANCH * D_MODEL

LANES = 128
B_HALF_WIN = 64
VMEM_LIMIT_BYTES = 56 * 1024 * 1024

BF16 = jnp.bfloat16
F32 = jnp.float32


def _alibi_slopes(n):
    return [float(2.0 ** (-8.0 * (i + 1) / n)) for i in range(n)]


def _params(n_grid_axes):
    return pltpu.CompilerParams(
        dimension_semantics=("arbitrary",) * n_grid_axes,
        vmem_limit_bytes=VMEM_LIMIT_BYTES)


def _resident(shape):
    zeros = (0,) * len(shape)
    return pl.BlockSpec(shape, lambda *_: zeros, pipeline_mode=pl.Buffered(1))


def _rms_rows(x, gain):
    ms = jnp.mean(x * x, axis=-1, keepdims=True)
    return x * lax.rsqrt(ms + EPS) * gain


def _head_norm128(blk, gain):
    ms = jnp.sum(blk * blk, axis=-1, keepdims=True) * (1.0 / LANES)
    return blk * lax.rsqrt(ms + EPS) * gain


def _head_norm64(blk, gain2):
    low = lax.broadcasted_iota(jnp.int32, blk.shape, 1) < A_HEAD_DIM
    sq = blk * blk
    ss_lo = jnp.sum(jnp.where(low, sq, 0.0), axis=-1, keepdims=True)
    ss_hi = jnp.sum(jnp.where(low, 0.0, sq), axis=-1, keepdims=True)
    ms = jnp.where(low, ss_lo, ss_hi) * (1.0 / A_HEAD_DIM)
    return blk * lax.rsqrt(ms + EPS) * gain2


PROJ_ROWS = 1024
AM_QA, AM_QM, AM_KA, AM_VA = 0, A_Q_W, A_Q_W + M_W, A_Q_W + M_W + A_KV_W
AM_W = A_Q_W + M_W + 2 * A_KV_W
B_QKV_W = 3 * B_GROUP_W
DEINTERLEAVE_STRIDE = 4
N_DILATED_SLABS = 2 * (B_GROUP_W // LANES)


def _proj_kernel(x_ref, g_ref, wqa_ref, wkva_ref, wqb_ref, wkb_ref, wvb_ref, wqm_ref,
                 gqa_ref, gka_ref, gqb_ref, gkb_ref, gqm_ref,
                 am_ref, *rest):
    b_refs, ybuf = rest[:-1], rest[-1]
    tm = x_ref.shape[1]
    h = _rms_rows(x_ref[0], g_ref[...]).astype(BF16)

    def mm(w_ref, lo, hi):
        return lambda: jnp.dot(h, w_ref[:, lo:hi], preferred_element_type=F32)

    def finish_qa(y):
        for c in range(A_Q_W // LANES):
            sl = slice(c * LANES, (c + 1) * LANES)
            am_ref[0, :, AM_QA + c * LANES:AM_QA + (c + 1) * LANES] = (
                _head_norm64(y[:, sl], gqa_ref[...]).astype(BF16))

    def finish_kva(y):
        am_ref[0, :, AM_KA:AM_KA + A_KV_W] = _head_norm64(y[:, :A_KV_W], gka_ref[...]).astype(BF16)
        am_ref[0, :, AM_VA:AM_VA + A_KV_W] = y[:, A_KV_W:].astype(BF16)

    def finish_qm(y):
        for c in range(M_W // LANES):
            sl = slice(c * LANES, (c + 1) * LANES)
            am_ref[0, :, AM_QM + c * LANES:AM_QM + (c + 1) * LANES] = (
                _head_norm128(y[:, sl], gqm_ref[...]).astype(BF16))

    def finish_b(o_ref, col0, gain_ref, dil, slab):
        def finish(y):
            for c in range(B_GROUP_W // LANES):
                sl = slice(c * LANES, (c + 1) * LANES)
                out = slice(col0 + c * LANES, col0 + (c + 1) * LANES)
                blk = y[:, sl]
                if gain_ref is not None:
                    blk = _head_norm128(blk, gain_ref[...])
                if dil == 1:
                    o_ref[0, 0, :, out] = blk.astype(BF16)
                else:
                    ybuf[slab + c] = blk
                    groups, stride = 1, dil
                    while stride > DEINTERLEAVE_STRIDE:
                        rows = tm // groups
                        parts = [ybuf[slab + c, pl.ds(g * rows + r, rows // DEINTERLEAVE_STRIDE,
                                                      stride=DEINTERLEAVE_STRIDE), :]
                                 for g in range(groups) for r in range(DEINTERLEAVE_STRIDE)]
                        ybuf[slab + c] = jnp.concatenate(parts, axis=0)
                        groups, stride = groups * DEINTERLEAVE_STRIDE, stride // DEINTERLEAVE_STRIDE
                    rows = tm // groups
                    for g in range(groups):
                        for r in range(stride):
                            o_ref[0, g + groups * r, :, out] = (
                                ybuf[slab + c, pl.ds(g * rows + r, rows // stride, stride=stride), :].astype(BF16))
        return finish

    def branch_b(gi, slab):
        dil = B_GROUPS[gi][1]
        lo = gi * B_GROUP_W
        parts = []
        for part, (w_ref, gain_ref) in enumerate(((wqb_ref, gqb_ref), (wkb_ref, gkb_ref), (wvb_ref, None))):
            parts.append((mm(w_ref, lo, lo + B_GROUP_W),
                          finish_b(b_refs[gi], part * B_GROUP_W, gain_ref, dil, slab)))
            slab = (slab + B_GROUP_W // LANES) % N_DILATED_SLABS
        return parts, slab

    by_dilation = sorted(range(len(B_GROUPS)), key=lambda gi: -B_GROUPS[gi][1])
    stages, slab = [], 0
    for gi in by_dilation[:-1]:
        parts, slab = branch_b(gi, slab)
        stages += parts
    stages += [(mm(wqa_ref, 0, A_Q_W), finish_qa), (mm(wkva_ref, 0, 2 * A_KV_W), finish_kva),
               (mm(wqm_ref, 0, M_W), finish_qm)]
    stages += branch_b(by_dilation[-1], slab)[0]

    pending = None
    for matmul, finish in stages:
        y = matmul()
        if pending is not None:
            pending[0](pending[1])
        pending = (finish, y)
    pending[0](pending[1])


def _proj(x, g_mix, ws, gains):
    n, s, _ = x.shape
    tm = PROJ_ROWS
    cur = lambda b, i: (b, i, 0)
    plane = lambda b, i: (b, 0, i, 0)
    in_specs = ([pl.BlockSpec((1, tm, D_MODEL), cur), _resident((1, D_MODEL))]
                + [_resident(w.shape) for w in ws]
                + [_resident((1, LANES)) for _ in gains])
    out_specs = [pl.BlockSpec((1, tm, AM_W), cur)]
    out_shape = [jax.ShapeDtypeStruct((n, s, AM_W), BF16)]
    for _, dil in B_GROUPS:
        out_specs.append(pl.BlockSpec((1, dil, tm // dil, B_QKV_W), plane))
        out_shape.append(jax.ShapeDtypeStruct((n, dil, s // dil, B_QKV_W), BF16))
    return pl.pallas_call(
        _proj_kernel,
        grid=(n, s // tm),
        in_specs=in_specs,
        out_specs=out_specs,
        out_shape=out_shape,
        scratch_shapes=[pltpu.VMEM((N_DILATED_SLABS, tm, LANES), F32)],
        compiler_params=_params(2),
        name="proj",
    )(x, g_mix, *ws, *gains)


MEM_ROWS = 1024


def _mem_kv_kernel(m_ref, g_ref, w_ref, gk_ref, k_ref, v_ref):
    h = _rms_rows(m_ref[...], g_ref[...]).astype(BF16)
    y = jnp.dot(h, w_ref[...], preferred_element_type=F32)
    for c in range(M_HEADS):
        sl = slice(c * LANES, (c + 1) * LANES)
        k_ref[:, sl] = _head_norm128(y[:, sl], gk_ref[...]).astype(BF16)
    v_ref[...] = y[:, M_W:].astype(BF16)


def _mem_kv(mem2d, g_mem, w_mem_kv, gk_m):
    t = mem2d.shape[0]
    tm = min(t, MEM_ROWS)
    row = lambda i: (i, 0)
    return pl.pallas_call(
        _mem_kv_kernel,
        grid=(t // tm,),
        in_specs=[pl.BlockSpec((tm, D_MODEL), row), _resident((1, D_MODEL)),
                  _resident(w_mem_kv.shape), _resident((1, LANES))],
        out_specs=[pl.BlockSpec((tm, M_W), row)] * 2,
        out_shape=[jax.ShapeDtypeStruct((t, M_W), BF16)] * 2,
        compiler_params=_params(1),
        name="mem_kv",
    )(mem2d, g_mem, w_mem_kv, gk_m)


A_BLOCK = A_HALF_WIN
A_STEP = 2048
ATTN_CHUNK = 1024
INTERIOR, FIRST, LAST = 0, 1, 2
LOG2E = float(np.log2(np.e))
LN2 = float(np.log(2.0))

SOFTMAX_SHIFT_LIMIT = 40.0
BF16_SLACK = 1.0 + 2.0 ** -6


def _logit_bound(gain_q, gain_k, head_dim):
    return head_dim * jnp.max(jnp.abs(gain_q)) * jnp.max(jnp.abs(gain_k)) * BF16_SLACK


def _softmax_ctl(shifts):
    shifts = jnp.asarray(shifts, F32).reshape(-1)
    fixed = jnp.all(shifts <= SOFTMAX_SHIFT_LIMIT)
    return jnp.concatenate([fixed.astype(F32).reshape(1), shifts])


def _by_softmax_shift(ctl_ref, run):
    fixed = ctl_ref[0] > 0.5

    @pl.when(fixed)
    def _():
        run(True)

    @pl.when(jnp.logical_not(fixed))
    def _():
        run(False)


def _attn_a_init_bias(bias_ref, ctl_ref):
    n_keys = 3 * A_BLOCK
    q_row = lax.broadcasted_iota(jnp.int32, (A_BLOCK, n_keys), 0)
    k_col = lax.broadcasted_iota(jnp.int32, (A_BLOCK, n_keys), 1)
    dist = jnp.abs(q_row + A_BLOCK - k_col)
    dist_f = dist.astype(F32)
    for head, slope in enumerate(_alibi_slopes(A_HEADS)):
        shift = jnp.where(ctl_ref[0] > 0.5, ctl_ref[1 + head], 0.0)
        base = jnp.where(dist <= A_HALF_WIN, (-slope * LOG2E) * dist_f - shift, NEG_INF)
        bias_ref[INTERIOR, head] = base
        bias_ref[FIRST, head] = jnp.where(k_col >= A_BLOCK, base, NEG_INF)
        bias_ref[LAST, head] = jnp.where(k_col < 2 * A_BLOCK, base, NEG_INF)


def _attn_a_load_kv(kv_refs, kbuf, vbuf):
    kp_ref, kc_ref, kn_ref, vp_ref, vc_ref, vn_ref = kv_refs
    rows = kc_ref.shape[1]
    for buf, prev, cur, nxt in ((kbuf, kp_ref, kc_ref, kn_ref), (vbuf, vp_ref, vc_ref, vn_ref)):
        buf[0:A_BLOCK, :] = prev[0]
        buf[A_BLOCK:A_BLOCK + rows, :] = cur[0]
        buf[A_BLOCK + rows:, :] = nxt[0]


def _attn_a_pass(row0, first_block, n_blocks, fixed_shift, q_ref, kbuf, vbuf, bias_ref, sink_ref, ctl_ref,
                 o_ref):
    n_keys = 3 * A_BLOCK
    low = lax.broadcasted_iota(jnp.int32, (n_keys, LANES), 1) < A_HEAD_DIM
    low_q = lax.broadcasted_iota(jnp.int32, (A_BLOCK, LANES), 1) < A_HEAD_DIM
    zero = jnp.zeros((n_keys, LANES), BF16)
    nt = (((1,), (1,)), ((), ()))
    per_chunk = ATTN_CHUNK // A_BLOCK
    units = [(j, kvh) for j in range(per_chunk) for kvh in range(A_KV_HEADS)]
    logits, values, probs = {}, {}, {}

    def swap_halves(t):
        return jnp.concatenate([t[:, A_HEAD_DIM:], t[:, :A_HEAD_DIM]], axis=1)

    def padded(t, t_sw, kvh):
        lo, hi = (t, t_sw) if kvh == 0 else (t_sw, t)
        return jnp.concatenate([jnp.where(low, lo, zero), jnp.where(low, zero, hi)], axis=0)

    def logit_matmuls():
        for j in range(per_chunk):
            rows = pl.ds(row0 + j * A_BLOCK, A_BLOCK)
            kw = kbuf[pl.ds(row0 + j * A_BLOCK, n_keys), :]
            vw = vbuf[pl.ds(row0 + j * A_BLOCK, n_keys), :]
            kw_sw, vw_sw = swap_halves(kw), swap_halves(vw)
            for kvh in range(A_KV_HEADS):
                q2 = jnp.concatenate([q_ref[0, rows, (2 * kvh + c) * LANES:(2 * kvh + c + 1) * LANES]
                                      for c in range(2)], axis=0)
                logits[j, kvh] = lax.dot_general(q2, padded(kw, kw_sw, kvh), nt,
                                                 preferred_element_type=F32)
                values[j, kvh] = padded(vw, vw_sw, kvh)

    def softmaxes():
        for j, kvh in units:
            block = first_block + j
            variant = jnp.where(block == 0, FIRST, jnp.where(block == n_blocks - 1, LAST, INTERIOR))
            s2 = logits[j, kvh]
            p_rows, r_rows = [], []
            for c in range(2):
                ps, sums, sink_gaps = [], [], []
                for half in range(2):
                    head = 2 * (2 * kvh + c) + half
                    sh = s2[c * A_BLOCK:(c + 1) * A_BLOCK, half * n_keys:(half + 1) * n_keys]
                    sh = sh + bias_ref[variant, head]
                    sink = sink_ref[head] * LOG2E
                    if fixed_shift:
                        m = ctl_ref[1 + head]
                        p = jnp.exp2(sh)
                    else:
                        m = jnp.maximum(jnp.max(sh, axis=-1, keepdims=True), sink)
                        p = jnp.exp2(sh - m)
                    ps.append(p.astype(BF16))
                    sums.append(jnp.sum(p, axis=-1, keepdims=True))
                    sink_gaps.append(sink - m)
                sink_lanes = low_q[:1] if fixed_shift else low_q
                den = (jnp.where(low_q, sums[0], sums[1])
                       + jnp.exp2(jnp.where(sink_lanes, sink_gaps[0], sink_gaps[1])))
                p_rows.append(jnp.concatenate(ps, axis=1))
                r_rows.append(1.0 / den)
            probs[j, kvh] = (jnp.concatenate(p_rows, axis=0), jnp.concatenate(r_rows, axis=0))

    def value_matmuls():
        for j, kvh in units:
            p2, r2 = probs[j, kvh]
            o2 = jnp.dot(p2, values[j, kvh], preferred_element_type=F32) * r2
            for c in range(2):
                tile = 2 * kvh + c
                o_ref[0, pl.ds(row0 + j * A_BLOCK, A_BLOCK), tile * LANES:(tile + 1) * LANES] = (
                    o2[c * A_BLOCK:(c + 1) * A_BLOCK].astype(BF16))

    return logit_matmuls, softmaxes, value_matmuls


def _attn_a_kernel(seq_len, q_ref, kp_ref, kc_ref, kn_ref, vp_ref, vc_ref, vn_ref, sink_ref, ctl_ref,
                   o_ref, kbuf, vbuf, bias_ref):
    i = pl.program_id(1)
    step_rows = q_ref.shape[1]

    @pl.when((pl.program_id(0) == 0) & (i == 0))
    def _():
        _attn_a_init_bias(bias_ref, ctl_ref)

    _attn_a_load_kv((kp_ref, kc_ref, kn_ref, vp_ref, vc_ref, vn_ref), kbuf, vbuf)

    def run(fixed_shift):
        def chunk(t, carry):
            row0 = pl.multiple_of(t * ATTN_CHUNK, ATTN_CHUNK)
            for phase in _attn_a_pass(row0, (i * step_rows + row0) // A_BLOCK, seq_len // A_BLOCK,
                                      fixed_shift, q_ref, kbuf, vbuf, bias_ref, sink_ref, ctl_ref, o_ref):
                phase()
            return carry

        lax.fori_loop(0, step_rows // ATTN_CHUNK, chunk, 0)

    _by_softmax_shift(ctl_ref, run)


def _attn_a_specs(s, step_rows, position):
    per_step = step_rows // A_BLOCK
    n_blocks = s // A_BLOCK
    assert n_blocks >= 2 and s % step_rows == 0

    def cur(col):
        def index(*ids):
            b, i = position(*ids)
            return (b, i, col)
        return index

    def prev(col):
        def index(*ids):
            b, i = position(*ids)
            return (b, jnp.maximum(i * per_step - 1, 0), col)
        return index

    def nxt(col):
        def index(*ids):
            b, i = position(*ids)
            return (b, jnp.minimum((i + 1) * per_step, n_blocks - 1), col)
        return index

    q_spec = pl.BlockSpec((1, step_rows, A_Q_W), cur(AM_QA // A_Q_W))
    o_spec = pl.BlockSpec((1, step_rows, A_Q_W), cur(0))
    kv_specs = [[pl.BlockSpec((1, A_BLOCK, A_KV_W), prev(col // A_KV_W)),
                 pl.BlockSpec((1, step_rows, A_KV_W), cur(col // A_KV_W)),
                 pl.BlockSpec((1, A_BLOCK, A_KV_W), nxt(col // A_KV_W))] for col in (AM_KA, AM_VA)]
    scratch = [pltpu.VMEM((step_rows + 2 * A_BLOCK, A_KV_W), BF16)] * 2 + [
        pltpu.VMEM((3, A_HEADS, A_BLOCK, 3 * A_BLOCK), F32)]
    return q_spec, o_spec, kv_specs, scratch


def _attn_a(am, sink, logit_bound):
    n, s, _ = am.shape
    step = min(s, A_STEP)
    q_spec, o_spec, kv_specs, scratch = _attn_a_specs(s, step, lambda b, i: (b, i))
    smem = pl.BlockSpec(memory_space=pltpu.SMEM)
    ctl = _softmax_ctl(jnp.maximum(logit_bound, sink * LOG2E))
    return pl.pallas_call(
        functools.partial(_attn_a_kernel, s),
        grid=(n, s // step),
        in_specs=[q_spec] + kv_specs[0] + kv_specs[1] + [smem, smem],
        out_specs=o_spec,
        out_shape=jax.ShapeDtypeStruct((n, s, A_Q_W), BF16),
        scratch_shapes=scratch,
        compiler_params=_params(2),
        name="attn_a",
    )(*[am] * 7, sink, ctl)


B_BLOCK = 2 * B_HALF_WIN
B_KEYS = B_BLOCK + 2 * B_HALF_WIN
B_ROWS_PER_STEP = 2048


def _attn_b_kernel(sub_len, slopes, whole_planes, q_ref, *refs):
    if whole_planes:
        kc_ref, vc_ref, ctl_ref, o_ref, lse_ref, kbuf, vbuf, bias_ref = refs
        kp_ref = kn_ref = vp_ref = vn_ref = None
    else:
        kp_ref, kc_ref, kn_ref, vp_ref, vc_ref, vn_ref, ctl_ref, o_ref, lse_ref, kbuf, vbuf, bias_ref = refs
    i = pl.program_id(2)
    n_planes, step = q_ref.shape[1], q_ref.shape[2]
    n_blocks = sub_len // B_BLOCK

    @pl.when((pl.program_id(0) == 0) & (pl.program_id(1) == 0) & (i == 0))
    def _():
        shift = jnp.where(ctl_ref[0] > 0.5, ctl_ref[1], 0.0)
        q_row = lax.broadcasted_iota(jnp.int32, (B_BLOCK, B_KEYS), 0)
        k_col = lax.broadcasted_iota(jnp.int32, (B_BLOCK, B_KEYS), 1)
        dist = jnp.abs(q_row + B_HALF_WIN - k_col)
        dist_f = dist.astype(F32)
        for head, slope in enumerate(slopes):
            base = jnp.where(dist <= B_HALF_WIN, (-slope * LOG2E) * dist_f - shift, NEG_INF)
            first = jnp.where(k_col >= B_HALF_WIN, base, NEG_INF)
            bias_ref[0, head] = base
            bias_ref[1, head] = first
            bias_ref[2, head] = jnp.where(k_col < B_HALF_WIN + B_BLOCK, base, NEG_INF)
            bias_ref[3, head] = jnp.where(k_col < B_HALF_WIN + B_BLOCK, first, NEG_INF)

    no_rows = jnp.zeros((B_HALF_WIN, B_GROUP_W), BF16)
    for buf, prev, cur, nxt in ((kbuf, kp_ref, kc_ref, kn_ref), (vbuf, vp_ref, vc_ref, vn_ref)):
        for r in range(n_planes):
            buf[r, 0:B_HALF_WIN, :] = no_rows if prev is None else prev[0, r]
            buf[r, B_HALF_WIN:B_HALF_WIN + step, :] = cur[0, r]
            buf[r, B_HALF_WIN + step:, :] = no_rows if nxt is None else nxt[0, r]

    lane = lax.broadcasted_iota(jnp.int32, (B_BLOCK, LANES), 1)
    lanes_per_head = LANES // B_HEADS_PER_GROUP
    heads = [slice(h * B_HEAD_DIM, (h + 1) * B_HEAD_DIM) for h in range(B_HEADS_PER_GROUP)]
    nt = (((1,), (1,)), ((), ()))
    rows_per_pass = min(step, ATTN_CHUNK)
    planes_per_pass = ATTN_CHUNK // rows_per_pass
    passes_per_plane = step // rows_per_pass
    blocks = [(dr, j) for dr in range(planes_per_pass) for j in range(rows_per_pass // B_BLOCK)]

    def one_pass(fixed_shift, t, carry):
        if planes_per_pass == 1:
            plane0 = t // passes_per_plane
            row0 = pl.multiple_of((t % passes_per_plane) * ATTN_CHUNK, ATTN_CHUNK)
        else:
            plane0, row0 = t * planes_per_pass, 0

        logits = {}
        for dr, j in blocks:
            for h, sl in enumerate(heads):
                q = q_ref[0, plane0 + dr, pl.ds(row0 + j * B_BLOCK, B_BLOCK), sl]
                k = kbuf[plane0 + dr, pl.ds(row0 + j * B_BLOCK, B_KEYS), sl]
                logits[dr, j, h] = lax.dot_general(q, k, nt, preferred_element_type=F32)

        probs = {}
        for dr, j in blocks:
            block = (i * step + row0) // B_BLOCK + j
            variant = (block == 0).astype(jnp.int32) + 2 * (block == n_blocks - 1).astype(jnp.int32)
            m_tile = den_tile = None
            for h in range(B_HEADS_PER_GROUP):
                s = logits[dr, j, h] + bias_ref[variant, h]
                if fixed_shift:
                    m = ctl_ref[1]
                    p = jnp.exp2(s)
                else:
                    m = jnp.max(s, axis=-1, keepdims=True)
                    p = jnp.exp2(s - m)
                den = jnp.sum(p, axis=-1, keepdims=True)
                probs[dr, j, h] = (p.astype(BF16), 1.0 / den)
                den_tile = den if h == 0 else jnp.where(lane >= h * lanes_per_head, den, den_tile)
                if not fixed_shift:
                    m_tile = m if h == 0 else jnp.where(lane >= h * lanes_per_head, m, m_tile)
            lse_ref[0, plane0 + dr, pl.ds(row0 + j * B_BLOCK, B_BLOCK), :] = (
                ((m if fixed_shift else m_tile) + jnp.log2(den_tile)) * LN2)

        for dr, j in blocks:
            for h, sl in enumerate(heads):
                p, rden = probs[dr, j, h]
                v = vbuf[plane0 + dr, pl.ds(row0 + j * B_BLOCK, B_KEYS), sl]
                o = jnp.dot(p, v, preferred_element_type=F32) * rden
                o_ref[0, plane0 + dr, pl.ds(row0 + j * B_BLOCK, B_BLOCK), sl] = o.astype(BF16)
        return carry

    _by_softmax_shift(ctl_ref, lambda fixed_shift: lax.fori_loop(
        0, n_planes * step // ATTN_CHUNK, functools.partial(one_pass, fixed_shift), 0))


def _attn_b_group(qkv, gi, logit_bound):
    n, dil, sub, _ = qkv.shape
    step = min(sub, B_ROWS_PER_STEP)
    n_planes = B_ROWS_PER_STEP // step
    assert sub % step == 0 and dil % n_planes == 0 and step % B_BLOCK == 0
    assert ATTN_CHUNK % step == 0 or step % ATTN_CHUNK == 0
    halo_per_step = step // B_HALF_WIN
    last_halo = sub // B_HALF_WIN - 1
    cur = lambda col: lambda b, r, i: (b, r, i, col)
    prev = lambda col: lambda b, r, i: (b, r, jnp.maximum(i * halo_per_step - 1, 0), col)
    nxt = lambda col: lambda b, r, i: (b, r, jnp.minimum((i + 1) * halo_per_step, last_halo), col)
    whole_planes = step == sub
    kv_specs = []
    for col in (1, 2):
        block = pl.BlockSpec((1, n_planes, step, B_GROUP_W), cur(col))
        kv_specs += [block] if whole_planes else [
            pl.BlockSpec((1, n_planes, B_HALF_WIN, B_GROUP_W), prev(col)), block,
            pl.BlockSpec((1, n_planes, B_HALF_WIN, B_GROUP_W), nxt(col))]
    all_slopes = _alibi_slopes(len(B_GROUPS) * B_HEADS_PER_GROUP)
    slopes = [all_slopes[gi * B_HEADS_PER_GROUP + h] * dil for h in range(B_HEADS_PER_GROUP)]
    return pl.pallas_call(
        functools.partial(_attn_b_kernel, sub, slopes, whole_planes),
        grid=(n, dil // n_planes, sub // step),
        in_specs=[pl.BlockSpec((1, n_planes, step, B_GROUP_W), cur(0))] + kv_specs
                 + [pl.BlockSpec(memory_space=pltpu.SMEM)],
        out_specs=[pl.BlockSpec((1, n_planes, step, B_GROUP_W), cur(0)),
                   pl.BlockSpec((1, n_planes, step, LANES), cur(0))],
        out_shape=[jax.ShapeDtypeStruct((n, dil, sub, B_GROUP_W), BF16),
                   jax.ShapeDtypeStruct((n, dil, sub, LANES), F32)],
        scratch_shapes=[pltpu.VMEM((n_planes, step + 2 * B_HALF_WIN, B_GROUP_W), BF16)] * 2
                       + [pltpu.VMEM((4, B_HEADS_PER_GROUP, B_BLOCK, B_KEYS), F32)],
        compiler_params=_params(3),
        name=f"attn_b{gi}",
    )(*[qkv] * (1 + len(kv_specs)), _softmax_ctl(logit_bound))


M_STEP = 2048


def _attn_m_kernel(q_ref, k_ref, v_ref, ctl_ref, o_ref):
    heads = [slice(h * M_HEAD_DIM, (h + 1) * M_HEAD_DIM) for h in range(M_HEADS)]
    nt = (((1,), (1,)), ((), ()))

    def chunk(fixed_shift, t, carry):
        rows = pl.ds(pl.multiple_of(t * ATTN_CHUNK, ATTN_CHUNK), ATTN_CHUNK)
        logits = [lax.dot_general(q_ref[0, rows, sl], k_ref[0, :, sl], nt, preferred_element_type=F32)
                  for sl in heads]
        probs = []
        for s in logits:
            m = ctl_ref[1] if fixed_shift else jnp.max(s, axis=-1, keepdims=True)
            p = jnp.exp2(s - m)
            probs.append((p.astype(BF16), 1.0 / jnp.sum(p, axis=-1, keepdims=True)))
        for sl, (p, rden) in zip(heads, probs):
            o = jnp.dot(p, v_ref[0, :, sl], preferred_element_type=F32) * rden
            o_ref[0, rows, sl] = o.astype(BF16)
        return carry

    _by_softmax_shift(ctl_ref, lambda fixed_shift: lax.fori_loop(
        0, q_ref.shape[1] // ATTN_CHUNK, functools.partial(chunk, fixed_shift), 0))


def _attn_m(am, mk, mv, logit_bound):
    n, s, _ = am.shape
    step = min(s, M_STEP)
    cur = lambda b, i: (b, i, 0)
    q_cols = lambda b, i: (b, i, AM_QM // M_W)
    mem = lambda b, i: (b, 0, 0)
    return pl.pallas_call(
        _attn_m_kernel,
        grid=(n, s // step),
        in_specs=[pl.BlockSpec((1, step, M_W), q_cols), pl.BlockSpec((1, N_MEM, M_W), mem),
                  pl.BlockSpec((1, N_MEM, M_W), mem), pl.BlockSpec(memory_space=pltpu.SMEM)],
        out_specs=pl.BlockSpec((1, step, M_W), cur),
        out_shape=jax.ShapeDtypeStruct((n, s, M_W), BF16),
        compiler_params=_params(2),
        name="attn_m",
    )(am, mk, mv, _softmax_ctl(logit_bound))


MERGE_ROWS = 512
assert all(dil <= DEINTERLEAVE_STRIDE ** 2 for _, dil in B_GROUPS)
N_MERGE_SLABS = sum((B_GROUP_W // LANES + 1) * (1 if dil <= DEINTERLEAVE_STRIDE else 2)
                    for _, dil in B_GROUPS if dil > 1)


def _merge_kernel(x_ref, g_ref, oa_ref, o0_ref, o1_ref, o2_ref, l0_ref, l1_ref, l2_ref, om_ref,
                  wg_ref, bg_ref, wbr_ref, wout_ref, y_ref, nat):
    tm = x_ref.shape[1]
    x = x_ref[0]
    h = _rms_rows(x, g_ref[...]).astype(BF16)
    tiles = B_GROUP_W // LANES

    o_nat, lse_nat = [], []
    slab = 0
    for (_, dil), o_ref, l_ref in zip(B_GROUPS, (o0_ref, o1_ref, o2_ref), (l0_ref, l1_ref, l2_ref)):
        if dil == 1:
            o_nat.append([o_ref[0, 0, :, c * LANES:(c + 1) * LANES].astype(F32) for c in range(tiles)])
            lse_nat.append(l_ref[0, 0])
            continue
        def plane_tile(r, c):
            if c == tiles:
                return l_ref[0, r]
            return o_ref[0, r, :, c * LANES:(c + 1) * LANES].astype(F32)

        for c in range(tiles + 1):
            if dil <= DEINTERLEAVE_STRIDE:
                for r in range(dil):
                    nat[slab + c, pl.ds(r, tm // dil, stride=dil), :] = plane_tile(r, c)
                continue
            groups = DEINTERLEAVE_STRIDE
            inner, rows = dil // groups, tm // groups
            spare = slab + tiles + 1 + c
            for g in range(groups):
                for r in range(inner):
                    nat[spare, pl.ds(g * rows + r, rows // inner, stride=inner), :] = plane_tile(g + groups * r, c)
            for g in range(groups):
                nat[slab + c, pl.ds(g, rows, stride=groups), :] = nat[spare, g * rows:(g + 1) * rows, :]
        o_nat.append([nat[slab + c] for c in range(tiles)])
        lse_nat.append(nat[slab + tiles])
        slab += (tiles + 1) * (1 if dil <= DEINTERLEAVE_STRIDE else 2)

    top = jnp.maximum(jnp.maximum(lse_nat[0], lse_nat[1]), lse_nat[2])
    es = [jnp.exp(l - top) for l in lse_nat]
    r_sum = 1.0 / (es[0] + es[1] + es[2])
    ws = [e * r_sum for e in es]
    lanes_per_head = LANES // B_HEADS_PER_GROUP
    ob_heads = []
    for hd in range(B_HEADS_PER_GROUP):
        acc = None
        for w, o in zip(ws, o_nat):
            term = w[:, hd * lanes_per_head:hd * lanes_per_head + 1] * o[hd]
            acc = term if acc is None else acc + term
        ob_heads.append(acc.astype(BF16))
    ob = jnp.concatenate(ob_heads, axis=1)

    z = None
    for b, o in enumerate((oa_ref[0], ob, om_ref[0])):
        logits = jnp.dot(h, wg_ref[:, b * D_MODEL:(b + 1) * D_MODEL], preferred_element_type=F32)
        gate = 0.5 * jnp.tanh(0.5 * (logits + bg_ref[b:b + 1, :])) + 0.5
        term = gate * jnp.dot(o, wbr_ref[b], preferred_element_type=F32)
        z = term if z is None else z + term
    y_ref[0] = x + jnp.dot(z.astype(BF16), wout_ref[...], preferred_element_type=F32)


def _merge(x, g_mix, oa, obs, lses, om, w_gate, b_gate, w_branch, w_out):
    n, s, _ = x.shape
    tm = MERGE_ROWS
    cur = lambda b, i: (b, i, 0)
    plane = lambda b, i: (b, 0, i, 0)
    rows = lambda w: pl.BlockSpec((1, tm, w), cur)
    planes = lambda w: [pl.BlockSpec((1, dil, tm // dil, w), plane) for _, dil in B_GROUPS]
    return pl.pallas_call(
        _merge_kernel,
        grid=(n, s // tm),
        in_specs=[rows(D_MODEL), _resident((1, D_MODEL)), rows(A_Q_W)]
                 + planes(B_GROUP_W) + planes(LANES) + [rows(M_W)]
                 + [_resident(w_gate.shape), _resident(b_gate.shape), _resident(w_branch.shape),
                    _resident(w_out.shape)],
        out_specs=rows(D_MODEL),
        out_shape=jax.ShapeDtypeStruct((n, s, D_MODEL), F32),
        scratch_shapes=[pltpu.VMEM((N_MERGE_SLABS, tm, LANES), F32)],
        compiler_params=_params(2),
        name="merge",
    )(x, g_mix, oa, *obs, *lses, om, w_gate, b_gate, w_branch, w_out)


MLP_ROWS = 1024
MLP_FF_CHUNK = 1024


def _mlp_kernel(x_ref, g_ref, wup_ref, wdn_ref, y_ref):
    x = x_ref[...]
    h = _rms_rows(x, g_ref[...]).astype(BF16)
    acc = x
    for lo in range(0, D_FF, MLP_FF_CHUNK):
        u = jnp.dot(h, wup_ref[:, lo:lo + MLP_FF_CHUNK], preferred_element_type=F32)
        a = jnp.square(jnp.maximum(u, 0.0)).astype(BF16)
        acc = acc + jnp.dot(a, wdn_ref[lo:lo + MLP_FF_CHUNK, :], preferred_element_type=F32)
    y_ref[...] = acc


def _mlp(x2d, g_mlp, w_up, w_down):
    t = x2d.shape[0]
    tm = MLP_ROWS
    row = lambda i: (i, 0)
    return pl.pallas_call(
        _mlp_kernel,
        grid=(t // tm,),
        in_specs=[pl.BlockSpec((tm, D_MODEL), row), _resident((1, D_MODEL)),
                  _resident(w_up.shape), _resident(w_down.shape)],
        out_specs=pl.BlockSpec((tm, D_MODEL), row),
        out_shape=jax.ShapeDtypeStruct((t, D_MODEL), F32),
        compiler_params=_params(1),
        name="mlp",
    )(x2d, g_mlp, w_up, w_down)


def _tile2(g):
    return jnp.concatenate([g, g]).reshape(1, LANES)


def _row(g):
    return g.reshape(1, -1)


def _layer(x, mem, g_mix, g_mem, w_in, b_gate, w_mem_kv, gq_a, gk_a, sink_a, gq_b, gk_b, gq_m, gk_m,
           w_branch, w_out, g_mlp, w_up, w_down):
    n, s, _ = x.shape
    bounds = np.cumsum((0, A_Q_W, A_KV_W, A_KV_W, B_W, B_W, B_W, M_W, GATE_W))
    seg = lambda a, b: w_in[:, int(bounds[a]):int(bounds[b])].astype(BF16)
    ws = (seg(0, 1), seg(1, 3), seg(3, 4), seg(4, 5), seg(5, 6), seg(6, 7))
    w_gate = seg(7, 8)
    gq_a, gq_b, gq_m = (gq_a * (A_HEAD_DIM ** -0.5 * LOG2E), gq_b * (B_HEAD_DIM ** -0.5 * LOG2E),
                        gq_m * (M_HEAD_DIM ** -0.5 * LOG2E))
    gains = (_tile2(gq_a), _tile2(gk_a), _row(gq_b), _row(gk_b), _row(gq_m))

    am, *qkv_b = _proj(x, _row(g_mix), ws, gains)
    mk, mv = _mem_kv(mem.reshape(n * N_MEM, D_MODEL), _row(g_mem), w_mem_kv.astype(BF16), _row(gk_m))

    oa = _attn_a(am, sink_a, _logit_bound(gq_a, gk_a, A_HEAD_DIM))
    bound_b = _logit_bound(gq_b, gk_b, B_HEAD_DIM)
    b_out = [_attn_b_group(qkv_b[gi], gi, bound_b) for gi in range(len(B_GROUPS))]
    om = _attn_m(am, mk.reshape(n, N_MEM, M_W), mv.reshape(n, N_MEM, M_W),
                 _logit_bound(gq_m, gk_m, M_HEAD_DIM))

    x1 = _merge(x, _row(g_mix), oa, [o for o, _ in b_out], [l for _, l in b_out], om,
                w_gate, b_gate, w_branch.astype(BF16), w_out.astype(BF16))
    y = _mlp(x1.reshape(n * s, D_MODEL), _row(g_mlp), w_up.astype(BF16), w_down.astype(BF16))
    return y.reshape(n, s, D_MODEL)


def kernel(x_prompt, x_sample, mem_prompt, mem_sample, g_mix, g_mem, w_in, b_gate, w_mem_kv, gq_a, gk_a,
           sink_a, gq_b, gk_b, gq_m, gk_m, w_branch, w_out, g_mlp, w_up, w_down):
    depth = w_in.shape[0]

    def run(x, mem):
        for l in range(depth):
            x = _layer(x, mem, g_mix[l], g_mem[l], w_in[l], b_gate[l], w_mem_kv[l], gq_a[l], gk_a[l],
                       sink_a[l], gq_b[l], gk_b[l], gq_m[l], gk_m[l], w_branch[l], w_out[l], g_mlp[l],
                       w_up[l], w_down[l])
        return x

    return (run(x_prompt, mem_prompt), run(x_sample, mem_sample))
```
